```python
import math
import jax, jax.numpy as jnp
from jax import lax
import numpy as np


D_MODEL = 1024
BATCH = 8
SEQ = 2048
DEPTH = 2
DEC_BATCH = 32
DEC_SEQ = 1
PAST_LEN = 8192
PAGE_SIZE = 128

A_HEADS = 4
A_HD = 64
A_VD = 2 * A_HD
A_QK_W = A_HEADS * 2 * A_HD
A_V_W = A_HEADS * A_VD
B_W = D_MODEL // 4
B_K = 3
C_W = D_MODEL // 4
C_K = 31
D_W = D_MODEL // 4
D_GROUPS = 4
CHUNK = 128
N_BRANCH = 4
IN_SIZES = (A_QK_W, A_QK_W, A_V_W, B_W, B_W, B_W, 2 * C_W, 2 * D_W, N_BRANCH * D_MODEL)
IN_COLS = sum(IN_SIZES)
N_MEM = 256
X_HEADS = 4
X_HD = 128
X_W = X_HEADS * X_HD
D_FF = 4 * D_MODEL
Q_BLOCK = 128
EPS = 1e-6
NEG_INF = -1e30

kernel_name = 'hybrid_gated_diffattn_conv_gmlp_decoder_step'


def rms_norm(x, g):
    xf = x.astype(jnp.float32)
    y = xf * lax.rsqrt(jnp.mean(xf * xf, axis=-1, keepdims=True) + EPS)
    return (y * g.astype(jnp.float32)).astype(x.dtype)


def layer_norm(x, g, b):
    xf = x.astype(jnp.float32)
    xc = xf - jnp.mean(xf, axis=-1, keepdims=True)
    var = jnp.mean(xc * xc, axis=-1, keepdims=True)
    return (xc * lax.rsqrt(var + EPS) * g.astype(jnp.float32) + b.astype(jnp.float32)).astype(x.dtype)


def alibi_slopes(n_heads):
    return jnp.asarray([2.0 ** (-8.0 * (i + 1) / n_heads) for i in range(n_heads)], dtype=jnp.float32)


def causal_dwconv(z, prev, w, b):
    width = w.shape[0]
    zp = jnp.concatenate([prev.astype(z.dtype), z], axis=1)
    y = lax.conv_general_dilated(zp, w.astype(z.dtype)[:, None, :], window_strides=(1,), padding='VALID',
                                 dimension_numbers=('NWC', 'WIO', 'NWC'), feature_group_count=z.shape[-1])
    return y + b.astype(z.dtype), zp[:, zp.shape[1] - (width - 1):]


def diff_attention(q1, q2, k1, k2, v, q_pos, k_pos, slopes, lam):
    bsz, lq, nh, d = q1.shape
    blk = min(Q_BLOCK, lq)
    nblk = -(-lq // blk)
    pad = nblk * blk - lq
    scale = d ** -0.5

    def prep(q):
        q = jnp.pad(q.astype(jnp.float32), ((0, 0), (0, pad), (0, 0), (0, 0))) * scale
        return q.reshape(bsz, nblk, blk, nh, d).transpose(1, 0, 2, 3, 4)

    qs1, qs2 = prep(q1), prep(q2)
    qp = jnp.pad(q_pos, (0, pad), mode='edge').reshape(nblk, blk)
    k1f, k2f, vf = k1.astype(jnp.float32), k2.astype(jnp.float32), v.astype(jnp.float32)

    def block(args):
        qb1, qb2, pb = args
        dist = (pb[:, None] - k_pos[None, :]).astype(jnp.float32)
        bias = jnp.where(dist[None] >= 0, -slopes[:, None, None] * dist[None], NEG_INF)
        p1 = jax.nn.softmax(jnp.einsum('bqhd,bkhd->bhqk', qb1, k1f) + bias, axis=-1)
        p2 = jax.nn.softmax(jnp.einsum('bqhd,bkhd->bhqk', qb2, k2f) + bias, axis=-1)
        return jnp.einsum('bhqk,bkhe->bqhe', p1 - lam * p2, vf)

    out = lax.map(block, (qs1, qs2, qp))
    return out.transpose(1, 0, 2, 3, 4).reshape(bsz, nblk * blk, nh, -1)[:, :lq]


def chunk_spatial_gate(v, ws, bs):
    bsz, t, c = v.shape
    nch = -(-t // CHUNK)
    pad = nch * CHUNK - t
    vp = jnp.pad(v, ((0, 0), (0, pad), (0, 0))).reshape(bsz, nch, CHUNK, D_GROUPS, c // D_GROUPS)
    w = ws * jnp.tril(jnp.ones((CHUNK, CHUNK), ws.dtype))
    s = jnp.einsum('gts,bnsgc->bntgc', w, vp) + bs.T[:, :, None]
    return s.reshape(bsz, nch * CHUNK, c)[:, :t]


def memory_kv(mem, mem_norm_g, w_xk, w_xv, x_knorm_g):
    bsz, n, _ = mem.shape
    mem_n = rms_norm(mem, mem_norm_g)
    mk = rms_norm((mem_n @ w_xk).reshape(bsz, n, X_HEADS, X_HD), x_knorm_g)
    mv = (mem_n @ w_xv).reshape(bsz, n, X_HEADS, X_HD)
    return mk, mv


def decoder_layer(x, lw, layer_idx, past_k, past_v, prev_b, prev_c, mem_k, mem_v):
    bsz, t, _ = x.shape
    p_len = 0 if past_k is None else past_k.shape[1]
    q_pos = jnp.arange(p_len, p_len + t, dtype=jnp.int32)
    k_pos = jnp.arange(p_len + t, dtype=jnp.int32)
    lam_init = 0.8 - 0.6 * math.exp(-0.3 * layer_idx)

    h = rms_norm(x, lw['norm_mix_g'])
    z = h @ lw['w_in']
    offs = np.cumsum(IN_SIZES)[:-1].tolist()
    qa, ka, va, bx, bb, bc, cz, dz, gz = jnp.split(z, offs, axis=-1)

    qa = rms_norm(qa.reshape(bsz, t, A_HEADS, 2, A_HD), lw['a_qnorm_g'])
    ka = rms_norm(ka.reshape(bsz, t, A_HEADS, 2, A_HD), lw['a_knorm_g'])
    k_rows = ka.reshape(bsz, t, A_HEADS, 2 * A_HD)
    v_rows = va.reshape(bsz, t, A_HEADS, A_VD)
    if past_k is None:
        k_all, v_all = k_rows, v_rows
    else:
        k_all = jnp.concatenate([past_k.astype(x.dtype), k_rows], axis=1)
        v_all = jnp.concatenate([past_v.astype(x.dtype), v_rows], axis=1)
    k_all = k_all.reshape(bsz, p_len + t, A_HEADS, 2, A_HD)
    lv = lw['a_lam'].astype(jnp.float32)
    lam = jnp.exp(jnp.sum(lv[0] * lv[1])) - jnp.exp(jnp.sum(lv[2] * lv[3])) + lam_init
    o = diff_attention(qa[..., 0, :], qa[..., 1, :], k_all[..., 0, :], k_all[..., 1, :], v_all,
                       q_pos, k_pos, alibi_slopes(A_HEADS), lam)
    o = rms_norm(o, lw['a_subln_g']) * (1.0 - lam_init)
    ya = o.reshape(bsz, t, A_V_W).astype(x.dtype) @ lw['w_a_out']

    conv_b, new_prev_b = causal_dwconv(bc * bx, prev_b, lw['b_conv_w'], lw['b_conv_b'])
    yb = (bb * conv_b) @ lw['w_b_out']

    ca, cg = jnp.split(cz, 2, axis=-1)
    conv_c, new_prev_c = causal_dwconv(ca * jax.nn.sigmoid(cg), prev_c, lw['c_conv_w'], lw['c_conv_b'])
    yc = jax.nn.silu(layer_norm(conv_c, lw['c_ln_g'], lw['c_ln_b'])) @ lw['w_c_out']

    du, dv = jnp.split(jax.nn.gelu(dz), 2, axis=-1)
    dvn = layer_norm(dv, lw['d_ln_g'], lw['d_ln_b'])
    yd = (du * chunk_spatial_gate(dvn, lw['d_ws'], lw['d_bs'])) @ lw['w_d_out']

    g = jax.nn.sigmoid(gz.reshape(bsz, t, N_BRANCH, D_MODEL))
    merged = g[:, :, 0] * ya + g[:, :, 1] * yb + g[:, :, 2] * yc + g[:, :, 3] * yd
    x = x + merged @ lw['w_o']

    h2 = rms_norm(x, lw['norm_x_g'])
    q = rms_norm((h2 @ lw['w_xq']).reshape(bsz, t, X_HEADS, X_HD), lw['x_qnorm_g'])
    s = jnp.einsum('bqhd,bkhd->bhqk', q.astype(jnp.float32), mem_k.astype(jnp.float32)) * (X_HD ** -0.5)
    pm = jax.nn.softmax(s, axis=-1)
    om = jnp.einsum('bhqk,bkhd->bqhd', pm, mem_v.astype(jnp.float32)).astype(x.dtype)
    x = x + om.reshape(bsz, t, X_W) @ lw['w_xo']

    h3 = rms_norm(x, lw['norm_ffn_g'])
    x = x + jnp.square(jax.nn.relu(h3 @ lw['w_up'])) @ lw['w_down']
    return x, (k_rows, v_rows, new_prev_b, new_prev_c, dvn)


def setup_inputs(seed: int = 0) -> dict:
    key = jax.random.key(seed)
    keys = iter(jax.random.split(key, 64))

    def nrm(shape, scale):
        return jax.random.normal(next(keys), shape, jnp.float32) * scale

    def gain(shape):
        return 1.0 + nrm(shape, 0.05)

    n_pages = PAST_LEN // PAGE_SIZE
    n_used = DEC_BATCH * n_pages
    n_pool = n_used + max(1, n_used // 4)
    page_table = jax.random.permutation(next(keys), n_pool)[:n_used].reshape(DEC_BATCH, n_pages).astype(jnp.int32)
    L = DEPTH
    return {
        'x_prompt': nrm((BATCH, SEQ, D_MODEL), 1.0),
        'x_sample': nrm((DEC_BATCH, DEC_SEQ, D_MODEL), 1.0),
        'cache_k_a': nrm((L, n_pool, PAGE_SIZE, A_HEADS, 2 * A_HD), 1.0),
        'cache_v_a': nrm((L, n_pool, PAGE_SIZE, A_HEADS, A_VD), 1.0),
        'state_conv_b': nrm((L, DEC_BATCH, B_K - 1, B_W), 0.5),
        'state_conv_c': nrm((L, DEC_BATCH, C_K - 1, C_W), 0.5),
        'cache_mem_k': nrm((L, DEC_BATCH, N_MEM, X_HEADS, X_HD), 1.0),
        'cache_mem_v': nrm((L, DEC_BATCH, N_MEM, X_HEADS, X_HD), 1.0),
        'page_table': page_table,
        'mem_prompt': nrm((BATCH, N_MEM, D_MODEL), 1.0),
        'norm_mix_g': gain((L, D_MODEL)),
        'w_in': nrm((L, D_MODEL, IN_COLS), D_MODEL ** -0.5),
        'a_qnorm_g': gain((L, A_HD)),
        'a_knorm_g': gain((L, A_HD)),
        'a_lam': nrm((L, 4, A_HD), 0.1),
        'a_subln_g': gain((L, A_VD)),
        'w_a_out': nrm((L, A_V_W, D_MODEL), A_V_W ** -0.5),
        'b_conv_w': nrm((L, B_K, B_W), B_K ** -0.5),
        'b_conv_b': nrm((L, B_W), 0.02),
        'w_b_out': nrm((L, B_W, D_MODEL), B_W ** -0.5),
        'c_conv_w': nrm((L, C_K, C_W), C_K ** -0.5),
        'c_conv_b': nrm((L, C_W), 0.02),
        'c_ln_g': gain((L, C_W)),
        'c_ln_b': nrm((L, C_W), 0.02),
        'w_c_out': nrm((L, C_W, D_MODEL), C_W ** -0.5),
        'd_ln_g': gain((L, D_W)),
        'd_ln_b': nrm((L, D_W), 0.02),
        'd_ws': nrm((L, D_GROUPS, CHUNK, CHUNK), CHUNK ** -0.5),
        'd_bs': gain((L, D_GROUPS, CHUNK)),
        'w_d_out': nrm((L, D_W, D_MODEL), D_W ** -0.5),
        'w_o': nrm((L, D_MODEL, D_MODEL), 0.5 * D_MODEL ** -0.5),
        'norm_x_g': gain((L, D_MODEL)),
        'mem_norm_g': gain((L, D_MODEL)),
        'w_xq': nrm((L, D_MODEL, X_W), D_MODEL ** -0.5),
        'w_xk': nrm((L, D_MODEL, X_W), D_MODEL ** -0.5),
        'w_xv': nrm((L, D_MODEL, X_W), D_MODEL ** -0.5),
        'x_qnorm_g': gain((L, X_HD)),
        'x_knorm_g': gain((L, X_HD)),
        'w_xo': nrm((L, X_W, D_MODEL), 0.5 * X_W ** -0.5),
        'norm_ffn_g': gain((L, D_MODEL)),
        'w_up': nrm((L, D_MODEL, D_FF), D_MODEL ** -0.5),
        'w_down': nrm((L, D_FF, D_MODEL), 0.5 * D_FF ** -0.5),
    }


def reference(x_prompt, x_sample, cache_k_a, cache_v_a, state_conv_b, state_conv_c, cache_mem_k, cache_mem_v,
              page_table, mem_prompt, norm_mix_g, w_in, a_qnorm_g, a_knorm_g, a_lam, a_subln_g, w_a_out,
              b_conv_w, b_conv_b, w_b_out, c_conv_w, c_conv_b, c_ln_g, c_ln_b, w_c_out, d_ln_g, d_ln_b,
              d_ws, d_bs, w_d_out, w_o, norm_x_g, mem_norm_g, w_xq, w_xk, w_xv, x_qnorm_g, x_knorm_g,
              w_xo, norm_ffn_g, w_up, w_down):
    bp = x_prompt.shape[0]
    bs_ = x_sample.shape[0]
    n_pages = page_table.shape[1]
    xp, xs = x_prompt, x_sample
    prev_b_p = jnp.zeros((bp, B_K - 1, B_W), x_prompt.dtype)
    prev_c_p = jnp.zeros((bp, C_K - 1, C_W), x_prompt.dtype)
    kp_l, vp_l, cbp_l, ccp_l, mkp_l, mvp_l = [], [], [], [], [], []
    ks_l, vs_l, cbs_l, ccs_l, dvs_l = [], [], [], [], []
    for l in range(DEPTH):
        lw = dict(norm_mix_g=norm_mix_g[l], w_in=w_in[l], a_qnorm_g=a_qnorm_g[l], a_knorm_g=a_knorm_g[l],
                  a_lam=a_lam[l], a_subln_g=a_subln_g[l], w_a_out=w_a_out[l], b_conv_w=b_conv_w[l],
                  b_conv_b=b_conv_b[l], w_b_out=w_b_out[l], c_conv_w=c_conv_w[l], c_conv_b=c_conv_b[l],
                  c_ln_g=c_ln_g[l], c_ln_b=c_ln_b[l], w_c_out=w_c_out[l], d_ln_g=d_ln_g[l], d_ln_b=d_ln_b[l],
                  d_ws=d_ws[l], d_bs=d_bs[l], w_d_out=w_d_out[l], w_o=w_o[l], norm_x_g=norm_x_g[l],
                  w_xq=w_xq[l], x_qnorm_g=x_qnorm_g[l], w_xo=w_xo[l], norm_ffn_g=norm_ffn_g[l],
                  w_up=w_up[l], w_down=w_down[l])
        mk_p, mv_p = memory_kv(mem_prompt, mem_norm_g[l], w_xk[l], w_xv[l], x_knorm_g[l])
        xp, (k_p, v_p, cb_p, cc_p, _) = decoder_layer(xp, lw, l, None, None, prev_b_p, prev_c_p, mk_p, mv_p)
        past_k = cache_k_a[l][page_table].reshape(bs_, n_pages * PAGE_SIZE, A_HEADS, 2 * A_HD)
        past_v = cache_v_a[l][page_table].reshape(bs_, n_pages * PAGE_SIZE, A_HEADS, A_VD)
        xs, (k_s, v_s, cb_s, cc_s, dv_s) = decoder_layer(xs, lw, l, past_k, past_v, state_conv_b[l],
                                                         state_conv_c[l], cache_mem_k[l], cache_mem_v[l])
        kp_l.append(k_p); vp_l.append(v_p); cbp_l.append(cb_p); ccp_l.append(cc_p)
        mkp_l.append(mk_p); mvp_l.append(mv_p)
        ks_l.append(k_s); vs_l.append(v_s); cbs_l.append(cb_s); ccs_l.append(cc_s); dvs_l.append(dv_s)
    return (xp, xs, jnp.stack(kp_l), jnp.stack(vp_l), jnp.stack(cbp_l), jnp.stack(ccp_l),
            jnp.stack(mkp_l), jnp.stack(mvp_l), jnp.stack(ks_l), jnp.stack(vs_l), jnp.stack(cbs_l),
            jnp.stack(ccs_l), jnp.stack(dvs_l))
```

```python
import functools
import math

import jax
import jax.numpy as jnp
from jax import lax
from jax.experimental import pallas as pl
from jax.experimental.pallas import tpu as pltpu

F32 = jnp.float32
BF16 = jnp.bfloat16

D_MODEL = 1024
A_HEADS = 4
A_HD = 64
A_VD = 2 * A_HD
A_W = A_HEADS * 2 * A_HD
B_W = 256
B_K = 3
C_W = 256
C_K = 31
D_W = 256
D_GROUPS = 4
CHUNK = 128
N_BRANCH = 4
X_HEADS = 4
X_HD = 128
X_W = X_HEADS * X_HD
D_FF = 4 * D_MODEL
PAGE_SIZE = 128
EPS = 1e-6
NEG_INF = -1e30

_R_BX, _R_BB, _R_BC = 0, 256, 512
_R_CA, _R_CG = 768, 1024
_R_DU, _R_DV = 1280, 1536
_R_GZ = 1792
_R_COLS = _R_GZ + N_BRANCH * D_MODEL

_V7X_VMEM_BYTES = 64 * 1024 * 1024
_VMEM_LIMIT = _V7X_VMEM_BYTES - 8 * 1024 * 1024

_TM = 512
_TQ = 256
_TK = 512
_CONV_ROWS = 64
_B_HALO = 8
_C_HALO = 32
_FF_CHUNK = 1024
_DEC_PAGES = 8


def _rms(x, g):
    return x * lax.rsqrt(jnp.mean(x * x, axis=-1, keepdims=True) + EPS) * g


def _layer_norm(x, g, b):
    xc = x - jnp.mean(x, axis=-1, keepdims=True)
    var = jnp.mean(xc * xc, axis=-1, keepdims=True)
    return xc * lax.rsqrt(var + EPS) * g + b


def _dot(a, b):
    return jnp.dot(a, b, preferred_element_type=F32)


def _dot_nt(a, b):
    return lax.dot_general(a, b, (((1,), (1,)), ((), ())), preferred_element_type=F32)


def _idiv(x, d):
    assert d & (d - 1) == 0
    return lax.shift_right_logical(x, int(math.log2(d)))


def _group_mean_matrix(width, group):
    r = _idiv(lax.broadcasted_iota(jnp.int32, (width, width), 0), group)
    c = _idiv(lax.broadcasted_iota(jnp.int32, (width, width), 1), group)
    return jnp.where(r == c, 1.0 / group, 0.0).astype(BF16)


def _group_rms(t, g, gm):
    ms = _dot((t * t).astype(BF16), gm)
    return t * lax.rsqrt(ms + EPS) * g


def _lam(lam_ref, lam_init):
    a = lam_ref[...]
    s1 = jnp.sum(a[0:1] * a[1:2], axis=-1, keepdims=True)
    s2 = jnp.sum(a[2:3] * a[3:4], axis=-1, keepdims=True)
    return jnp.exp(s1) - jnp.exp(s2) + lam_init


def _full(shape):
    return pl.BlockSpec(shape, lambda *_: (0,) * len(shape))


def _params(sem):
    return pltpu.CompilerParams(dimension_semantics=sem, vmem_limit_bytes=_VMEM_LIMIT)


def _memkv_kernel(mem_ref, g_ref, wk_ref, wv_ref, kg_ref, mk_ref, mv_ref, mkb_ref, mvb_ref):
    h = _rms(mem_ref[...], g_ref[...]).astype(BF16)
    k = _dot(h, wk_ref[...])
    v = _dot(h, wv_ref[...])
    kg = kg_ref[...]
    k = jnp.concatenate([_rms(k[:, i * X_HD:(i + 1) * X_HD], kg) for i in range(X_HEADS)], axis=1)
    mk_ref[...] = k
    mv_ref[...] = v
    mkb_ref[...] = k.astype(BF16)
    mvb_ref[...] = v.astype(BF16)


def _memory_kv(mem, g, wk, wv, kg):
    b, n, _ = mem.shape
    blk = pl.BlockSpec((None, n, X_W), lambda i: (i, 0, 0))
    return pl.pallas_call(
        _memkv_kernel,
        grid=(b,),
        in_specs=[pl.BlockSpec((None, n, D_MODEL), lambda i: (i, 0, 0)), _full((1, D_MODEL)),
                  _full((D_MODEL, X_W)), _full((D_MODEL, X_W)), _full((1, X_HD))],
        out_specs=[blk, blk, blk, blk],
        out_shape=[jax.ShapeDtypeStruct((b, n, X_W), F32), jax.ShapeDtypeStruct((b, n, X_W), F32),
                   jax.ShapeDtypeStruct((b, n, X_W), BF16), jax.ShapeDtypeStruct((b, n, X_W), BF16)],
        compiler_params=_params(("arbitrary",)),
        name="memory_kv",
    )(mem, g, wk, wv, kg)


def _qkv_from_h(h, w_ref, qg, kg):
    z = _dot(h, w_ref[...])
    gm = _group_mean_matrix(A_W, A_HD)
    q = _group_rms(z[:, :A_W], qg, gm) * (A_HD ** -0.5)
    k = _group_rms(z[:, A_W:2 * A_W], kg, gm)
    v = z[:, 2 * A_W:]
    return q, k, v


def _qkv_kernel(x_ref, g_ref, w_ref, qg_ref, kg_ref, k_ref, v_ref, qb_ref, kb_ref, vb_ref):
    h = _rms(x_ref[...], g_ref[...]).astype(BF16)
    q, k, v = _qkv_from_h(h, w_ref, qg_ref[...], kg_ref[...])
    k_ref[...] = k
    v_ref[...] = v
    qb_ref[...] = q.astype(BF16)
    kb_ref[...] = k.astype(BF16)
    vb_ref[...] = v.astype(BF16)


def _qkv_proj(x, g, w_a, qg, kg):
    b, t, _ = x.shape
    blk = pl.BlockSpec((None, _TM, A_W), lambda i, j: (i, j, 0))
    f32 = jax.ShapeDtypeStruct((b, t, A_W), F32)
    b16 = jax.ShapeDtypeStruct((b, t, A_W), BF16)
    return pl.pallas_call(
        _qkv_kernel,
        grid=(b, t // _TM),
        in_specs=[pl.BlockSpec((None, _TM, D_MODEL), lambda i, j: (i, j, 0)), _full((1, D_MODEL)),
                  _full((D_MODEL, 3 * A_W)), _full((1, A_W)), _full((1, A_W))],
        out_specs=[blk] * 5,
        out_shape=[f32, f32, b16, b16, b16],
        compiler_params=_params(("arbitrary", "arbitrary")),
        name="qkv_proj",
    )(x, g, w_a, qg, kg)


def _diff_attn_kernel(slope_ref, q_ref, k_ref, v_ref, lam_ref, sg_ref, o_ref,
                      qs_scr, m_scr, l_scr, acc_scr, *, lam_init):
    h = pl.program_id(1)
    qi = pl.program_id(2)
    ki = pl.program_id(3)
    nk = pl.num_programs(3)
    q0 = qi * _TQ
    k0 = ki * _TK

    @pl.when(ki == 0)
    def _():
        q = q_ref[...]
        lane = lax.broadcasted_iota(jnp.int32, q.shape, 1)
        zero = jnp.zeros_like(q)
        qs_scr[0:_TQ, :] = jnp.where(lane < A_HD, q, zero)
        qs_scr[_TQ:2 * _TQ, :] = jnp.where(lane >= A_HD, q, zero)
        m_scr[...] = jnp.full(m_scr.shape, NEG_INF, F32)
        l_scr[...] = jnp.zeros(l_scr.shape, F32)
        acc_scr[...] = jnp.zeros(acc_scr.shape, F32)

    def step(masked):
        s = _dot_nt(qs_scr[...], k_ref[...])
        kpos = k0 + lax.broadcasted_iota(jnp.int32, (1, _TK), 1)
        s = s + slope_ref[h] * kpos.astype(F32)
        if masked:
            row = lax.broadcasted_iota(jnp.int32, (2 * _TQ, _TK), 0)
            qpos = q0 + jnp.where(row >= _TQ, row - _TQ, row)
            col = k0 + lax.broadcasted_iota(jnp.int32, (2 * _TQ, _TK), 1)
            s = jnp.where(qpos >= col, s, NEG_INF)
        m_old = m_scr[...]
        m_new = jnp.maximum(m_old, jnp.max(s, axis=-1, keepdims=True))
        alpha = jnp.exp(m_old - m_new)
        p = jnp.exp(s - m_new)
        l_scr[...] = alpha * l_scr[...] + jnp.sum(p, axis=-1, keepdims=True)
        acc_scr[...] = alpha * acc_scr[...] + _dot(p.astype(BF16), v_ref[...])
        m_scr[...] = m_new

    needed = k0 <= q0 + _TQ - 1
    crosses = k0 + _TK - 1 > q0

    @pl.when(jnp.logical_and(needed, crosses))
    def _():
        step(True)

    @pl.when(jnp.logical_and(needed, jnp.logical_not(crosses)))
    def _():
        step(False)

    @pl.when(ki == nk - 1)
    def _():
        o = acc_scr[...] / l_scr[...]
        o = o[0:_TQ] - _lam(lam_ref, lam_init) * o[_TQ:2 * _TQ]
        o_ref[...] = (_rms(o, sg_ref[...]) * (1.0 - lam_init)).astype(BF16)


def _diff_attention(slopes, qb, kb, vb, a_lam, sg, lam_init):
    b, t, _ = qb.shape
    nq, nk = t // _TQ, t // _TK

    def kv_map(bi, h, qi, ki):
        last = (qi * _TQ + _TQ - 1) // _TK
        return (bi, jnp.minimum(ki, last), h)

    return pl.pallas_call(
        functools.partial(_diff_attn_kernel, lam_init=lam_init),
        grid=(b, A_HEADS, nq, nk),
        in_specs=[pl.BlockSpec(memory_space=pltpu.SMEM),
                  pl.BlockSpec((None, _TQ, A_VD), lambda bi, h, qi, ki: (bi, qi, h)),
                  pl.BlockSpec((None, _TK, A_VD), kv_map),
                  pl.BlockSpec((None, _TK, A_VD), kv_map),
                  _full((4, A_HD)), _full((1, A_VD))],
        out_specs=pl.BlockSpec((None, _TQ, A_VD), lambda bi, h, qi, ki: (bi, qi, h)),
        out_shape=jax.ShapeDtypeStruct((b, t, A_W), BF16),
        scratch_shapes=[pltpu.VMEM((2 * _TQ, A_VD), BF16), pltpu.VMEM((2 * _TQ, 1), F32),
                        pltpu.VMEM((2 * _TQ, 1), F32), pltpu.VMEM((2 * _TQ, A_VD), F32)],
        compiler_params=_params(("arbitrary",) * 4),
        name="diff_attention",
    )(slopes, qb, kb, vb, a_lam, sg)


def _lane_group_mask(width, group, g):
    lane = lax.broadcasted_iota(jnp.int32, (1, width), 1)
    return (_idiv(lane, group) == g).astype(F32)


def _mixer_kernel(x_ref, o_ref, g_ref, wr_ref, wa_ref, wb_ref, wc_ref, wd_ref, wo_ref,
                  bcw_ref, bcb_ref, ccw_ref, ccb_ref, clg_ref, clb_ref, dlg_ref, dlb_ref,
                  dws_ref, dbst_ref, x1_ref, cbp_ref, ccp_ref, ubuf, cbuf, ycin):
    tm = _TM

    @pl.when(pl.program_id(1) == 0)
    def _():
        ubuf[0:_B_HALO, :] = jnp.zeros((_B_HALO, B_W), F32)
        cbuf[0:_C_HALO, :] = jnp.zeros((_C_HALO, C_W), F32)

    x = x_ref[...]
    h = _rms(x, g_ref[...]).astype(BF16)

    def gate(i):
        return jax.nn.sigmoid(_dot(h, wr_ref[:, _R_GZ + i * D_MODEL:_R_GZ + (i + 1) * D_MODEL]))

    merged = gate(0) * _dot(o_ref[...], wa_ref[...])

    zb = _dot(h, wr_ref[:, _R_BX:_R_CA])
    u = zb[:, 512:768] * zb[:, 0:256]
    ubuf[_B_HALO:_B_HALO + tm, :] = u
    conv_b = (bcw_ref[0:1, :] * ubuf[_B_HALO - 2:_B_HALO - 2 + tm, :]
              + bcw_ref[1:2, :] * ubuf[_B_HALO - 1:_B_HALO - 1 + tm, :]
              + bcw_ref[2:3, :] * u + bcb_ref[...])
    yb = _dot((zb[:, 256:512] * conv_b).astype(BF16), wb_ref[...])
    merged = merged + gate(1) * yb
    cbp_ref[...] = ubuf[_B_HALO + tm - (B_K - 1):_B_HALO + tm, :]
    ubuf[0:_B_HALO, :] = ubuf[tm:tm + _B_HALO, :]

    zc = _dot(h, wr_ref[:, _R_CA:_R_DU])
    cbuf[_C_HALO:_C_HALO + tm, :] = zc[:, 0:256] * jax.nn.sigmoid(zc[:, 256:512])
    base = _C_HALO - (C_K - 1)
    for r in range(0, tm, _CONV_ROWS):
        acc = jnp.broadcast_to(ccb_ref[...], (_CONV_ROWS, C_W))
        for k in range(C_K):
            acc = acc + ccw_ref[k:k + 1, :] * cbuf[base + r + k:base + r + k + _CONV_ROWS, :]
        ycin[r:r + _CONV_ROWS, :] = jax.nn.silu(
            _layer_norm(acc, clg_ref[...], clb_ref[...])).astype(BF16)
    yc = _dot(ycin[...], wc_ref[...])
    merged = merged + gate(2) * yc
    ccp_ref[...] = cbuf[_C_HALO + tm - (C_K - 1):_C_HALO + tm, :]
    cbuf[0:_C_HALO, :] = cbuf[tm:tm + _C_HALO, :]

    zd = jax.nn.gelu(_dot(h, wr_ref[:, _R_DU:_R_GZ]))
    du = zd[:, 0:256]
    dvn = _layer_norm(zd[:, 256:512], dlg_ref[...], dlb_ref[...])
    row = lax.broadcasted_iota(jnp.int32, (CHUNK, CHUNK), 0)
    col = lax.broadcasted_iota(jnp.int32, (CHUNK, CHUNK), 1)
    tril = (row >= col).astype(F32)
    wcat = jnp.concatenate([dws_ref[g] * tril for g in range(D_GROUPS)], axis=1).astype(BF16)
    masks = [_lane_group_mask(D_W, D_W // D_GROUPS, g) for g in range(D_GROUPS)]
    bsmat = sum(dbst_ref[:, g:g + 1] * masks[g] for g in range(D_GROUPS))
    for c in range(0, tm, CHUNK):
        vch = dvn[c:c + CHUNK]
        rhs = jnp.concatenate([(vch * masks[g]).astype(BF16) for g in range(D_GROUPS)], axis=0)
        s = _dot(wcat, rhs) + bsmat
        ycin[c:c + CHUNK, :] = (du[c:c + CHUNK] * s).astype(BF16)
    yd = _dot(ycin[...], wd_ref[...])
    merged = merged + gate(3) * yd

    x1_ref[...] = x + _dot(merged.astype(BF16), wo_ref[...])


def _mixer(x, o, p):
    b, t, _ = x.shape
    row = lambda w: pl.BlockSpec((None, _TM, w), lambda i, j: (i, j, 0))
    return pl.pallas_call(
        _mixer_kernel,
        grid=(b, t // _TM),
        in_specs=[row(D_MODEL), row(A_W), _full((1, D_MODEL)), _full((D_MODEL, _R_COLS)),
                  _full((A_W, D_MODEL)), _full((B_W, D_MODEL)), _full((C_W, D_MODEL)),
                  _full((D_W, D_MODEL)), _full((D_MODEL, D_MODEL)),
                  _full((B_K, B_W)), _full((1, B_W)), _full((C_K, C_W)), _full((1, C_W)),
                  _full((1, C_W)), _full((1, C_W)), _full((1, D_W)), _full((1, D_W)),
                  _full((D_GROUPS, CHUNK, CHUNK)), _full((CHUNK, D_GROUPS))],
        out_specs=[row(D_MODEL),
                   pl.BlockSpec((None, B_K - 1, B_W), lambda i, j: (i, 0, 0)),
                   pl.BlockSpec((None, C_K - 1, C_W), lambda i, j: (i, 0, 0))],
        out_shape=[jax.ShapeDtypeStruct((b, t, D_MODEL), F32),
                   jax.ShapeDtypeStruct((b, B_K - 1, B_W), F32),
                   jax.ShapeDtypeStruct((b, C_K - 1, C_W), F32)],
        scratch_shapes=[pltpu.VMEM((_B_HALO + _TM, B_W), F32), pltpu.VMEM((_C_HALO + _TM, C_W), F32),
                        pltpu.VMEM((_TM, C_W), BF16)],
        compiler_params=_params(("arbitrary", "arbitrary")),
        name="mixer",
    )(x, o, p["norm_mix_g"], p["w_in_r"], p["w_a_out"], p["w_b_out"], p["w_c_out"], p["w_d_out"],
      p["w_o"], p["b_conv_w"], p["b_conv_b"], p["c_conv_w"], p["c_conv_b"], p["c_ln_g"], p["c_ln_b"],
      p["d_ln_g"], p["d_ln_b"], p["d_ws"], p["d_bs_t"])


def _ffn(x2, gf, wup_ref, wdn_ref):
    h3 = _rms(x2, gf).astype(BF16)
    acc = jnp.zeros(x2.shape, F32)
    for c in range(0, D_FF, _FF_CHUNK):
        a = jnp.maximum(_dot(h3, wup_ref[:, c:c + _FF_CHUNK]), 0.0)
        acc = acc + _dot((a * a).astype(BF16), wdn_ref[c:c + _FF_CHUNK, :])
    return x2 + acc


def _xffn_kernel(x_ref, mk_ref, mv_ref, gx_ref, wxq_ref, qg_ref, wxo_ref, gf_ref, wup_ref, wdn_ref,
                 out_ref):
    x = x_ref[...]
    h2 = _rms(x, gx_ref[...]).astype(BF16)
    q = _dot(h2, wxq_ref[...])
    qg = qg_ref[...]
    oms = []
    for i in range(X_HEADS):
        sl = slice(i * X_HD, (i + 1) * X_HD)
        qh = (_rms(q[:, sl], qg) * (X_HD ** -0.5)).astype(BF16)
        s = _dot_nt(qh, mk_ref[:, sl])
        p = jnp.exp(s - jnp.max(s, axis=-1, keepdims=True))
        p = p / jnp.sum(p, axis=-1, keepdims=True)
        oms.append(_dot(p.astype(BF16), mv_ref[:, sl]))
    om = jnp.concatenate(oms, axis=1).astype(BF16)
    x2 = x + _dot(om, wxo_ref[...])
    out_ref[...] = _ffn(x2, gf_ref[...], wup_ref, wdn_ref)


def _xattn_ffn(x, mkb, mvb, p):
    b, t, _ = x.shape
    n = mkb.shape[1]
    row = pl.BlockSpec((None, _TM, D_MODEL), lambda i, j: (i, j, 0))
    mem = pl.BlockSpec((None, n, X_W), lambda i, j: (i, 0, 0))
    return pl.pallas_call(
        _xffn_kernel,
        grid=(b, t // _TM),
        in_specs=[row, mem, mem, _full((1, D_MODEL)), _full((D_MODEL, X_W)), _full((1, X_HD)),
                  _full((X_W, D_MODEL)), _full((1, D_MODEL)), _full((D_MODEL, D_FF)),
                  _full((D_FF, D_MODEL))],
        out_specs=row,
        out_shape=jax.ShapeDtypeStruct((b, t, D_MODEL), F32),
        compiler_params=_params(("arbitrary", "arbitrary")),
        name="xattn_ffn",
    )(x, mkb, mvb, p["norm_x_g"], p["w_xq"], p["x_qnorm_g"], p["w_xo"], p["norm_ffn_g"],
      p["w_up"], p["w_down"])


def _sample_mixer_kernel(x_ref, sb_ref, sc_ref, g_ref, wa_ref, wr_ref, qg_ref, kg_ref,
                         wb_ref, wc_ref, wd_ref, bcw_ref, bcb_ref, ccw_ref, ccb_ref, clg_ref, clb_ref,
                         dlg_ref, dlb_ref, dws_ref, dbs_ref,
                         q_ref, k_ref, v_ref, cb_ref, cc_ref, dvn_ref, part_ref, g0_ref):
    x = x_ref[...]
    h = _rms(x, g_ref[...]).astype(BF16)
    q, k, v = _qkv_from_h(h, wa_ref, qg_ref[...], kg_ref[...])
    q_ref[...] = q
    k_ref[...] = k
    v_ref[...] = v

    def gate(i):
        return jax.nn.sigmoid(_dot(h, wr_ref[:, _R_GZ + i * D_MODEL:_R_GZ + (i + 1) * D_MODEL]))

    g0_ref[...] = gate(0)

    zb = _dot(h, wr_ref[:, _R_BX:_R_CA])
    u = zb[:, 512:768] * zb[:, 0:256]
    conv_b = (bcw_ref[0:1, :] * sb_ref[:, 0:B_W] + bcw_ref[1:2, :] * sb_ref[:, B_W:2 * B_W]
              + bcw_ref[2:3, :] * u + bcb_ref[...])
    part = gate(1) * _dot((zb[:, 256:512] * conv_b).astype(BF16), wb_ref[...])
    cb_ref[:, 0:B_W] = sb_ref[:, B_W:2 * B_W]
    cb_ref[:, B_W:2 * B_W] = u

    zc = _dot(h, wr_ref[:, _R_CA:_R_DU])
    uc = zc[:, 0:256] * jax.nn.sigmoid(zc[:, 256:512])
    acc = ccw_ref[C_K - 1:C_K, :] * uc + ccb_ref[...]
    for kk in range(C_K - 1):
        acc = acc + ccw_ref[kk:kk + 1, :] * sc_ref[:, kk * C_W:(kk + 1) * C_W]
    yc = _dot(jax.nn.silu(_layer_norm(acc, clg_ref[...], clb_ref[...])).astype(BF16), wc_ref[...])
    part = part + gate(2) * yc
    cc_ref[:, 0:(C_K - 2) * C_W] = sc_ref[:, C_W:(C_K - 1) * C_W]
    cc_ref[:, (C_K - 2) * C_W:(C_K - 1) * C_W] = uc

    zd = jax.nn.gelu(_dot(h, wr_ref[:, _R_DU:_R_GZ]))
    dvn = _layer_norm(zd[:, 256:512], dlg_ref[...], dlb_ref[...])
    dvn_ref[...] = dvn
    w00 = sum(dws_ref[g, 0:1, 0:1] * _lane_group_mask(D_W, D_W // D_GROUPS, g) for g in range(D_GROUPS))
    b0 = sum(dbs_ref[g:g + 1, 0:1] * _lane_group_mask(D_W, D_W // D_GROUPS, g) for g in range(D_GROUPS))
    yd = _dot((zd[:, 0:256] * (w00 * dvn + b0)).astype(BF16), wd_ref[...])
    part_ref[...] = part + gate(3) * yd


def _sample_mixer(x, sb, sc, p):
    n = x.shape[0]
    shapes = [(n, A_W), (n, A_W), (n, A_W), (n, (B_K - 1) * B_W), (n, (C_K - 1) * C_W), (n, D_W),
              (n, D_MODEL), (n, D_MODEL)]
    ins = [x, sb, sc, p["norm_mix_g"], p["w_in_a"], p["w_in_r"], p["a_qnorm_g"], p["a_knorm_g"],
           p["w_b_out"], p["w_c_out"], p["w_d_out"], p["b_conv_w"], p["b_conv_b"], p["c_conv_w"],
           p["c_conv_b"], p["c_ln_g"], p["c_ln_b"], p["d_ln_g"], p["d_ln_b"], p["d_ws"], p["d_bs"]]
    return pl.pallas_call(
        _sample_mixer_kernel,
        grid=(1,),
        in_specs=[_full(a.shape) for a in ins],
        out_specs=[_full(s) for s in shapes],
        out_shape=[jax.ShapeDtypeStruct(s, F32) for s in shapes],
        compiler_params=_params(("arbitrary",)),
        name="sample_mixer",
    )(*ins)


_DEC_ROWS = 2 * A_HEADS


def _decode_attn_kernel(pt_ref, q_ref, kn_ref, vn_ref, lam_ref, sg_ref, *rest, lam_init, past_len):
    del pt_ref
    k_refs = rest[:_DEC_PAGES]
    v_refs = rest[_DEC_PAGES:2 * _DEC_PAGES]
    o_ref, qd_scr, m_scr, l_scr, acc_scr = rest[2 * _DEC_PAGES:]
    j = pl.program_id(1)
    nj = pl.num_programs(1)

    row = lax.broadcasted_iota(jnp.int32, (_DEC_ROWS, A_W), 0)
    lane = lax.broadcasted_iota(jnp.int32, (_DEC_ROWS, A_W), 1)
    own_comp = _idiv(lane, A_HD) == row
    rowv = lax.broadcasted_iota(jnp.int32, (_DEC_ROWS, 1), 0)
    slope = sum(jnp.where(_idiv(rowv, 2) == i, 2.0 ** (-8.0 * (i + 1) / A_HEADS), 0.0)
                for i in range(A_HEADS))

    @pl.when(j == 0)
    def _():
        qd_scr[...] = jnp.where(own_comp, jnp.broadcast_to(q_ref[...], (_DEC_ROWS, A_W)), 0.0)
        m_scr[...] = jnp.full(m_scr.shape, NEG_INF, F32)
        l_scr[...] = jnp.zeros(l_scr.shape, F32)
        acc_scr[...] = jnp.zeros(acc_scr.shape, F32)

    qd = qd_scr[...].astype(BF16)
    s = jnp.concatenate([_dot_nt(qd, k_refs[i][...].astype(BF16)) for i in range(_DEC_PAGES)], axis=1)
    kpos = j * (_DEC_PAGES * PAGE_SIZE) + lax.broadcasted_iota(jnp.int32, (1, _DEC_PAGES * PAGE_SIZE), 1)
    s = s - slope * (past_len - kpos).astype(F32)
    m_old = m_scr[...]
    m_new = jnp.maximum(m_old, jnp.max(s, axis=-1, keepdims=True))
    alpha = jnp.exp(m_old - m_new)
    p = jnp.exp(s - m_new)
    l_scr[...] = alpha * l_scr[...] + jnp.sum(p, axis=-1, keepdims=True)
    p = p.astype(BF16)
    pv = sum(_dot(p[:, i * PAGE_SIZE:(i + 1) * PAGE_SIZE], v_refs[i][...].astype(BF16))
             for i in range(_DEC_PAGES))
    acc_scr[...] = alpha * acc_scr[...] + pv
    m_scr[...] = m_new

    @pl.when(j == nj - 1)
    def _():
        s_new = jnp.sum(qd_scr[...] * kn_ref[...], axis=-1, keepdims=True)
        m_old = m_scr[...]
        m_fin = jnp.maximum(m_old, s_new)
        alpha = jnp.exp(m_old - m_fin)
        p_new = jnp.exp(s_new - m_fin)
        l_fin = alpha * l_scr[...] + p_new
        o_all = (alpha * acc_scr[...] + p_new * vn_ref[...]) / l_fin
        lam = _lam(lam_ref, lam_init)
        coef = jnp.where((rowv & 1) == 0, 1.0, -lam)
        own_head = _idiv(lane, A_VD) == _idiv(row, 2)
        o = jnp.sum(jnp.where(own_head, coef * o_all, 0.0), axis=0, keepdims=True)
        sg = sg_ref[...]
        o_ref[...] = jnp.concatenate(
            [_rms(o[:, i * A_VD:(i + 1) * A_VD], sg) for i in range(A_HEADS)], axis=1) * (1.0 - lam_init)


def _decode_attention(page_table, q, k_new, v_new, cache_k, cache_v, layer, a_lam, sg, lam_init):
    n, n_pages = page_table.shape
    past_len = n_pages * PAGE_SIZE
    row = pl.BlockSpec((None, 1, A_W), lambda b, j, pt: (b, 0, 0))

    def page(i):
        return pl.BlockSpec((None, None, PAGE_SIZE, A_W),
                            lambda b, j, pt: (layer, pt[b, j * _DEC_PAGES + i], 0, 0))

    grid_spec = pltpu.PrefetchScalarGridSpec(
        num_scalar_prefetch=1,
        grid=(n, n_pages // _DEC_PAGES),
        in_specs=[row, row, row, pl.BlockSpec((4, A_HD), lambda b, j, pt: (0, 0)),
                  pl.BlockSpec((1, A_VD), lambda b, j, pt: (0, 0))]
        + [page(i) for i in range(_DEC_PAGES)] + [page(i) for i in range(_DEC_PAGES)],
        out_specs=row,
        scratch_shapes=[pltpu.VMEM((_DEC_ROWS, A_W), F32), pltpu.VMEM((_DEC_ROWS, 1), F32),
                        pltpu.VMEM((_DEC_ROWS, 1), F32), pltpu.VMEM((_DEC_ROWS, A_W), F32)],
    )
    return pl.pallas_call(
        functools.partial(_decode_attn_kernel, lam_init=lam_init, past_len=past_len),
        grid_spec=grid_spec,
        out_shape=jax.ShapeDtypeStruct((n, 1, A_W), F32),
        compiler_params=_params(("arbitrary", "arbitrary")),
        name="decode_attention",
    )(page_table, q.reshape(n, 1, A_W), k_new.reshape(n, 1, A_W), v_new.reshape(n, 1, A_W), a_lam, sg,
      *([cache_k] * _DEC_PAGES), *([cache_v] * _DEC_PAGES))


def _sample_tail_kernel(x_ref, o_ref, g0_ref, part_ref, mk_ref, mv_ref, wa_ref, wo_ref, gx_ref, wxq_ref,
                        qg_ref, wxo_ref, gf_ref, wup_ref, wdn_ref, out_ref, x1_scr, q_scr, om_scr):
    b = pl.program_id(0)
    nb = pl.num_programs(0)

    @pl.when(b == 0)
    def _():
        merged = g0_ref[...] * _dot(o_ref[...].astype(BF16), wa_ref[...]) + part_ref[...]
        x1 = x_ref[...] + _dot(merged.astype(BF16), wo_ref[...])
        x1_scr[...] = x1
        q = _dot(_rms(x1, gx_ref[...]).astype(BF16), wxq_ref[...])
        qg = qg_ref[...]
        q_scr[...] = jnp.concatenate(
            [_rms(q[:, i * X_HD:(i + 1) * X_HD], qg) for i in range(X_HEADS)], axis=1) * (X_HD ** -0.5)

    row = lax.broadcasted_iota(jnp.int32, (8, X_W), 0)
    lane = lax.broadcasted_iota(jnp.int32, (8, X_W), 1)
    own_head = _idiv(lane, X_HD) == row
    qd = jnp.where(own_head, jnp.broadcast_to(q_scr[pl.ds(b, 1), :], (8, X_W)), 0.0).astype(BF16)
    s = _dot_nt(qd, mk_ref[...].astype(BF16))
    p = jnp.exp(s - jnp.max(s, axis=-1, keepdims=True))
    p = p / jnp.sum(p, axis=-1, keepdims=True)
    om_all = _dot(p.astype(BF16), mv_ref[...].astype(BF16))
    om_scr[pl.ds(b, 1), :] = jnp.sum(jnp.where(own_head, om_all, 0.0), axis=0, keepdims=True)

    @pl.when(b == nb - 1)
    def _():
        x2 = x1_scr[...] + _dot(om_scr[...].astype(BF16), wxo_ref[...])
        out_ref[...] = _ffn(x2, gf_ref[...], wup_ref, wdn_ref)


def _sample_tail(x, o, g0, part, mem_k, mem_v, layer, p):
    n = x.shape[0]
    n_mem = mem_k.shape[2]
    mem = pl.BlockSpec((None, None, n_mem, X_W), lambda b: (layer, b, 0, 0))
    ins = [x, o, g0, part]
    ws = [p["w_a_out"], p["w_o"], p["norm_x_g"], p["w_xq"], p["x_qnorm_g"], p["w_xo"], p["norm_ffn_g"],
          p["w_up"], p["w_down"]]
    return pl.pallas_call(
        _sample_tail_kernel,
        grid=(n,),
        in_specs=[_full(a.shape) for a in ins] + [mem, mem] + [_full(a.shape) for a in ws],
        out_specs=_full((n, D_MODEL)),
        out_shape=jax.ShapeDtypeStruct((n, D_MODEL), F32),
        scratch_shapes=[pltpu.VMEM((n, D_MODEL), F32), pltpu.VMEM((n, X_W), F32),
                        pltpu.VMEM((n, X_W), F32)],
        compiler_params=_params(("arbitrary",)),
        name="sample_tail",
    )(*ins, mem_k, mem_v, *ws)


def _layer_params(l, a):
    row = lambda v: v[l].reshape(1, -1)
    b16 = lambda w: w.astype(BF16)
    w_in = a["w_in"][l]
    return dict(
        norm_mix_g=row(a["norm_mix_g"]),
        w_in_a=b16(w_in[:, :3 * A_W]), w_in_r=b16(w_in[:, 3 * A_W:]),
        a_qnorm_g=jnp.tile(a["a_qnorm_g"][l], A_W // A_HD).reshape(1, A_W),
        a_knorm_g=jnp.tile(a["a_knorm_g"][l], A_W // A_HD).reshape(1, A_W),
        a_lam=a["a_lam"][l], a_subln_g=row(a["a_subln_g"]),
        w_a_out=b16(a["w_a_out"][l]),
        b_conv_w=a["b_conv_w"][l], b_conv_b=row(a["b_conv_b"]), w_b_out=b16(a["w_b_out"][l]),
        c_conv_w=a["c_conv_w"][l], c_conv_b=row(a["c_conv_b"]),
        c_ln_g=row(a["c_ln_g"]), c_ln_b=row(a["c_ln_b"]), w_c_out=b16(a["w_c_out"][l]),
        d_ln_g=row(a["d_ln_g"]), d_ln_b=row(a["d_ln_b"]),
        d_ws=a["d_ws"][l], d_bs=a["d_bs"][l], d_bs_t=a["d_bs"][l].T, w_d_out=b16(a["w_d_out"][l]),
        w_o=b16(a["w_o"][l]),
        norm_x_g=row(a["norm_x_g"]), mem_norm_g=row(a["mem_norm_g"]),
        w_xq=b16(a["w_xq"][l]), w_xk=b16(a["w_xk"][l]), w_xv=b16(a["w_xv"][l]),
        x_qnorm_g=row(a["x_qnorm_g"]), x_knorm_g=row(a["x_knorm_g"]),
        w_xo=b16(a["w_xo"][l]), norm_ffn_g=row(a["norm_ffn_g"]),
        w_up=b16(a["w_up"][l]), w_down=b16(a["w_down"][l]),
    )


def kernel(x_prompt, x_sample, cache_k_a, cache_v_a, state_conv_b, state_conv_c, cache_mem_k, cache_mem_v,
           page_table, mem_prompt, norm_mix_g, w_in, a_qnorm_g, a_knorm_g, a_lam, a_subln_g, w_a_out,
           b_conv_w, b_conv_b, w_b_out, c_conv_w, c_conv_b, c_ln_g, c_ln_b, w_c_out, d_ln_g, d_ln_b,
           d_ws, d_bs, w_d_out, w_o, norm_x_g, mem_norm_g, w_xq, w_xk, w_xv, x_qnorm_g, x_knorm_g,
           w_xo, norm_ffn_g, w_up, w_down):
    weights = dict(norm_mix_g=norm_mix_g, w_in=w_in, a_qnorm_g=a_qnorm_g, a_knorm_g=a_knorm_g, a_lam=a_lam,
                   a_subln_g=a_subln_g, w_a_out=w_a_out, b_conv_w=b_conv_w, b_conv_b=b_conv_b,
                   w_b_out=w_b_out, c_conv_w=c_conv_w, c_conv_b=c_conv_b, c_ln_g=c_ln_g, c_ln_b=c_ln_b,
                   w_c_out=w_c_out, d_ln_g=d_ln_g, d_ln_b=d_ln_b, d_ws=d_ws, d_bs=d_bs, w_d_out=w_d_out,
                   w_o=w_o, norm_x_g=norm_x_g, mem_norm_g=mem_norm_g, w_xq=w_xq, w_xk=w_xk, w_xv=w_xv,
                   x_qnorm_g=x_qnorm_g, x_knorm_g=x_knorm_g, w_xo=w_xo, norm_ffn_g=norm_ffn_g,
                   w_up=w_up, w_down=w_down)
    depth = w_in.shape[0]
    bp, t, _ = x_prompt.shape
    ns = x_sample.shape[0]
    n_pool = cache_k_a.shape[1]
    n_mem = cache_mem_k.shape[2]
    cache_k = cache_k_a.reshape(depth, n_pool, PAGE_SIZE, A_W)
    cache_v = cache_v_a.reshape(depth, n_pool, PAGE_SIZE, A_W)
    mem_k_s = cache_mem_k.reshape(depth, ns, n_mem, X_W)
    mem_v_s = cache_mem_v.reshape(depth, ns, n_mem, X_W)
    slopes = jnp.asarray([2.0 ** (-8.0 * (i + 1) / A_HEADS) for i in range(A_HEADS)], F32)

    xp = x_prompt
    xs = x_sample.reshape(ns, D_MODEL)
    outs = [[] for _ in range(11)]
    for l in range(depth):
        p = _layer_params(l, weights)
        lam_init = 0.8 - 0.6 * math.exp(-0.3 * l)

        mk, mv, mkb, mvb = _memory_kv(mem_prompt, p["mem_norm_g"], p["w_xk"], p["w_xv"], p["x_knorm_g"])
        k_p, v_p, qb, kb, vb = _qkv_proj(xp, p["norm_mix_g"], p["w_in_a"], p["a_qnorm_g"], p["a_knorm_g"])
        o_p = _diff_attention(slopes, qb, kb, vb, p["a_lam"], p["a_subln_g"], lam_init)
        xp, cb_p, cc_p = _mixer(xp, o_p, p)
        xp = _xattn_ffn(xp, mkb, mvb, p)

        sb = state_conv_b[l].reshape(ns, (B_K - 1) * B_W)
        sc = state_conv_c[l].reshape(ns, (C_K - 1) * C_W)
        q_s, k_s, v_s, cb_s, cc_s, dvn_s, part, g0 = _sample_mixer(xs, sb, sc, p)
        o_s = _decode_attention(page_table, q_s, k_s, v_s, cache_k, cache_v, l, p["a_lam"],
                                p["a_subln_g"], lam_init)
        xs = _sample_tail(xs, o_s.reshape(ns, A_W), g0, part, mem_k_s, mem_v_s, l, p)

        for lst, val in zip(outs, (
                k_p.reshape(bp, t, A_HEADS, 2 * A_HD), v_p.reshape(bp, t, A_HEADS, A_VD), cb_p, cc_p,
                mk.reshape(bp, n_mem, X_HEADS, X_HD), mv.reshape(bp, n_mem, X_HEADS, X_HD),
                k_s.reshape(ns, 1, A_HEADS, 2 * A_HD), v_s.reshape(ns, 1, A_HEADS, A_VD),
                cb_s.reshape(ns, B_K - 1, B_W), cc_s.reshape(ns, C_K - 1, C_W),
                dvn_s.reshape(ns, 1, D_W))):
            lst.append(val)
    return (xp, xs.reshape(ns, 1, D_MODEL)) + tuple(jnp.stack(o) for o in outs)
```

```python
import functools
import math

import jax
import jax.numpy as jnp
from jax import lax
from jax.experimental import pallas as pl
from jax.experimental.pallas import tpu as pltpu

F32 = jnp.float32
BF16 = jnp.bfloat16

D_MODEL = 1024
A_HEADS = 4
A_HD = 64
A_VD = 2 * A_HD
A_W = A_HEADS * 2 * A_HD
B_W = 256
B_K = 3
C_W = 256
C_K = 31
D_W = 256
D_GROUPS = 4
CHUNK = 128
N_BRANCH = 4
X_HEADS = 4
X_HD = 128
X_W = X_HEADS * X_HD
D_FF = 4 * D_MODEL
PAGE_SIZE = 128
EPS = 1e-6
NEG_INF = -1e30

_R_BX, _R_BB, _R_BC = 0, 256, 512
_R_CA, _R_CG = 768, 1024
_R_DU, _R_DV = 1280, 1536
_R_GZ = 1792
_R_COLS = _R_GZ + N_BRANCH * D_MODEL

_V7X_VMEM_BYTES = 64 * 1024 * 1024
_VMEM_LIMIT = _V7X_VMEM_BYTES - 8 * 1024 * 1024

_TM = 512
_TQ = 512
_TK = 512
_CONV_ROWS = 64
_B_HALO = 8
_C_HALO = 32
_FF_CHUNK = 1024
_DEC_PAGES = 8


def _rms(x, g):
    return x * lax.rsqrt(jnp.mean(x * x, axis=-1, keepdims=True) + EPS) * g


def _layer_norm(x, g, b):
    xc = x - jnp.mean(x, axis=-1, keepdims=True)
    var = jnp.mean(xc * xc, axis=-1, keepdims=True)
    return xc * lax.rsqrt(var + EPS) * g + b


def _dot(a, b):
    return jnp.dot(a, b, preferred_element_type=F32)


def _dot_nt(a, b):
    return lax.dot_general(a, b, (((1,), (1,)), ((), ())), preferred_element_type=F32)


def _idiv(x, d):
    assert d & (d - 1) == 0
    return lax.shift_right_logical(x, int(math.log2(d)))


def _group_mean_matrix(width, group):
    r = _idiv(lax.broadcasted_iota(jnp.int32, (width, width), 0), group)
    c = _idiv(lax.broadcasted_iota(jnp.int32, (width, width), 1), group)
    return jnp.where(r == c, 1.0 / group, 0.0).astype(BF16)


def _group_rms(t, g, gm):
    ms = _dot((t * t).astype(BF16), gm)
    return t * lax.rsqrt(ms + EPS) * g


def _lam(lam_ref, lam_init):
    a = lam_ref[...]
    s1 = jnp.sum(a[0:1] * a[1:2], axis=-1, keepdims=True)
    s2 = jnp.sum(a[2:3] * a[3:4], axis=-1, keepdims=True)
    return jnp.exp(s1) - jnp.exp(s2) + lam_init


def _full(shape):
    return pl.BlockSpec(shape, lambda *_: (0,) * len(shape))


def _params(sem):
    return pltpu.CompilerParams(dimension_semantics=sem, vmem_limit_bytes=_VMEM_LIMIT)


def _memkv_kernel(mem_ref, g_ref, wk_ref, wv_ref, kg_ref, mk_ref, mv_ref, mkb_ref, mvb_ref):
    h = _rms(mem_ref[...], g_ref[...]).astype(BF16)
    k = _dot(h, wk_ref[...])
    v = _dot(h, wv_ref[...])
    kg = kg_ref[...]
    k = jnp.concatenate([_rms(k[:, i * X_HD:(i + 1) * X_HD], kg) for i in range(X_HEADS)], axis=1)
    mk_ref[...] = k
    mv_ref[...] = v
    mkb_ref[...] = k.astype(BF16)
    mvb_ref[...] = v.astype(BF16)


def _memory_kv(mem, g, wk, wv, kg):
    b, n, _ = mem.shape
    blk = pl.BlockSpec((None, n, X_W), lambda i: (i, 0, 0))
    return pl.pallas_call(
        _memkv_kernel,
        grid=(b,),
        in_specs=[pl.BlockSpec((None, n, D_MODEL), lambda i: (i, 0, 0)), _full((1, D_MODEL)),
                  _full((D_MODEL, X_W)), _full((D_MODEL, X_W)), _full((1, X_HD))],
        out_specs=[blk, blk, blk, blk],
        out_shape=[jax.ShapeDtypeStruct((b, n, X_W), F32), jax.ShapeDtypeStruct((b, n, X_W), F32),
                   jax.ShapeDtypeStruct((b, n, X_W), BF16), jax.ShapeDtypeStruct((b, n, X_W), BF16)],
        compiler_params=_params(("arbitrary",)),
        name="memory_kv",
    )(mem, g, wk, wv, kg)


def _qkv_from_h(h, w_ref, qg, kg):
    z = _dot(h, w_ref[...])
    gm = _group_mean_matrix(A_W, A_HD)
    q = _group_rms(z[:, :A_W], qg, gm) * (A_HD ** -0.5)
    k = _group_rms(z[:, A_W:2 * A_W], kg, gm)
    v = z[:, 2 * A_W:]
    return q, k, v


def _qkv_kernel(x_ref, g_ref, w_ref, qg_ref, kg_ref, k_ref, v_ref, qb_ref, kb_ref, vb_ref):
    h = _rms(x_ref[...], g_ref[...]).astype(BF16)
    q, k, v = _qkv_from_h(h, w_ref, qg_ref[...], kg_ref[...])
    k_ref[...] = k
    v_ref[...] = v
    qb_ref[...] = q.astype(BF16)
    kb_ref[...] = k.astype(BF16)
    vb_ref[...] = v.astype(BF16)


def _qkv_proj(x, g, w_a, qg, kg):
    b, t, _ = x.shape
    blk = pl.BlockSpec((None, _TM, A_W), lambda i, j: (i, j, 0))
    f32 = jax.ShapeDtypeStruct((b, t, A_W), F32)
    b16 = jax.ShapeDtypeStruct((b, t, A_W), BF16)
    return pl.pallas_call(
        _qkv_kernel,
        grid=(b, t // _TM),
        in_specs=[pl.BlockSpec((None, _TM, D_MODEL), lambda i, j: (i, j, 0)), _full((1, D_MODEL)),
                  _full((D_MODEL, 3 * A_W)), _full((1, A_W)), _full((1, A_W))],
        out_specs=[blk] * 5,
        out_shape=[f32, f32, b16, b16, b16],
        compiler_params=_params(("arbitrary", "arbitrary")),
        name="qkv_proj",
    )(x, g, w_a, qg, kg)


def _diff_attn_kernel(slope_ref, q_ref, k_ref, v_ref, lam_ref, sg_ref, o_ref,
                      qs_scr, m_scr, l_scr, acc_scr, *, lam_init):
    h = pl.program_id(1)
    q0 = pl.program_id(2) * _TQ
    rows = 2 * _TQ

    q = q_ref[...]
    lane = lax.broadcasted_iota(jnp.int32, q.shape, 1)
    zero = jnp.zeros_like(q)
    qs_scr[0:_TQ, :] = jnp.where(lane < A_HD, q, zero)
    qs_scr[_TQ:rows, :] = jnp.where(lane >= A_HD, q, zero)
    m_scr[...] = jnp.full(m_scr.shape, NEG_INF, F32)
    l_scr[...] = jnp.zeros(l_scr.shape, F32)
    acc_scr[...] = jnp.zeros(acc_scr.shape, F32)
    slope = slope_ref[h]

    def chunk(j, masked):
        k0 = pl.multiple_of(j * _TK, _TK)
        s = _dot_nt(qs_scr[...], k_ref[pl.ds(k0, _TK), :])
        kpos = k0 + lax.broadcasted_iota(jnp.int32, (1, _TK), 1)
        s = s + slope * kpos.astype(F32)
        if masked:
            row = lax.broadcasted_iota(jnp.int32, (rows, _TK), 0)
            qpos = q0 + jnp.where(row >= _TQ, row - _TQ, row)
            col = k0 + lax.broadcasted_iota(jnp.int32, (rows, _TK), 1)
            s = jnp.where(qpos >= col, s, NEG_INF)
        m_old = m_scr[...]
        m_new = jnp.maximum(m_old, jnp.max(s, axis=-1, keepdims=True))
        alpha = jnp.exp(m_old - m_new)
        p = jnp.exp(s - jnp.concatenate([m_new] * (_TK // A_VD), axis=1))
        l_scr[...] = alpha * l_scr[...] + jnp.sum(p, axis=-1, keepdims=True)
        acc_scr[...] = alpha * acc_scr[...] + _dot(p.astype(BF16), v_ref[pl.ds(k0, _TK), :])
        m_scr[...] = m_new

    n_full = q0 // _TK

    def body(j, carry):
        chunk(j, False)
        return carry

    lax.fori_loop(0, n_full, body, 0)
    for d in range(_TQ // _TK):
        chunk(n_full + d, True)

    o = acc_scr[...] / l_scr[...]
    o = o[0:_TQ] - _lam(lam_ref, lam_init) * o[_TQ:rows]
    o_ref[...] = (_rms(o, sg_ref[...]) * (1.0 - lam_init)).astype(BF16)


def _diff_attention(slopes, qb, kb, vb, a_lam, sg, lam_init):
    b, t, _ = qb.shape
    assert _TQ % _TK == 0 and t % _TQ == 0
    qblk = pl.BlockSpec((None, _TQ, A_VD), lambda bi, h, qi: (bi, qi, h))
    kvblk = pl.BlockSpec((None, t, A_VD), lambda bi, h, qi: (bi, 0, h))
    return pl.pallas_call(
        functools.partial(_diff_attn_kernel, lam_init=lam_init),
        grid=(b, A_HEADS, t // _TQ),
        in_specs=[pl.BlockSpec(memory_space=pltpu.SMEM), qblk, kvblk, kvblk,
                  _full((4, A_HD)), _full((1, A_VD))],
        out_specs=qblk,
        out_shape=jax.ShapeDtypeStruct((b, t, A_W), BF16),
        scratch_shapes=[pltpu.VMEM((2 * _TQ, A_VD), BF16), pltpu.VMEM((2 * _TQ, A_VD), F32),
                        pltpu.VMEM((2 * _TQ, A_VD), F32), pltpu.VMEM((2 * _TQ, A_VD), F32)],
        compiler_params=_params(("arbitrary",) * 3),
        name="diff_attention",
    )(slopes, qb, kb, vb, a_lam, sg)


def _lane_group_mask(width, group, g):
    lane = lax.broadcasted_iota(jnp.int32, (1, width), 1)
    return (_idiv(lane, group) == g).astype(F32)


def _mixer_kernel(x_ref, o_ref, g_ref, wr_ref, wa_ref, wb_ref, wc_ref, wd_ref, wo_ref,
                  bcw_ref, bcb_ref, ccw_ref, ccb_ref, clg_ref, clb_ref, dlg_ref, dlb_ref,
                  dws_ref, dbst_ref, x1_ref, cbp_ref, ccp_ref, ubuf, cbuf, ycin):
    tm = _TM

    @pl.when(pl.program_id(1) == 0)
    def _():
        ubuf[0:_B_HALO, :] = jnp.zeros((_B_HALO, B_W), F32)
        cbuf[0:_C_HALO, :] = jnp.zeros((_C_HALO, C_W), F32)

    x = x_ref[...]
    h = _rms(x, g_ref[...]).astype(BF16)

    def gate(i):
        return jax.nn.sigmoid(_dot(h, wr_ref[:, _R_GZ + i * D_MODEL:_R_GZ + (i + 1) * D_MODEL]))

    merged = gate(0) * _dot(o_ref[...], wa_ref[...])

    zb = _dot(h, wr_ref[:, _R_BX:_R_CA])
    u = zb[:, 512:768] * zb[:, 0:256]
    ubuf[_B_HALO:_B_HALO + tm, :] = u
    conv_b = (bcw_ref[0:1, :] * ubuf[_B_HALO - 2:_B_HALO - 2 + tm, :]
              + bcw_ref[1:2, :] * ubuf[_B_HALO - 1:_B_HALO - 1 + tm, :]
              + bcw_ref[2:3, :] * u + bcb_ref[...])
    yb = _dot((zb[:, 256:512] * conv_b).astype(BF16), wb_ref[...])
    merged = merged + gate(1) * yb
    cbp_ref[...] = ubuf[_B_HALO + tm - (B_K - 1):_B_HALO + tm, :]
    ubuf[0:_B_HALO, :] = ubuf[tm:tm + _B_HALO, :]

    zc = _dot(h, wr_ref[:, _R_CA:_R_DU])
    cbuf[_C_HALO:_C_HALO + tm, :] = zc[:, 0:256] * jax.nn.sigmoid(zc[:, 256:512])
    base = _C_HALO - (C_K - 1)
    for r in range(0, tm, _CONV_ROWS):
        acc = jnp.broadcast_to(ccb_ref[...], (_CONV_ROWS, C_W))
        for k in range(C_K):
            acc = acc + ccw_ref[k:k + 1, :] * cbuf[base + r + k:base + r + k + _CONV_ROWS, :]
        ycin[r:r + _CONV_ROWS, :] = jax.nn.silu(
            _layer_norm(acc, clg_ref[...], clb_ref[...])).astype(BF16)
    yc = _dot(ycin[...], wc_ref[...])
    merged = merged + gate(2) * yc
    ccp_ref[...] = cbuf[_C_HALO + tm - (C_K - 1):_C_HALO + tm, :]
    cbuf[0:_C_HALO, :] = cbuf[tm:tm + _C_HALO, :]

    zd = jax.nn.gelu(_dot(h, wr_ref[:, _R_DU:_R_GZ]))
    du = zd[:, 0:256]
    dvn = _layer_norm(zd[:, 256:512], dlg_ref[...], dlb_ref[...])
    row = lax.broadcasted_iota(jnp.int32, (CHUNK, CHUNK), 0)
    col = lax.broadcasted_iota(jnp.int32, (CHUNK, CHUNK), 1)
    tril = (row >= col).astype(F32)
    wcat = jnp.concatenate([dws_ref[g] * tril for g in range(D_GROUPS)], axis=1).astype(BF16)
    masks = [_lane_group_mask(D_W, D_W // D_GROUPS, g) for g in range(D_GROUPS)]
    bsmat = sum(dbst_ref[:, g:g + 1] * masks[g] for g in range(D_GROUPS))
    for c in range(0, tm, CHUNK):
        vch = dvn[c:c + CHUNK]
        rhs = jnp.concatenate([(vch * masks[g]).astype(BF16) for g in range(D_GROUPS)], axis=0)
        s = _dot(wcat, rhs) + bsmat
        ycin[c:c + CHUNK, :] = (du[c:c + CHUNK] * s).astype(BF16)
    yd = _dot(ycin[...], wd_ref[...])
    merged = merged + gate(3) * yd

    x1_ref[...] = x + _dot(merged.astype(BF16), wo_ref[...])


def _mixer(x, o, p):
    b, t, _ = x.shape
    row = lambda w: pl.BlockSpec((None, _TM, w), lambda i, j: (i, j, 0))
    return pl.pallas_call(
        _mixer_kernel,
        grid=(b, t // _TM),
        in_specs=[row(D_MODEL), row(A_W), _full((1, D_MODEL)), _full((D_MODEL, _R_COLS)),
                  _full((A_W, D_MODEL)), _full((B_W, D_MODEL)), _full((C_W, D_MODEL)),
                  _full((D_W, D_MODEL)), _full((D_MODEL, D_MODEL)),
                  _full((B_K, B_W)), _full((1, B_W)), _full((C_K, C_W)), _full((1, C_W)),
                  _full((1, C_W)), _full((1, C_W)), _full((1, D_W)), _full((1, D_W)),
                  _full((D_GROUPS, CHUNK, CHUNK)), _full((CHUNK, D_GROUPS))],
        out_specs=[row(D_MODEL),
                   pl.BlockSpec((None, B_K - 1, B_W), lambda i, j: (i, 0, 0)),
                   pl.BlockSpec((None, C_K - 1, C_W), lambda i, j: (i, 0, 0))],
        out_shape=[jax.ShapeDtypeStruct((b, t, D_MODEL), F32),
                   jax.ShapeDtypeStruct((b, B_K - 1, B_W), F32),
                   jax.ShapeDtypeStruct((b, C_K - 1, C_W), F32)],
        scratch_shapes=[pltpu.VMEM((_B_HALO + _TM, B_W), F32), pltpu.VMEM((_C_HALO + _TM, C_W), F32),
                        pltpu.VMEM((_TM, C_W), BF16)],
        compiler_params=_params(("arbitrary", "arbitrary")),
        name="mixer",
    )(x, o, p["norm_mix_g"], p["w_in_r"], p["w_a_out"], p["w_b_out"], p["w_c_out"], p["w_d_out"],
      p["w_o"], p["b_conv_w"], p["b_conv_b"], p["c_conv_w"], p["c_conv_b"], p["c_ln_g"], p["c_ln_b"],
      p["d_ln_g"], p["d_ln_b"], p["d_ws"], p["d_bs_t"])


def _ffn(x2, gf, wup_ref, wdn_ref):
    h3 = _rms(x2, gf).astype(BF16)
    acc = jnp.zeros(x2.shape, F32)
    for c in range(0, D_FF, _FF_CHUNK):
        a = jnp.maximum(_dot(h3, wup_ref[:, c:c + _FF_CHUNK]), 0.0)
        acc = acc + _dot((a * a).astype(BF16), wdn_ref[c:c + _FF_CHUNK, :])
    return x2 + acc


def _xffn_kernel(x_ref, mk_ref, mv_ref, gx_ref, wxq_ref, qg_ref, wxo_ref, gf_ref, wup_ref, wdn_ref,
                 out_ref):
    x = x_ref[...]
    h2 = _rms(x, gx_ref[...]).astype(BF16)
    q = _dot(h2, wxq_ref[...])
    qg = qg_ref[...]
    oms = []
    for i in range(X_HEADS):
        sl = slice(i * X_HD, (i + 1) * X_HD)
        qh = (_rms(q[:, sl], qg) * (X_HD ** -0.5)).astype(BF16)
        s = _dot_nt(qh, mk_ref[:, sl])
        p = jnp.exp(s - jnp.max(s, axis=-1, keepdims=True))
        p = p / jnp.sum(p, axis=-1, keepdims=True)
        oms.append(_dot(p.astype(BF16), mv_ref[:, sl]))
    om = jnp.concatenate(oms, axis=1).astype(BF16)
    x2 = x + _dot(om, wxo_ref[...])
    out_ref[...] = _ffn(x2, gf_ref[...], wup_ref, wdn_ref)


def _xattn_ffn(x, mkb, mvb, p):
    b, t, _ = x.shape
    n = mkb.shape[1]
    row = pl.BlockSpec((None, _TM, D_MODEL), lambda i, j: (i, j, 0))
    mem = pl.BlockSpec((None, n, X_W), lambda i, j: (i, 0, 0))
    return pl.pallas_call(
        _xffn_kernel,
        grid=(b, t // _TM),
        in_specs=[row, mem, mem, _full((1, D_MODEL)), _full((D_MODEL, X_W)), _full((1, X_HD)),
                  _full((X_W, D_MODEL)), _full((1, D_MODEL)), _full((D_MODEL, D_FF)),
                  _full((D_FF, D_MODEL))],
        out_specs=row,
        out_shape=jax.ShapeDtypeStruct((b, t, D_MODEL), F32),
        compiler_params=_params(("arbitrary", "arbitrary")),
        name="xattn_ffn",
    )(x, mkb, mvb, p["norm_x_g"], p["w_xq"], p["x_qnorm_g"], p["w_xo"], p["norm_ffn_g"],
      p["w_up"], p["w_down"])


def _sample_mixer_kernel(x_ref, sb_ref, sc_ref, g_ref, wa_ref, wr_ref, qg_ref, kg_ref,
                         wb_ref, wc_ref, wd_ref, bcw_ref, bcb_ref, ccw_ref, ccb_ref, clg_ref, clb_ref,
                         dlg_ref, dlb_ref, dws_ref, dbs_ref,
                         q_ref, k_ref, v_ref, cb_ref, cc_ref, dvn_ref, part_ref, g0_ref):
    x = x_ref[...]
    h = _rms(x, g_ref[...]).astype(BF16)
    q, k, v = _qkv_from_h(h, wa_ref, qg_ref[...], kg_ref[...])
    q_ref[...] = q
    k_ref[...] = k
    v_ref[...] = v

    def gate(i):
        return jax.nn.sigmoid(_dot(h, wr_ref[:, _R_GZ + i * D_MODEL:_R_GZ + (i + 1) * D_MODEL]))

    g0_ref[...] = gate(0)

    zb = _dot(h, wr_ref[:, _R_BX:_R_CA])
    u = zb[:, 512:768] * zb[:, 0:256]
    conv_b = (bcw_ref[0:1, :] * sb_ref[:, 0:B_W] + bcw_ref[1:2, :] * sb_ref[:, B_W:2 * B_W]
              + bcw_ref[2:3, :] * u + bcb_ref[...])
    part = gate(1) * _dot((zb[:, 256:512] * conv_b).astype(BF16), wb_ref[...])
    cb_ref[:, 0:B_W] = sb_ref[:, B_W:2 * B_W]
    cb_ref[:, B_W:2 * B_W] = u

    zc = _dot(h, wr_ref[:, _R_CA:_R_DU])
    uc = zc[:, 0:256] * jax.nn.sigmoid(zc[:, 256:512])
    acc = ccw_ref[C_K - 1:C_K, :] * uc + ccb_ref[...]
    for kk in range(C_K - 1):
        acc = acc + ccw_ref[kk:kk + 1, :] * sc_ref[:, kk * C_W:(kk + 1) * C_W]
    yc = _dot(jax.nn.silu(_layer_norm(acc, clg_ref[...], clb_ref[...])).astype(BF16), wc_ref[...])
    part = part + gate(2) * yc
    cc_ref[:, 0:(C_K - 2) * C_W] = sc_ref[:, C_W:(C_K - 1) * C_W]
    cc_ref[:, (C_K - 2) * C_W:(C_K - 1) * C_W] = uc

    zd = jax.nn.gelu(_dot(h, wr_ref[:, _R_DU:_R_GZ]))
    dvn = _layer_norm(zd[:, 256:512], dlg_ref[...], dlb_ref[...])
    dvn_ref[...] = dvn
    w00 = sum(dws_ref[g, 0:1, 0:1] * _lane_group_mask(D_W, D_W // D_GROUPS, g) for g in range(D_GROUPS))
    b0 = sum(dbs_ref[g:g + 1, 0:1] * _lane_group_mask(D_W, D_W // D_GROUPS, g) for g in range(D_GROUPS))
    yd = _dot((zd[:, 0:256] * (w00 * dvn + b0)).astype(BF16), wd_ref[...])
    part_ref[...] = part + gate(3) * yd


def _sample_mixer(x, sb, sc, p):
    n = x.shape[0]
    shapes = [(n, A_W), (n, A_W), (n, A_W), (n, (B_K - 1) * B_W), (n, (C_K - 1) * C_W), (n, D_W),
              (n, D_MODEL), (n, D_MODEL)]
    ins = [x, sb, sc, p["norm_mix_g"], p["w_in_a"], p["w_in_r"], p["a_qnorm_g"], p["a_knorm_g"],
           p["w_b_out"], p["w_c_out"], p["w_d_out"], p["b_conv_w"], p["b_conv_b"], p["c_conv_w"],
           p["c_conv_b"], p["c_ln_g"], p["c_ln_b"], p["d_ln_g"], p["d_ln_b"], p["d_ws"], p["d_bs"]]
    return pl.pallas_call(
        _sample_mixer_kernel,
        grid=(1,),
        in_specs=[_full(a.shape) for a in ins],
        out_specs=[_full(s) for s in shapes],
        out_shape=[jax.ShapeDtypeStruct(s, F32) for s in shapes],
        compiler_params=_params(("arbitrary",)),
        name="sample_mixer",
    )(*ins)


_DEC_ROWS = 2 * A_HEADS
_PAGE_ROWS = PAGE_SIZE * A_HEADS


def _head_rows(ref, row, per_head):
    return sum(jnp.where(_idiv(row, per_head) == h, jnp.broadcast_to(ref[h:h + 1, :], row.shape), 0.0)
               for h in range(ref.shape[0]))


def _decode_attn_kernel(pt_ref, q_ref, kn_ref, vn_ref, lam_ref, sg_ref, *rest, lam_init, past_len):
    del pt_ref
    k_refs = rest[:_DEC_PAGES]
    v_refs = rest[_DEC_PAGES:2 * _DEC_PAGES]
    o_ref, qz_scr, m_scr, l_scr, acc_scr = rest[2 * _DEC_PAGES:]
    j = pl.program_id(1)
    nj = pl.num_programs(1)
    width = _DEC_PAGES * _PAGE_ROWS

    row = lax.broadcasted_iota(jnp.int32, (_DEC_ROWS, A_VD), 0)
    lane = lax.broadcasted_iota(jnp.int32, (_DEC_ROWS, A_VD), 1)
    rowv = lax.broadcasted_iota(jnp.int32, (_DEC_ROWS, 1), 0)
    slope = sum(jnp.where(_idiv(rowv, 2) == i, 2.0 ** (-8.0 * (i + 1) / A_HEADS), 0.0)
                for i in range(A_HEADS))

    @pl.when(j == 0)
    def _():
        qz_scr[...] = jnp.where(_idiv(lane, A_HD) == (row & 1), _head_rows(q_ref, row, 2), 0.0)
        m_scr[...] = jnp.full(m_scr.shape, NEG_INF, F32)
        l_scr[...] = jnp.zeros(l_scr.shape, F32)
        acc_scr[...] = jnp.zeros(acc_scr.shape, F32)

    qz = qz_scr[...].astype(BF16)
    s = jnp.concatenate([_dot_nt(qz, k_refs[i][...].astype(BF16)) for i in range(_DEC_PAGES)], axis=1)
    col = lax.broadcasted_iota(jnp.int32, (_DEC_ROWS, width), 1)
    kpos = j * (_DEC_PAGES * PAGE_SIZE) + _idiv(lax.broadcasted_iota(jnp.int32, (1, width), 1), A_HEADS)
    own_head = (col & (A_HEADS - 1)) == _idiv(lax.broadcasted_iota(jnp.int32, (_DEC_ROWS, width), 0), 2)
    s = jnp.where(own_head, s - slope * (past_len - kpos).astype(F32), NEG_INF)
    m_old = m_scr[...]
    m_new = jnp.maximum(m_old, jnp.max(s, axis=-1, keepdims=True))
    alpha = jnp.exp(m_old - m_new)
    p = jnp.exp(s - m_new)
    l_scr[...] = alpha * l_scr[...] + jnp.sum(p, axis=-1, keepdims=True)
    p = p.astype(BF16)
    pv = sum(_dot(p[:, i * _PAGE_ROWS:(i + 1) * _PAGE_ROWS], v_refs[i][...].astype(BF16))
             for i in range(_DEC_PAGES))
    acc_scr[...] = alpha * acc_scr[...] + pv
    m_scr[...] = m_new

    @pl.when(j == nj - 1)
    def _():
        s_new = jnp.sum(qz_scr[...] * _head_rows(kn_ref, row, 2), axis=-1, keepdims=True)
        m_old = m_scr[...]
        m_fin = jnp.maximum(m_old, s_new)
        alpha = jnp.exp(m_old - m_fin)
        p_new = jnp.exp(s_new - m_fin)
        l_fin = alpha * l_scr[...] + p_new
        o_all = (alpha * acc_scr[...] + p_new * _head_rows(vn_ref, row, 2)) / l_fin
        lam = _lam(lam_ref, lam_init)
        sg = sg_ref[...]
        for h in range(A_HEADS):
            o = o_all[2 * h:2 * h + 1] - lam * o_all[2 * h + 1:2 * h + 2]
            o_ref[h:h + 1, :] = _rms(o, sg) * (1.0 - lam_init)


def _decode_attention(page_table, q, k_new, v_new, cache_k, cache_v, layer, a_lam, sg, lam_init):
    n, n_pages = page_table.shape
    past_len = n_pages * PAGE_SIZE
    row = pl.BlockSpec((None, A_HEADS, A_VD), lambda b, j, pt: (b, 0, 0))

    def page(i):
        return pl.BlockSpec((None, None, _PAGE_ROWS, A_VD),
                            lambda b, j, pt: (layer, pt[b, j * _DEC_PAGES + i], 0, 0))

    grid_spec = pltpu.PrefetchScalarGridSpec(
        num_scalar_prefetch=1,
        grid=(n, n_pages // _DEC_PAGES),
        in_specs=[row, row, row, pl.BlockSpec((4, A_HD), lambda b, j, pt: (0, 0)),
                  pl.BlockSpec((1, A_VD), lambda b, j, pt: (0, 0))]
        + [page(i) for i in range(_DEC_PAGES)] + [page(i) for i in range(_DEC_PAGES)],
        out_specs=row,
        scratch_shapes=[pltpu.VMEM((_DEC_ROWS, A_VD), F32), pltpu.VMEM((_DEC_ROWS, 1), F32),
                        pltpu.VMEM((_DEC_ROWS, 1), F32), pltpu.VMEM((_DEC_ROWS, A_VD), F32)],
    )
    heads = lambda a: a.reshape(n, A_HEADS, A_VD)
    return pl.pallas_call(
        functools.partial(_decode_attn_kernel, lam_init=lam_init, past_len=past_len),
        grid_spec=grid_spec,
        out_shape=jax.ShapeDtypeStruct((n, A_HEADS, A_VD), F32),
        compiler_params=_params(("arbitrary", "arbitrary")),
        name="decode_attention",
    )(page_table, heads(q), heads(k_new), heads(v_new), a_lam, sg,
      *([cache_k] * _DEC_PAGES), *([cache_v] * _DEC_PAGES))


def _sample_tail_kernel(x_ref, o_ref, g0_ref, part_ref, mk_ref, mv_ref, wa_ref, wo_ref, gx_ref, wxq_ref,
                        qg_ref, wxo_ref, gf_ref, wup_ref, wdn_ref, out_ref, x1_scr, q_scr, om_scr):
    b = pl.program_id(0)
    nb = pl.num_programs(0)

    @pl.when(b == 0)
    def _():
        merged = g0_ref[...] * _dot(o_ref[...].astype(BF16), wa_ref[...]) + part_ref[...]
        x1 = x_ref[...] + _dot(merged.astype(BF16), wo_ref[...])
        x1_scr[...] = x1
        q = _dot(_rms(x1, gx_ref[...]).astype(BF16), wxq_ref[...])
        qg = qg_ref[...]
        for h in range(X_HEADS):
            q_scr[h] = _rms(q[:, h * X_HD:(h + 1) * X_HD], qg) * (X_HD ** -0.5)

    n_rows = mk_ref.shape[0]
    row = lax.broadcasted_iota(jnp.int32, (8, X_HD), 0)
    qz = sum(jnp.where(row == h, jnp.broadcast_to(q_scr[h, pl.ds(b, 1), :], (8, X_HD)), 0.0)
             for h in range(X_HEADS)).astype(BF16)
    s = _dot_nt(qz, mk_ref[...].astype(BF16))
    col = lax.broadcasted_iota(jnp.int32, (8, n_rows), 1)
    s = jnp.where((col & (X_HEADS - 1)) == lax.broadcasted_iota(jnp.int32, (8, n_rows), 0), s, NEG_INF)
    p = jnp.exp(s - jnp.max(s, axis=-1, keepdims=True))
    p = p / jnp.sum(p, axis=-1, keepdims=True)
    om = _dot(p.astype(BF16), mv_ref[...].astype(BF16))
    for h in range(X_HEADS):
        om_scr[h, pl.ds(b, 1), :] = om[h:h + 1, :]

    @pl.when(b == nb - 1)
    def _():
        x2 = x1_scr[...] + sum(_dot(om_scr[h].astype(BF16), wxo_ref[h * X_HD:(h + 1) * X_HD, :])
                               for h in range(X_HEADS))
        out_ref[...] = _ffn(x2, gf_ref[...], wup_ref, wdn_ref)


def _sample_tail(x, o, g0, part, mem_k, mem_v, layer, p):
    n = x.shape[0]
    n_rows = mem_k.shape[2]
    mem = pl.BlockSpec((None, None, n_rows, X_HD), lambda b: (layer, b, 0, 0))
    ins = [x, o, g0, part]
    ws = [p["w_a_out"], p["w_o"], p["norm_x_g"], p["w_xq"], p["x_qnorm_g"], p["w_xo"], p["norm_ffn_g"],
          p["w_up"], p["w_down"]]
    return pl.pallas_call(
        _sample_tail_kernel,
        grid=(n,),
        in_specs=[_full(a.shape) for a in ins] + [mem, mem] + [_full(a.shape) for a in ws],
        out_specs=_full((n, D_MODEL)),
        out_shape=jax.ShapeDtypeStruct((n, D_MODEL), F32),
        scratch_shapes=[pltpu.VMEM((n, D_MODEL), F32), pltpu.VMEM((X_HEADS, n, X_HD), F32),
                        pltpu.VMEM((X_HEADS, n, X_HD), F32)],
        compiler_params=_params(("arbitrary",)),
        name="sample_tail",
    )(*ins, mem_k, mem_v, *ws)


def _layer_params(l, a):
    row = lambda v: v[l].reshape(1, -1)
    b16 = lambda w: w.astype(BF16)
    w_in = a["w_in"][l]
    return dict(
        norm_mix_g=row(a["norm_mix_g"]),
        w_in_a=b16(w_in[:, :3 * A_W]), w_in_r=b16(w_in[:, 3 * A_W:]),
        a_qnorm_g=jnp.tile(a["a_qnorm_g"][l], A_W // A_HD).reshape(1, A_W),
        a_knorm_g=jnp.tile(a["a_knorm_g"][l], A_W // A_HD).reshape(1, A_W),
        a_lam=a["a_lam"][l], a_subln_g=row(a["a_subln_g"]),
        w_a_out=b16(a["w_a_out"][l]),
        b_conv_w=a["b_conv_w"][l], b_conv_b=row(a["b_conv_b"]), w_b_out=b16(a["w_b_out"][l]),
        c_conv_w=a["c_conv_w"][l], c_conv_b=row(a["c_conv_b"]),
        c_ln_g=row(a["c_ln_g"]), c_ln_b=row(a["c_ln_b"]), w_c_out=b16(a["w_c_out"][l]),
        d_ln_g=row(a["d_ln_g"]), d_ln_b=row(a["d_ln_b"]),
        d_ws=a["d_ws"][l], d_bs=a["d_bs"][l], d_bs_t=a["d_bs"][l].T, w_d_out=b16(a["w_d_out"][l]),
        w_o=b16(a["w_o"][l]),
        norm_x_g=row(a["norm_x_g"]), mem_norm_g=row(a["mem_norm_g"]),
        w_xq=b16(a["w_xq"][l]), w_xk=b16(a["w_xk"][l]), w_xv=b16(a["w_xv"][l]),
        x_qnorm_g=row(a["x_qnorm_g"]), x_knorm_g=row(a["x_knorm_g"]),
        w_xo=b16(a["w_xo"][l]), norm_ffn_g=row(a["norm_ffn_g"]),
        w_up=b16(a["w_up"][l]), w_down=b16(a["w_down"][l]),
    )


def kernel(x_prompt, x_sample, cache_k_a, cache_v_a, state_conv_b, state_conv_c, cache_mem_k, cache_mem_v,
           page_table, mem_prompt, norm_mix_g, w_in, a_qnorm_g, a_knorm_g, a_lam, a_subln_g, w_a_out,
           b_conv_w, b_conv_b, w_b_out, c_conv_w, c_conv_b, c_ln_g, c_ln_b, w_c_out, d_ln_g, d_ln_b,
           d_ws, d_bs, w_d_out, w_o, norm_x_g, mem_norm_g, w_xq, w_xk, w_xv, x_qnorm_g, x_knorm_g,
           w_xo, norm_ffn_g, w_up, w_down):
    weights = dict(norm_mix_g=norm_mix_g, w_in=w_in, a_qnorm_g=a_qnorm_g, a_knorm_g=a_knorm_g, a_lam=a_lam,
                   a_subln_g=a_subln_g, w_a_out=w_a_out, b_conv_w=b_conv_w, b_conv_b=b_conv_b,
                   w_b_out=w_b_out, c_conv_w=c_conv_w, c_conv_b=c_conv_b, c_ln_g=c_ln_g, c_ln_b=c_ln_b,
                   w_c_out=w_c_out, d_ln_g=d_ln_g, d_ln_b=d_ln_b, d_ws=d_ws, d_bs=d_bs, w_d_out=w_d_out,
                   w_o=w_o, norm_x_g=norm_x_g, mem_norm_g=mem_norm_g, w_xq=w_xq, w_xk=w_xk, w_xv=w_xv,
                   x_qnorm_g=x_qnorm_g, x_knorm_g=x_knorm_g, w_xo=w_xo, norm_ffn_g=norm_ffn_g,
                   w_up=w_up, w_down=w_down)
    depth = w_in.shape[0]
    bp, t, _ = x_prompt.shape
    ns = x_sample.shape[0]
    n_pool = cache_k_a.shape[1]
    n_mem = cache_mem_k.shape[2]
    cache_k = cache_k_a.reshape(depth, n_pool, _PAGE_ROWS, A_VD)
    cache_v = cache_v_a.reshape(depth, n_pool, _PAGE_ROWS, A_VD)
    mem_k_s = cache_mem_k.reshape(depth, ns, n_mem * X_HEADS, X_HD)
    mem_v_s = cache_mem_v.reshape(depth, ns, n_mem * X_HEADS, X_HD)
    slopes = jnp.asarray([2.0 ** (-8.0 * (i + 1) / A_HEADS) for i in range(A_HEADS)], F32)

    xp = x_prompt
    xs = x_sample.reshape(ns, D_MODEL)
    outs = [[] for _ in range(11)]
    for l in range(depth):
        p = _layer_params(l, weights)
        lam_init = 0.8 - 0.6 * math.exp(-0.3 * l)

        mk, mv, mkb, mvb = _memory_kv(mem_prompt, p["mem_norm_g"], p["w_xk"], p["w_xv"], p["x_knorm_g"])
        k_p, v_p, qb, kb, vb = _qkv_proj(xp, p["norm_mix_g"], p["w_in_a"], p["a_qnorm_g"], p["a_knorm_g"])
        o_p = _diff_attention(slopes, qb, kb, vb, p["a_lam"], p["a_subln_g"], lam_init)
        xp, cb_p, cc_p = _mixer(xp, o_p, p)
        xp = _xattn_ffn(xp, mkb, mvb, p)

        sb = state_conv_b[l].reshape(ns, (B_K - 1) * B_W)
        sc = state_conv_c[l].reshape(ns, (C_K - 1) * C_W)
        q_s, k_s, v_s, cb_s, cc_s, dvn_s, part, g0 = _sample_mixer(xs, sb, sc, p)
        o_s = _decode_attention(page_table, q_s, k_s, v_s, cache_k, cache_v, l, p["a_lam"],
                                p["a_subln_g"], lam_init)
        xs = _sample_tail(xs, o_s.reshape(ns, A_W), g0, part, mem_k_s, mem_v_s, l, p)

        for lst, val in zip(outs, (
                k_p.reshape(bp, t, A_HEADS, 2 * A_HD), v_p.reshape(bp, t, A_HEADS, A_VD), cb_p, cc_p,
                mk.reshape(bp, n_mem, X_HEADS, X_HD), mv.reshape(bp, n_mem, X_HEADS, X_HD),
                k_s.reshape(ns, 1, A_HEADS, 2 * A_HD), v_s.reshape(ns, 1, A_HEADS, A_VD),
                cb_s.reshape(ns, B_K - 1, B_W), cc_s.reshape(ns, C_K - 1, C_W),
                dvn_s.reshape(ns, 1, D_W))):
            lst.append(val)
    return (xp, xs.reshape(ns, 1, D_MODEL)) + tuple(jnp.stack(o) for o in outs)
```

```python
import functools
import math

import jax
import jax.numpy as jnp
from jax import lax
from jax.experimental import pallas as pl
from jax.experimental.pallas import tpu as pltpu

F32 = jnp.float32
BF16 = jnp.bfloat16

D_MODEL = 1024
A_HEADS = 4
A_HD = 64
A_VD = 2 * A_HD
A_W = A_HEADS * 2 * A_HD
B_W = 256
B_K = 3
C_W = 256
C_K = 31
D_W = 256
D_GROUPS = 4
CHUNK = 128
N_BRANCH = 4
X_HEADS = 4
X_HD = 128
X_W = X_HEADS * X_HD
D_FF = 4 * D_MODEL
PAGE_SIZE = 128
EPS = 1e-6
NEG_INF = -1e30
LOG2E = math.log2(math.e)

_R_BX, _R_BB, _R_BC = 0, 256, 512
_R_CA, _R_CG = 768, 1024
_R_DU, _R_DV = 1280, 1536
_R_GZ = 1792
_R_COLS = _R_GZ + N_BRANCH * D_MODEL

_V7X_VMEM_BYTES = 64 * 1024 * 1024
_VMEM_LIMIT = _V7X_VMEM_BYTES - 8 * 1024 * 1024

_TM = 512
_TQ = 512
_TK = 512
_SUBLANES = 8
_CONV_ROWS = 64
_B_HALO = 8
_C_HALO = 32
_FF_CHUNK = 1024
_DEC_PAGES = 16


def _rms(x, g):
    return x * lax.rsqrt(jnp.mean(x * x, axis=-1, keepdims=True) + EPS) * g


def _layer_norm(x, g, b):
    xc = x - jnp.mean(x, axis=-1, keepdims=True)
    var = jnp.mean(xc * xc, axis=-1, keepdims=True)
    return xc * lax.rsqrt(var + EPS) * g + b


def _sigmoid(x):
    return 0.5 * jnp.tanh(0.5 * x) + 0.5


def _dot(a, b):
    return jnp.dot(a, b, preferred_element_type=F32)


def _dot_nt(a, b):
    return lax.dot_general(a, b, (((1,), (1,)), ((), ())), preferred_element_type=F32)


def _idiv(x, d):
    assert d & (d - 1) == 0
    return lax.shift_right_logical(x, int(math.log2(d)))


def _group_mean_matrix(width, group):
    r = _idiv(lax.broadcasted_iota(jnp.int32, (width, width), 0), group)
    c = _idiv(lax.broadcasted_iota(jnp.int32, (width, width), 1), group)
    return jnp.where(r == c, 1.0 / group, 0.0).astype(BF16)


def _group_rms(t, g, gm):
    ms = _dot((t * t).astype(BF16), gm)
    return t * lax.rsqrt(ms + EPS) * g


def _lam(lam_ref, lam_init):
    a = lam_ref[...]
    s1 = jnp.sum(a[0:1] * a[1:2], axis=-1, keepdims=True)
    s2 = jnp.sum(a[2:3] * a[3:4], axis=-1, keepdims=True)
    return jnp.exp(s1) - jnp.exp(s2) + lam_init


def _full(shape):
    return pl.BlockSpec(shape, lambda *_: (0,) * len(shape))


def _params(sem):
    return pltpu.CompilerParams(dimension_semantics=sem, vmem_limit_bytes=_VMEM_LIMIT)


def _memkv_kernel(mem_ref, g_ref, wk_ref, wv_ref, kg_ref, mk_ref, mv_ref, mkb_ref, mvb_ref):
    h = _rms(mem_ref[...], g_ref[...]).astype(BF16)
    k = _dot(h, wk_ref[...])
    v = _dot(h, wv_ref[...])
    kg = kg_ref[...]
    k = jnp.concatenate([_rms(k[:, i * X_HD:(i + 1) * X_HD], kg) for i in range(X_HEADS)], axis=1)
    mk_ref[...] = k.reshape(mk_ref.shape)
    mv_ref[...] = v.reshape(mv_ref.shape)
    mkb_ref[...] = k.astype(BF16)
    mvb_ref[...] = v.astype(BF16)


def _memory_kv(mem, g, wk, wv, kg):
    b, n, _ = mem.shape
    blk = pl.BlockSpec((None, n, X_W), lambda i: (i, 0, 0))
    hblk = pl.BlockSpec((None, n, X_HEADS, X_HD), lambda i: (i, 0, 0, 0))
    return pl.pallas_call(
        _memkv_kernel,
        grid=(b,),
        in_specs=[pl.BlockSpec((None, n, D_MODEL), lambda i: (i, 0, 0)), _full((1, D_MODEL)),
                  _full((D_MODEL, X_W)), _full((D_MODEL, X_W)), _full((1, X_HD))],
        out_specs=[hblk, hblk, blk, blk],
        out_shape=[jax.ShapeDtypeStruct((b, n, X_HEADS, X_HD), F32),
                   jax.ShapeDtypeStruct((b, n, X_HEADS, X_HD), F32),
                   jax.ShapeDtypeStruct((b, n, X_W), BF16), jax.ShapeDtypeStruct((b, n, X_W), BF16)],
        compiler_params=_params(("arbitrary",)),
        name="memory_kv",
    )(mem, g, wk, wv, kg)


def _qkv_from_h(h, w_ref, qg, kg):
    z = _dot(h, w_ref[...])
    gm = _group_mean_matrix(A_W, A_HD)
    q = _group_rms(z[:, :A_W], qg, gm) * (A_HD ** -0.5)
    k = _group_rms(z[:, A_W:2 * A_W], kg, gm)
    v = z[:, 2 * A_W:]
    return q, k, v


def _qkv_kernel(x_ref, g_ref, w_ref, qg_ref, kg_ref, k_ref, v_ref, qb_ref, kb_ref, vb_ref):
    h = _rms(x_ref[...], g_ref[...]).astype(BF16)
    q, k, v = _qkv_from_h(h, w_ref, qg_ref[...], kg_ref[...])
    k_ref[...] = k.reshape(k_ref.shape)
    v_ref[...] = v.reshape(v_ref.shape)
    qb_ref[...] = (q * LOG2E).astype(BF16)
    kb_ref[...] = k.astype(BF16)
    vb_ref[...] = v.astype(BF16)


def _qkv_proj(x, g, w_a, qg, kg):
    b, t, _ = x.shape
    blk = pl.BlockSpec((None, _TM, A_W), lambda i, j: (i, j, 0))
    hblk = pl.BlockSpec((None, _TM, A_HEADS, A_VD), lambda i, j: (i, j, 0, 0))
    f32 = jax.ShapeDtypeStruct((b, t, A_HEADS, A_VD), F32)
    b16 = jax.ShapeDtypeStruct((b, t, A_W), BF16)
    return pl.pallas_call(
        _qkv_kernel,
        grid=(b, t // _TM),
        in_specs=[pl.BlockSpec((None, _TM, D_MODEL), lambda i, j: (i, j, 0)), _full((1, D_MODEL)),
                  _full((D_MODEL, 3 * A_W)), _full((1, A_W)), _full((1, A_W))],
        out_specs=[hblk, hblk, blk, blk, blk],
        out_shape=[f32, f32, b16, b16, b16],
        compiler_params=_params(("arbitrary", "arbitrary")),
        name="qkv_proj",
    )(x, g, w_a, qg, kg)


def _diff_attn_kernel(slope_ref, q_ref, k_ref, v_ref, lam_ref, sg_ref, o_ref,
                      qs_scr, m_scr, l_scr, acc_scr, *, lam_init):
    h = pl.program_id(1)
    q0 = pl.program_id(2) * _TQ
    rows = 2 * _TQ

    q = q_ref[...]
    lane = lax.broadcasted_iota(jnp.int32, q.shape, 1)
    zero = jnp.zeros_like(q)
    qs_scr[0:_TQ, :] = jnp.where(lane < A_HD, q, zero)
    qs_scr[_TQ:rows, :] = jnp.where(lane >= A_HD, q, zero)
    m_scr[...] = jnp.full(m_scr.shape, NEG_INF, F32)
    l_scr[...] = jnp.zeros(l_scr.shape, F32)
    acc_scr[...] = jnp.zeros(acc_scr.shape, F32)
    slope = slope_ref[h]

    def chunk(j, masked):
        k0 = pl.multiple_of(j * _TK, _TK)
        s = _dot_nt(qs_scr[...], k_ref[pl.ds(k0, _TK), :])
        kpos = k0 + lax.broadcasted_iota(jnp.int32, (1, _TK), 1)
        s = s + (slope * LOG2E) * kpos.astype(F32)
        if masked:
            row = lax.broadcasted_iota(jnp.int32, (rows, _TK), 0)
            qpos = q0 + jnp.where(row >= _TQ, row - _TQ, row)
            col = k0 + lax.broadcasted_iota(jnp.int32, (rows, _TK), 1)
            s = jnp.where(qpos >= col, s, NEG_INF)
        m_old = m_scr[...]
        m_new = jnp.maximum(m_old, jnp.max(s, axis=-1, keepdims=True))
        alpha = jnp.exp2(m_old - m_new)
        p = jnp.exp2(s - jnp.concatenate([m_new] * (_TK // A_VD), axis=1))
        l_scr[...] = alpha * l_scr[...] + jnp.sum(p, axis=-1, keepdims=True)
        acc_scr[...] = alpha * acc_scr[...] + _dot(p.astype(BF16), v_ref[pl.ds(k0, _TK), :])
        m_scr[...] = m_new

    n_full = q0 // _TK

    def body(j, carry):
        chunk(j, False)
        return carry

    lax.fori_loop(0, n_full, body, 0)
    for d in range(_TQ // _TK):
        chunk(n_full + d, True)

    o = acc_scr[...] / l_scr[...]
    o = o[0:_TQ] - _lam(lam_ref, lam_init) * o[_TQ:rows]
    o_ref[...] = (_rms(o, sg_ref[...]) * (1.0 - lam_init)).astype(BF16)


def _diff_attention(slopes, qb, kb, vb, a_lam, sg, lam_init):
    b, t, _ = qb.shape
    assert _TQ % _TK == 0 and t % _TQ == 0
    qblk = pl.BlockSpec((None, _TQ, A_VD), lambda bi, h, qi: (bi, qi, h))
    kvblk = pl.BlockSpec((None, t, A_VD), lambda bi, h, qi: (bi, 0, h))
    return pl.pallas_call(
        functools.partial(_diff_attn_kernel, lam_init=lam_init),
        grid=(b, A_HEADS, t // _TQ),
        in_specs=[pl.BlockSpec(memory_space=pltpu.SMEM), qblk, kvblk, kvblk,
                  _full((4, A_HD)), _full((1, A_VD))],
        out_specs=qblk,
        out_shape=jax.ShapeDtypeStruct((b, t, A_W), BF16),
        scratch_shapes=[pltpu.VMEM((2 * _TQ, A_VD), BF16), pltpu.VMEM((2 * _TQ, A_VD), F32),
                        pltpu.VMEM((2 * _TQ, A_VD), F32), pltpu.VMEM((2 * _TQ, A_VD), F32)],
        compiler_params=_params(("arbitrary",) * 3),
        name="diff_attention",
    )(slopes, qb, kb, vb, a_lam, sg)


def _lane_group_mask(width, group, g):
    lane = lax.broadcasted_iota(jnp.int32, (1, width), 1)
    return (_idiv(lane, group) == g).astype(F32)


def _mixer_kernel(x_ref, o_ref, g_ref, wr_ref, wa_ref, wb_ref, wc_ref, wd_ref, wo_ref,
                  bcw_ref, bcb_ref, ccw_ref, ccb_ref, clg_ref, clb_ref, dlg_ref, dlb_ref,
                  dws_ref, dbst_ref, x1_ref, cbp_ref, ccp_ref, ubuf, cbuf, ycin, shwin):
    tm = _TM

    @pl.when(pl.program_id(1) == 0)
    def _():
        ubuf[0:_B_HALO, :] = jnp.zeros((_B_HALO, B_W), F32)
        cbuf[0:_C_HALO, :] = jnp.zeros((_C_HALO, C_W), F32)

    x = x_ref[...]
    h = _rms(x, g_ref[...]).astype(BF16)

    def gate(i):
        return _sigmoid(_dot(h, wr_ref[:, _R_GZ + i * D_MODEL:_R_GZ + (i + 1) * D_MODEL]))

    merged = gate(0) * _dot(o_ref[...], wa_ref[...])

    zb = _dot(h, wr_ref[:, _R_BX:_R_CA])
    u = zb[:, 512:768] * zb[:, 0:256]
    ubuf[_B_HALO:_B_HALO + tm, :] = u
    conv_b = (bcw_ref[0:1, :] * ubuf[_B_HALO - 2:_B_HALO - 2 + tm, :]
              + bcw_ref[1:2, :] * ubuf[_B_HALO - 1:_B_HALO - 1 + tm, :]
              + bcw_ref[2:3, :] * u + bcb_ref[...])
    yb = _dot((zb[:, 256:512] * conv_b).astype(BF16), wb_ref[...])
    merged = merged + gate(1) * yb
    cbp_ref[...] = ubuf[_B_HALO + tm - (B_K - 1):_B_HALO + tm, :]
    ubuf[0:_B_HALO, :] = ubuf[tm:tm + _B_HALO, :]

    zc = _dot(h, wr_ref[:, _R_CA:_R_DU])
    cbuf[_C_HALO:_C_HALO + tm, :] = zc[:, 0:256] * _sigmoid(zc[:, 256:512])
    base = _C_HALO - (C_K - 1)
    for r in range(0, tm, _CONV_ROWS):
        acc = jnp.broadcast_to(ccb_ref[...], (_CONV_ROWS, C_W))
        for ph in range(_SUBLANES):
            n_taps = (C_K - 1 - ph) // _SUBLANES + 1
            rows = _CONV_ROWS + _SUBLANES * (n_taps - 1)
            shwin[ph, 0:rows, :] = cbuf[base + r + ph:base + r + ph + rows, :]
            for i in range(n_taps):
                k = ph + _SUBLANES * i
                acc = acc + ccw_ref[k:k + 1, :] * shwin[ph, _SUBLANES * i:_SUBLANES * i + _CONV_ROWS, :]
        y = _layer_norm(acc, clg_ref[...], clb_ref[...])
        ycin[r:r + _CONV_ROWS, :] = (y * _sigmoid(y)).astype(BF16)
    yc = _dot(ycin[...], wc_ref[...])
    merged = merged + gate(2) * yc
    ccp_ref[...] = cbuf[_C_HALO + tm - (C_K - 1):_C_HALO + tm, :]
    cbuf[0:_C_HALO, :] = cbuf[tm:tm + _C_HALO, :]

    zd = jax.nn.gelu(_dot(h, wr_ref[:, _R_DU:_R_GZ]))
    du = zd[:, 0:256]
    dvn = _layer_norm(zd[:, 256:512], dlg_ref[...], dlb_ref[...])
    row = lax.broadcasted_iota(jnp.int32, (CHUNK, CHUNK), 0)
    col = lax.broadcasted_iota(jnp.int32, (CHUNK, CHUNK), 1)
    tril = (row >= col).astype(F32)
    wcat = jnp.concatenate([dws_ref[g] * tril for g in range(D_GROUPS)], axis=1).astype(BF16)
    masks = [_lane_group_mask(D_W, D_W // D_GROUPS, g) for g in range(D_GROUPS)]
    bsmat = sum(dbst_ref[:, g:g + 1] * masks[g] for g in range(D_GROUPS))
    for c in range(0, tm, CHUNK):
        vch = dvn[c:c + CHUNK]
        rhs = jnp.concatenate([(vch * masks[g]).astype(BF16) for g in range(D_GROUPS)], axis=0)
        s = _dot(wcat, rhs) + bsmat
        ycin[c:c + CHUNK, :] = (du[c:c + CHUNK] * s).astype(BF16)
    yd = _dot(ycin[...], wd_ref[...])
    merged = merged + gate(3) * yd

    x1_ref[...] = x + _dot(merged.astype(BF16), wo_ref[...])


def _mixer(x, o, p):
    b, t, _ = x.shape
    row = lambda w: pl.BlockSpec((None, _TM, w), lambda i, j: (i, j, 0))
    return pl.pallas_call(
        _mixer_kernel,
        grid=(b, t // _TM),
        in_specs=[row(D_MODEL), row(A_W), _full((1, D_MODEL)), _full((D_MODEL, _R_COLS)),
                  _full((A_W, D_MODEL)), _full((B_W, D_MODEL)), _full((C_W, D_MODEL)),
                  _full((D_W, D_MODEL)), _full((D_MODEL, D_MODEL)),
                  _full((B_K, B_W)), _full((1, B_W)), _full((C_K, C_W)), _full((1, C_W)),
                  _full((1, C_W)), _full((1, C_W)), _full((1, D_W)), _full((1, D_W)),
                  _full((D_GROUPS, CHUNK, CHUNK)), _full((CHUNK, D_GROUPS))],
        out_specs=[row(D_MODEL),
                   pl.BlockSpec((None, B_K - 1, B_W), lambda i, j: (i, 0, 0)),
                   pl.BlockSpec((None, C_K - 1, C_W), lambda i, j: (i, 0, 0))],
        out_shape=[jax.ShapeDtypeStruct((b, t, D_MODEL), F32),
                   jax.ShapeDtypeStruct((b, B_K - 1, B_W), F32),
                   jax.ShapeDtypeStruct((b, C_K - 1, C_W), F32)],
        scratch_shapes=[pltpu.VMEM((_B_HALO + _TM, B_W), F32), pltpu.VMEM((_C_HALO + _TM, C_W), F32),
                        pltpu.VMEM((_TM, C_W), BF16),
                        pltpu.VMEM((_SUBLANES, _CONV_ROWS + _C_HALO, C_W), F32)],
        compiler_params=_params(("arbitrary", "arbitrary")),
        name="mixer",
    )(x, o, p["norm_mix_g"], p["w_in_r"], p["w_a_out"], p["w_b_out"], p["w_c_out"], p["w_d_out"],
      p["w_o"], p["b_conv_w"], p["b_conv_b"], p["c_conv_w"], p["c_conv_b"], p["c_ln_g"], p["c_ln_b"],
      p["d_ln_g"], p["d_ln_b"], p["d_ws"], p["d_bs_t"])


def _ffn(x2, gf, wup_ref, wdn_ref):
    h3 = _rms(x2, gf).astype(BF16)
    acc = jnp.zeros(x2.shape, F32)
    for c in range(0, D_FF, _FF_CHUNK):
        a = jnp.maximum(_dot(h3, wup_ref[:, c:c + _FF_CHUNK]), 0.0)
        acc = acc + _dot((a * a).astype(BF16), wdn_ref[c:c + _FF_CHUNK, :])
    return x2 + acc


def _xffn_kernel(x_ref, mk_ref, mv_ref, gx_ref, wxq_ref, qg_ref, wxo_ref, gf_ref, wup_ref, wdn_ref,
                 out_ref):
    x = x_ref[...]
    h2 = _rms(x, gx_ref[...]).astype(BF16)
    q = _dot(h2, wxq_ref[...])
    qg = qg_ref[...]
    oms = []
    for i in range(X_HEADS):
        sl = slice(i * X_HD, (i + 1) * X_HD)
        qh = (_rms(q[:, sl], qg) * (X_HD ** -0.5)).astype(BF16)
        s = _dot_nt(qh, mk_ref[:, sl])
        p = jnp.exp(s - jnp.max(s, axis=-1, keepdims=True))
        oms.append(_dot(p.astype(BF16), mv_ref[:, sl]) / jnp.sum(p, axis=-1, keepdims=True))
    om = jnp.concatenate(oms, axis=1).astype(BF16)
    x2 = x + _dot(om, wxo_ref[...])
    out_ref[...] = _ffn(x2, gf_ref[...], wup_ref, wdn_ref)


def _xattn_ffn(x, mkb, mvb, p):
    b, t, _ = x.shape
    n = mkb.shape[1]
    row = pl.BlockSpec((None, _TM, D_MODEL), lambda i, j: (i, j, 0))
    mem = pl.BlockSpec((None, n, X_W), lambda i, j: (i, 0, 0))
    return pl.pallas_call(
        _xffn_kernel,
        grid=(b, t // _TM),
        in_specs=[row, mem, mem, _full((1, D_MODEL)), _full((D_MODEL, X_W)), _full((1, X_HD)),
                  _full((X_W, D_MODEL)), _full((1, D_MODEL)), _full((D_MODEL, D_FF)),
                  _full((D_FF, D_MODEL))],
        out_specs=row,
        out_shape=jax.ShapeDtypeStruct((b, t, D_MODEL), F32),
        compiler_params=_params(("arbitrary", "arbitrary")),
        name="xattn_ffn",
    )(x, mkb, mvb, p["norm_x_g"], p["w_xq"], p["x_qnorm_g"], p["w_xo"], p["norm_ffn_g"],
      p["w_up"], p["w_down"])


def _sample_mixer_kernel(x_ref, sb_ref, sc_ref, g_ref, wa_ref, wr_ref, qg_ref, kg_ref,
                         wb_ref, wc_ref, wd_ref, bcw_ref, bcb_ref, ccw_ref, ccb_ref, clg_ref, clb_ref,
                         dlg_ref, dlb_ref, dws_ref, dbs_ref,
                         q_ref, k_ref, v_ref, cb_ref, cc_ref, dvn_ref, part_ref, g0_ref):
    x = x_ref[...]
    h = _rms(x, g_ref[...]).astype(BF16)
    q, k, v = _qkv_from_h(h, wa_ref, qg_ref[...], kg_ref[...])
    q_ref[...] = q
    k_ref[...] = k
    v_ref[...] = v

    def gate(i):
        return _sigmoid(_dot(h, wr_ref[:, _R_GZ + i * D_MODEL:_R_GZ + (i + 1) * D_MODEL]))

    g0_ref[...] = gate(0)

    zb = _dot(h, wr_ref[:, _R_BX:_R_CA])
    u = zb[:, 512:768] * zb[:, 0:256]
    conv_b = (bcw_ref[0:1, :] * sb_ref[:, 0:B_W] + bcw_ref[1:2, :] * sb_ref[:, B_W:2 * B_W]
              + bcw_ref[2:3, :] * u + bcb_ref[...])
    part = gate(1) * _dot((zb[:, 256:512] * conv_b).astype(BF16), wb_ref[...])
    cb_ref[:, 0:B_W] = sb_ref[:, B_W:2 * B_W]
    cb_ref[:, B_W:2 * B_W] = u

    zc = _dot(h, wr_ref[:, _R_CA:_R_DU])
    uc = zc[:, 0:256] * _sigmoid(zc[:, 256:512])
    acc = ccw_ref[C_K - 1:C_K, :] * uc + ccb_ref[...]
    for kk in range(C_K - 1):
        acc = acc + ccw_ref[kk:kk + 1, :] * sc_ref[:, kk * C_W:(kk + 1) * C_W]
    y = _layer_norm(acc, clg_ref[...], clb_ref[...])
    yc = _dot((y * _sigmoid(y)).astype(BF16), wc_ref[...])
    part = part + gate(2) * yc
    cc_ref[:, 0:(C_K - 2) * C_W] = sc_ref[:, C_W:(C_K - 1) * C_W]
    cc_ref[:, (C_K - 2) * C_W:(C_K - 1) * C_W] = uc

    zd = jax.nn.gelu(_dot(h, wr_ref[:, _R_DU:_R_GZ]))
    dvn = _layer_norm(zd[:, 256:512], dlg_ref[...], dlb_ref[...])
    dvn_ref[...] = dvn
    w00 = sum(dws_ref[g, 0:1, 0:1] * _lane_group_mask(D_W, D_W // D_GROUPS, g) for g in range(D_GROUPS))
    b0 = sum(dbs_ref[g:g + 1, 0:1] * _lane_group_mask(D_W, D_W // D_GROUPS, g) for g in range(D_GROUPS))
    yd = _dot((zd[:, 0:256] * (w00 * dvn + b0)).astype(BF16), wd_ref[...])
    part_ref[...] = part + gate(3) * yd


def _sample_mixer(x, sb, sc, p):
    n = x.shape[0]
    shapes = [(n, A_W), (n, A_W), (n, A_W), (n, (B_K - 1) * B_W), (n, (C_K - 1) * C_W), (n, D_W),
              (n, D_MODEL), (n, D_MODEL)]
    ins = [x, sb, sc, p["norm_mix_g"], p["w_in_a"], p["w_in_r"], p["a_qnorm_g"], p["a_knorm_g"],
           p["w_b_out"], p["w_c_out"], p["w_d_out"], p["b_conv_w"], p["b_conv_b"], p["c_conv_w"],
           p["c_conv_b"], p["c_ln_g"], p["c_ln_b"], p["d_ln_g"], p["d_ln_b"], p["d_ws"], p["d_bs"]]
    return pl.pallas_call(
        _sample_mixer_kernel,
        grid=(1,),
        in_specs=[_full(a.shape) for a in ins],
        out_specs=[_full(s) for s in shapes],
        out_shape=[jax.ShapeDtypeStruct(s, F32) for s in shapes],
        compiler_params=_params(("arbitrary",)),
        name="sample_mixer",
    )(*ins)


_DEC_ROWS = 2 * A_HEADS
_PAGE_ROWS = PAGE_SIZE * A_HEADS


def _head_rows(ref, row, per_head):
    return sum(jnp.where(_idiv(row, per_head) == h, jnp.broadcast_to(ref[h:h + 1, :], row.shape), 0.0)
               for h in range(ref.shape[0]))


def _decode_attn_kernel(pt_ref, q_ref, kn_ref, vn_ref, lam_ref, sg_ref, *rest, lam_init, past_len):
    del pt_ref
    k_refs = rest[:_DEC_PAGES]
    v_refs = rest[_DEC_PAGES:2 * _DEC_PAGES]
    o_ref, qz_scr, m_scr, l_scr, acc_scr = rest[2 * _DEC_PAGES:]
    j = pl.program_id(1)
    nj = pl.num_programs(1)
    width = _DEC_PAGES * _PAGE_ROWS

    row = lax.broadcasted_iota(jnp.int32, (_DEC_ROWS, A_VD), 0)
    lane = lax.broadcasted_iota(jnp.int32, (_DEC_ROWS, A_VD), 1)
    rowv = lax.broadcasted_iota(jnp.int32, (_DEC_ROWS, 1), 0)
    slope = sum(jnp.where(_idiv(rowv, 2) == i, 2.0 ** (-8.0 * (i + 1) / A_HEADS), 0.0)
                for i in range(A_HEADS))

    @pl.when(j == 0)
    def _():
        qz_scr[...] = jnp.where(_idiv(lane, A_HD) == (row & 1), _head_rows(q_ref, row, 2), 0.0)
        m_scr[...] = jnp.full(m_scr.shape, NEG_INF, F32)
        l_scr[...] = jnp.zeros(l_scr.shape, F32)
        acc_scr[...] = jnp.zeros(acc_scr.shape, F32)

    qz = qz_scr[...].astype(BF16)
    s = jnp.concatenate([_dot_nt(qz, k_refs[i][...].astype(BF16)) for i in range(_DEC_PAGES)], axis=1)
    col = lax.broadcasted_iota(jnp.int32, (_DEC_ROWS, width), 1)
    kpos = j * (_DEC_PAGES * PAGE_SIZE) + _idiv(lax.broadcasted_iota(jnp.int32, (1, width), 1), A_HEADS)
    own_head = (col & (A_HEADS - 1)) == _idiv(lax.broadcasted_iota(jnp.int32, (_DEC_ROWS, width), 0), 2)
    s = jnp.where(own_head, s - slope * (past_len - kpos).astype(F32), NEG_INF)
    m_old = m_scr[...]
    m_new = jnp.maximum(m_old, jnp.max(s, axis=-1, keepdims=True))
    alpha = jnp.exp(m_old - m_new)
    p = jnp.exp(s - m_new)
    l_scr[...] = alpha * l_scr[...] + jnp.sum(p, axis=-1, keepdims=True)
    p = p.astype(BF16)
    pv = sum(_dot(p[:, i * _PAGE_ROWS:(i + 1) * _PAGE_ROWS], v_refs[i][...].astype(BF16))
             for i in range(_DEC_PAGES))
    acc_scr[...] = alpha * acc_scr[...] + pv
    m_scr[...] = m_new

    @pl.when(j == nj - 1)
    def _():
        s_new = jnp.sum(qz_scr[...] * _head_rows(kn_ref, row, 2), axis=-1, keepdims=True)
        m_old = m_scr[...]
        m_fin = jnp.maximum(m_old, s_new)
        alpha = jnp.exp(m_old - m_fin)
        p_new = jnp.exp(s_new - m_fin)
        l_fin = alpha * l_scr[...] + p_new
        o_all = (alpha * acc_scr[...] + p_new * _head_rows(vn_ref, row, 2)) / l_fin
        lam = _lam(lam_ref, lam_init)
        sg = sg_ref[...]
        for h in range(A_HEADS):
            o = o_all[2 * h:2 * h + 1] - lam * o_all[2 * h + 1:2 * h + 2]
            o_ref[h:h + 1, :] = _rms(o, sg) * (1.0 - lam_init)


def _decode_attention(page_table, q, k_new, v_new, cache_k, cache_v, layer, a_lam, sg, lam_init):
    n, n_pages = page_table.shape
    past_len = n_pages * PAGE_SIZE
    row = pl.BlockSpec((None, A_HEADS, A_VD), lambda b, j, pt: (b, 0, 0))

    def page(i):
        return pl.BlockSpec((None, None, _PAGE_ROWS, A_VD),
                            lambda b, j, pt: (layer, pt[b, j * _DEC_PAGES + i], 0, 0))

    grid_spec = pltpu.PrefetchScalarGridSpec(
        num_scalar_prefetch=1,
        grid=(n, n_pages // _DEC_PAGES),
        in_specs=[row, row, row, pl.BlockSpec((4, A_HD), lambda b, j, pt: (0, 0)),
                  pl.BlockSpec((1, A_VD), lambda b, j, pt: (0, 0))]
        + [page(i) for i in range(_DEC_PAGES)] + [page(i) for i in range(_DEC_PAGES)],
        out_specs=row,
        scratch_shapes=[pltpu.VMEM((_DEC_ROWS, A_VD), F32), pltpu.VMEM((_DEC_ROWS, 1), F32),
                        pltpu.VMEM((_DEC_ROWS, 1), F32), pltpu.VMEM((_DEC_ROWS, A_VD), F32)],
    )
    heads = lambda a: a.reshape(n, A_HEADS, A_VD)
    return pl.pallas_call(
        functools.partial(_decode_attn_kernel, lam_init=lam_init, past_len=past_len),
        grid_spec=grid_spec,
        out_shape=jax.ShapeDtypeStruct((n, A_HEADS, A_VD), F32),
        compiler_params=_params(("arbitrary", "arbitrary")),
        name="decode_attention",
    )(page_table, heads(q), heads(k_new), heads(v_new), a_lam, sg,
      *([cache_k] * _DEC_PAGES), *([cache_v] * _DEC_PAGES))


def _sample_tail_kernel(x_ref, o_ref, g0_ref, part_ref, mk_ref, mv_ref, wa_ref, wo_ref, gx_ref, wxq_ref,
                        qg_ref, wxo_ref, gf_ref, wup_ref, wdn_ref, out_ref, x1_scr, q_scr, om_scr):
    b = pl.program_id(0)
    nb = pl.num_programs(0)

    @pl.when(b == 0)
    def _():
        merged = g0_ref[...] * _dot(o_ref[...].astype(BF16), wa_ref[...]) + part_ref[...]
        x1 = x_ref[...] + _dot(merged.astype(BF16), wo_ref[...])
        x1_scr[...] = x1
        q = _dot(_rms(x1, gx_ref[...]).astype(BF16), wxq_ref[...])
        qg = qg_ref[...]
        for h in range(X_HEADS):
            q_scr[h] = _rms(q[:, h * X_HD:(h + 1) * X_HD], qg) * (X_HD ** -0.5)

    n_rows = mk_ref.shape[0]
    row = lax.broadcasted_iota(jnp.int32, (8, X_HD), 0)
    qz = sum(jnp.where(row == h, jnp.broadcast_to(q_scr[h, pl.ds(b, 1), :], (8, X_HD)), 0.0)
             for h in range(X_HEADS)).astype(BF16)
    s = _dot_nt(qz, mk_ref[...].astype(BF16))
    col = lax.broadcasted_iota(jnp.int32, (8, n_rows), 1)
    s = jnp.where((col & (X_HEADS - 1)) == lax.broadcasted_iota(jnp.int32, (8, n_rows), 0), s, NEG_INF)
    p = jnp.exp(s - jnp.max(s, axis=-1, keepdims=True))
    p = p / jnp.sum(p, axis=-1, keepdims=True)
    om = _dot(p.astype(BF16), mv_ref[...].astype(BF16))
    for h in range(X_HEADS):
        om_scr[h, pl.ds(b, 1), :] = om[h:h + 1, :]

    @pl.when(b == nb - 1)
    def _():
        x2 = x1_scr[...] + sum(_dot(om_scr[h].astype(BF16), wxo_ref[h * X_HD:(h + 1) * X_HD, :])
                               for h in range(X_HEADS))
        out_ref[...] = _ffn(x2, gf_ref[...], wup_ref, wdn_ref)


def _sample_tail(x, o, g0, part, mem_k, mem_v, layer, p):
    n = x.shape[0]
    n_rows = mem_k.shape[2]
    mem = pl.BlockSpec((None, None, n_rows, X_HD), lambda b: (layer, b, 0, 0))
    ins = [x, o, g0, part]
    ws = [p["w_a_out"], p["w_o"], p["norm_x_g"], p["w_xq"], p["x_qnorm_g"], p["w_xo"], p["norm_ffn_g"],
          p["w_up"], p["w_down"]]
    return pl.pallas_call(
        _sample_tail_kernel,
        grid=(n,),
        in_specs=[_full(a.shape) for a in ins] + [mem, mem] + [_full(a.shape) for a in ws],
        out_specs=_full((n, D_MODEL)),
        out_shape=jax.ShapeDtypeStruct((n, D_MODEL), F32),
        scratch_shapes=[pltpu.VMEM((n, D_MODEL), F32), pltpu.VMEM((X_HEADS, n, X_HD), F32),
                        pltpu.VMEM((X_HEADS, n, X_HD), F32)],
        compiler_params=_params(("arbitrary",)),
        name="sample_tail",
    )(*ins, mem_k, mem_v, *ws)


def _layer_params(l, a):
    row = lambda v: v[l].reshape(1, -1)
    b16 = lambda w: w.astype(BF16)
    w_in = a["w_in"][l]
    return dict(
        norm_mix_g=row(a["norm_mix_g"]),
        w_in_a=b16(w_in[:, :3 * A_W]), w_in_r=b16(w_in[:, 3 * A_W:]),
        a_qnorm_g=jnp.tile(a["a_qnorm_g"][l], A_W // A_HD).reshape(1, A_W),
        a_knorm_g=jnp.tile(a["a_knorm_g"][l], A_W // A_HD).reshape(1, A_W),
        a_lam=a["a_lam"][l], a_subln_g=row(a["a_subln_g"]),
        w_a_out=b16(a["w_a_out"][l]),
        b_conv_w=a["b_conv_w"][l], b_conv_b=row(a["b_conv_b"]), w_b_out=b16(a["w_b_out"][l]),
        c_conv_w=a["c_conv_w"][l], c_conv_b=row(a["c_conv_b"]),
        c_ln_g=row(a["c_ln_g"]), c_ln_b=row(a["c_ln_b"]), w_c_out=b16(a["w_c_out"][l]),
        d_ln_g=row(a["d_ln_g"]), d_ln_b=row(a["d_ln_b"]),
        d_ws=a["d_ws"][l], d_bs=a["d_bs"][l], d_bs_t=a["d_bs"][l].T, w_d_out=b16(a["w_d_out"][l]),
        w_o=b16(a["w_o"][l]),
        norm_x_g=row(a["norm_x_g"]), mem_norm_g=row(a["mem_norm_g"]),
        w_xq=b16(a["w_xq"][l]), w_xk=b16(a["w_xk"][l]), w_xv=b16(a["w_xv"][l]),
        x_qnorm_g=row(a["x_qnorm_g"]), x_knorm_g=row(a["x_knorm_g"]),
        w_xo=b16(a["w_xo"][l]), norm_ffn_g=row(a["norm_ffn_g"]),
        w_up=b16(a["w_up"][l]), w_down=b16(a["w_down"][l]),
    )


def kernel(x_prompt, x_sample, cache_k_a, cache_v_a, state_conv_b, state_conv_c, cache_mem_k, cache_mem_v,
           page_table, mem_prompt, norm_mix_g, w_in, a_qnorm_g, a_knorm_g, a_lam, a_subln_g, w_a_out,
           b_conv_w, b_conv_b, w_b_out, c_conv_w, c_conv_b, c_ln_g, c_ln_b, w_c_out, d_ln_g, d_ln_b,
           d_ws, d_bs, w_d_out, w_o, norm_x_g, mem_norm_g, w_xq, w_xk, w_xv, x_qnorm_g, x_knorm_g,
           w_xo, norm_ffn_g, w_up, w_down):
    weights = dict(norm_mix_g=norm_mix_g, w_in=w_in, a_qnorm_g=a_qnorm_g, a_knorm_g=a_knorm_g, a_lam=a_lam,
                   a_subln_g=a_subln_g, w_a_out=w_a_out, b_conv_w=b_conv_w, b_conv_b=b_conv_b,
                   w_b_out=w_b_out, c_conv_w=c_conv_w, c_conv_b=c_conv_b, c_ln_g=c_ln_g, c_ln_b=c_ln_b,
                   w_c_out=w_c_out, d_ln_g=d_ln_g, d_ln_b=d_ln_b, d_ws=d_ws, d_bs=d_bs, w_d_out=w_d_out,
                   w_o=w_o, norm_x_g=norm_x_g, mem_norm_g=mem_norm_g, w_xq=w_xq, w_xk=w_xk, w_xv=w_xv,
                   x_qnorm_g=x_qnorm_g, x_knorm_g=x_knorm_g, w_xo=w_xo, norm_ffn_g=norm_ffn_g,
                   w_up=w_up, w_down=w_down)
    depth = w_in.shape[0]
    bp, t, _ = x_prompt.shape
    ns = x_sample.shape[0]
    n_pool = cache_k_a.shape[1]
    n_mem = cache_mem_k.shape[2]
    cache_k = cache_k_a.reshape(depth, n_pool, _PAGE_ROWS, A_VD)
    cache_v = cache_v_a.reshape(depth, n_pool, _PAGE_ROWS, A_VD)
    mem_k_s = cache_mem_k.reshape(depth, ns, n_mem * X_HEADS, X_HD)
    mem_v_s = cache_mem_v.reshape(depth, ns, n_mem * X_HEADS, X_HD)
    slopes = jnp.asarray([2.0 ** (-8.0 * (i + 1) / A_HEADS) for i in range(A_HEADS)], F32)

    xp = x_prompt
    xs = x_sample.reshape(ns, D_MODEL)
    outs = [[] for _ in range(11)]
    for l in range(depth):
        p = _layer_params(l, weights)
        lam_init = 0.8 - 0.6 * math.exp(-0.3 * l)

        mk, mv, mkb, mvb = _memory_kv(mem_prompt, p["mem_norm_g"], p["w_xk"], p["w_xv"], p["x_knorm_g"])
        k_p, v_p, qb, kb, vb = _qkv_proj(xp, p["norm_mix_g"], p["w_in_a"], p["a_qnorm_g"], p["a_knorm_g"])
        o_p = _diff_attention(slopes, qb, kb, vb, p["a_lam"], p["a_subln_g"], lam_init)
        xp, cb_p, cc_p = _mixer(xp, o_p, p)
        xp = _xattn_ffn(xp, mkb, mvb, p)

        sb = state_conv_b[l].reshape(ns, (B_K - 1) * B_W)
        sc = state_conv_c[l].reshape(ns, (C_K - 1) * C_W)
        q_s, k_s, v_s, cb_s, cc_s, dvn_s, part, g0 = _sample_mixer(xs, sb, sc, p)
        o_s = _decode_attention(page_table, q_s, k_s, v_s, cache_k, cache_v, l, p["a_lam"],
                                p["a_subln_g"], lam_init)
        xs = _sample_tail(xs, o_s.reshape(ns, A_W), g0, part, mem_k_s, mem_v_s, l, p)

        for lst, val in zip(outs, (
                k_p, v_p, cb_p, cc_p, mk, mv,
                k_s.reshape(ns, 1, A_HEADS, 2 * A_HD), v_s.reshape(ns, 1, A_HEADS, A_VD),
                cb_s.reshape(ns, B_K - 1, B_W), cc_s.reshape(ns, C_K - 1, C_W),
                dvn_s.reshape(ns, 1, D_W))):
            lst.append(val)
    return (xp, xs.reshape(ns, 1, D_MODEL)) + tuple(jnp.stack(o) for o in outs)
```

```python
import functools
import math

import jax
import jax.numpy as jnp
from jax import lax
from jax.experimental import pallas as pl
from jax.experimental.pallas import tpu as pltpu

F32 = jnp.float32
BF16 = jnp.bfloat16

D_MODEL = 1024
A_HEADS = 4
A_HD = 64
A_VD = 2 * A_HD
A_W = A_HEADS * 2 * A_HD
B_W = 256
B_K = 3
C_W = 256
C_K = 31
D_W = 256
D_GROUPS = 4
CHUNK = 128
N_BRANCH = 4
X_HEADS = 4
X_HD = 128
X_W = X_HEADS * X_HD
D_FF = 4 * D_MODEL
PAGE_SIZE = 128
EPS = 1e-6
NEG_INF = -1e30
LOG2E = math.log2(math.e)

_R_BX, _R_BB, _R_BC = 0, 256, 512
_R_CA, _R_CG = 768, 1024
_R_DU, _R_DV = 1280, 1536
_R_GZ = 1792
_R_COLS = _R_GZ + N_BRANCH * D_MODEL

_V7X_VMEM_BYTES = 64 * 1024 * 1024
_VMEM_LIMIT = _V7X_VMEM_BYTES - 8 * 1024 * 1024

_TM = 512
_TQ = 512
_TK = 512
_SUBLANES = 8
_CONV_ROWS = 64
_B_HALO = 8
_C_HALO = 32
_FF_CHUNK = 1024
_DEC_PAGES = 8
_FF_SLICE = 512


def _rms(x, g):
    return x * lax.rsqrt(jnp.mean(x * x, axis=-1, keepdims=True) + EPS) * g


def _layer_norm(x, g, b):
    xc = x - jnp.mean(x, axis=-1, keepdims=True)
    var = jnp.mean(xc * xc, axis=-1, keepdims=True)
    return xc * lax.rsqrt(var + EPS) * g + b


def _sigmoid(x):
    return 0.5 * jnp.tanh(0.5 * x) + 0.5


def _dot(a, b):
    return jnp.dot(a, b, preferred_element_type=F32)


def _dot_nt(a, b):
    return lax.dot_general(a, b, (((1,), (1,)), ((), ())), preferred_element_type=F32)


def _idiv(x, d):
    assert d & (d - 1) == 0
    return lax.shift_right_logical(x, int(math.log2(d)))


def _group_mean_matrix(width, group):
    r = _idiv(lax.broadcasted_iota(jnp.int32, (width, width), 0), group)
    c = _idiv(lax.broadcasted_iota(jnp.int32, (width, width), 1), group)
    return jnp.where(r == c, 1.0 / group, 0.0).astype(BF16)


def _group_rms(t, g, gm):
    ms = _dot((t * t).astype(BF16), gm)
    return t * lax.rsqrt(ms + EPS) * g


def _lam(lam_ref, lam_init):
    a = lam_ref[...]
    s1 = jnp.sum(a[0:1] * a[1:2], axis=-1, keepdims=True)
    s2 = jnp.sum(a[2:3] * a[3:4], axis=-1, keepdims=True)
    return jnp.exp(s1) - jnp.exp(s2) + lam_init


def _full(shape):
    return pl.BlockSpec(shape, lambda *_: (0,) * len(shape))


def _params(sem):
    return pltpu.CompilerParams(dimension_semantics=sem, vmem_limit_bytes=_VMEM_LIMIT)


def _memkv_kernel(mem_ref, g_ref, wk_ref, wv_ref, kg_ref, mk_ref, mv_ref, mkb_ref, mvb_ref):
    h = _rms(mem_ref[...], g_ref[...]).astype(BF16)
    k = _dot(h, wk_ref[...])
    v = _dot(h, wv_ref[...])
    kg = kg_ref[...]
    k = jnp.concatenate([_rms(k[:, i * X_HD:(i + 1) * X_HD], kg) for i in range(X_HEADS)], axis=1)
    mk_ref[...] = k.reshape(mk_ref.shape)
    mv_ref[...] = v.reshape(mv_ref.shape)
    mkb_ref[...] = k.astype(BF16)
    mvb_ref[...] = v.astype(BF16)


def _memory_kv(mem, g, wk, wv, kg):
    b, n, _ = mem.shape
    blk = pl.BlockSpec((None, n, X_W), lambda i: (i, 0, 0))
    hblk = pl.BlockSpec((None, n, X_HEADS, X_HD), lambda i: (i, 0, 0, 0))
    return pl.pallas_call(
        _memkv_kernel,
        grid=(b,),
        in_specs=[pl.BlockSpec((None, n, D_MODEL), lambda i: (i, 0, 0)), _full((1, D_MODEL)),
                  _full((D_MODEL, X_W)), _full((D_MODEL, X_W)), _full((1, X_HD))],
        out_specs=[hblk, hblk, blk, blk],
        out_shape=[jax.ShapeDtypeStruct((b, n, X_HEADS, X_HD), F32),
                   jax.ShapeDtypeStruct((b, n, X_HEADS, X_HD), F32),
                   jax.ShapeDtypeStruct((b, n, X_W), BF16), jax.ShapeDtypeStruct((b, n, X_W), BF16)],
        compiler_params=_params(("arbitrary",)),
        name="memory_kv",
    )(mem, g, wk, wv, kg)


def _qkv_from_h(h, w_ref, qg, kg):
    z = _dot(h, w_ref[...])
    gm = _group_mean_matrix(A_W, A_HD)
    q = _group_rms(z[:, :A_W], qg, gm) * (A_HD ** -0.5)
    k = _group_rms(z[:, A_W:2 * A_W], kg, gm)
    v = z[:, 2 * A_W:]
    return q, k, v


def _qkv_kernel(x_ref, g_ref, w_ref, qg_ref, kg_ref, k_ref, v_ref, qb_ref, kb_ref, vb_ref):
    h = _rms(x_ref[...], g_ref[...]).astype(BF16)
    q, k, v = _qkv_from_h(h, w_ref, qg_ref[...], kg_ref[...])
    k_ref[...] = k.reshape(k_ref.shape)
    v_ref[...] = v.reshape(v_ref.shape)
    qb_ref[...] = (q * LOG2E).astype(BF16)
    kb_ref[...] = k.astype(BF16)
    vb_ref[...] = v.astype(BF16)


def _qkv_proj(x, g, w_a, qg, kg):
    b, t, _ = x.shape
    blk = pl.BlockSpec((None, _TM, A_W), lambda i, j: (i, j, 0))
    hblk = pl.BlockSpec((None, _TM, A_HEADS, A_VD), lambda i, j: (i, j, 0, 0))
    f32 = jax.ShapeDtypeStruct((b, t, A_HEADS, A_VD), F32)
    b16 = jax.ShapeDtypeStruct((b, t, A_W), BF16)
    return pl.pallas_call(
        _qkv_kernel,
        grid=(b, t // _TM),
        in_specs=[pl.BlockSpec((None, _TM, D_MODEL), lambda i, j: (i, j, 0)), _full((1, D_MODEL)),
                  _full((D_MODEL, 3 * A_W)), _full((1, A_W)), _full((1, A_W))],
        out_specs=[hblk, hblk, blk, blk, blk],
        out_shape=[f32, f32, b16, b16, b16],
        compiler_params=_params(("arbitrary", "arbitrary")),
        name="qkv_proj",
    )(x, g, w_a, qg, kg)


def _diff_attn_kernel(slope_ref, q_ref, k_ref, v_ref, lam_ref, sg_ref, o_ref,
                      qs_scr, m_scr, l_scr, acc_scr, *, lam_init):
    h = pl.program_id(1)
    q0 = pl.program_id(2) * _TQ
    rows = 2 * _TQ

    q = q_ref[...]
    lane = lax.broadcasted_iota(jnp.int32, q.shape, 1)
    zero = jnp.zeros_like(q)
    qs_scr[0:_TQ, :] = jnp.where(lane < A_HD, q, zero)
    qs_scr[_TQ:rows, :] = jnp.where(lane >= A_HD, q, zero)
    m_scr[...] = jnp.full(m_scr.shape, NEG_INF, F32)
    l_scr[...] = jnp.zeros(l_scr.shape, F32)
    acc_scr[...] = jnp.zeros(acc_scr.shape, F32)
    slope = slope_ref[h]

    def chunk(j, masked):
        k0 = pl.multiple_of(j * _TK, _TK)
        s = _dot_nt(qs_scr[...], k_ref[pl.ds(k0, _TK), :])
        kpos = k0 + lax.broadcasted_iota(jnp.int32, (1, _TK), 1)
        s = s + (slope * LOG2E) * kpos.astype(F32)
        if masked:
            row = lax.broadcasted_iota(jnp.int32, (rows, _TK), 0)
            qpos = q0 + jnp.where(row >= _TQ, row - _TQ, row)
            col = k0 + lax.broadcasted_iota(jnp.int32, (rows, _TK), 1)
            s = jnp.where(qpos >= col, s, NEG_INF)
        m_old = m_scr[...]
        m_new = jnp.maximum(m_old, jnp.max(s, axis=-1, keepdims=True))
        alpha = jnp.exp2(m_old - m_new)
        p = jnp.exp2(s - jnp.concatenate([m_new] * (_TK // A_VD), axis=1))
        l_scr[...] = alpha * l_scr[...] + jnp.sum(p, axis=-1, keepdims=True)
        acc_scr[...] = alpha * acc_scr[...] + _dot(p.astype(BF16), v_ref[pl.ds(k0, _TK), :])
        m_scr[...] = m_new

    n_full = q0 // _TK

    def body(j, carry):
        chunk(j, False)
        return carry

    lax.fori_loop(0, n_full, body, 0)
    for d in range(_TQ // _TK):
        chunk(n_full + d, True)

    o = acc_scr[...] / l_scr[...]
    o = o[0:_TQ] - _lam(lam_ref, lam_init) * o[_TQ:rows]
    o_ref[...] = (_rms(o, sg_ref[...]) * (1.0 - lam_init)).astype(BF16)


def _diff_attention(slopes, qb, kb, vb, a_lam, sg, lam_init):
    b, t, _ = qb.shape
    assert _TQ % _TK == 0 and t % _TQ == 0
    qblk = pl.BlockSpec((None, _TQ, A_VD), lambda bi, h, qi: (bi, qi, h))
    kvblk = pl.BlockSpec((None, t, A_VD), lambda bi, h, qi: (bi, 0, h))
    return pl.pallas_call(
        functools.partial(_diff_attn_kernel, lam_init=lam_init),
        grid=(b, A_HEADS, t // _TQ),
        in_specs=[pl.BlockSpec(memory_space=pltpu.SMEM), qblk, kvblk, kvblk,
                  _full((4, A_HD)), _full((1, A_VD))],
        out_specs=qblk,
        out_shape=jax.ShapeDtypeStruct((b, t, A_W), BF16),
        scratch_shapes=[pltpu.VMEM((2 * _TQ, A_VD), BF16), pltpu.VMEM((2 * _TQ, A_VD), F32),
                        pltpu.VMEM((2 * _TQ, A_VD), F32), pltpu.VMEM((2 * _TQ, A_VD), F32)],
        compiler_params=_params(("arbitrary",) * 3),
        name="diff_attention",
    )(slopes, qb, kb, vb, a_lam, sg)


def _lane_group_mask(width, group, g):
    lane = lax.broadcasted_iota(jnp.int32, (1, width), 1)
    return (_idiv(lane, group) == g).astype(F32)


def _mixer_kernel(x_ref, o_ref, g_ref, wr_ref, wa_ref, wb_ref, wc_ref, wd_ref, wo_ref,
                  bcw_ref, bcb_ref, ccw_ref, ccb_ref, clg_ref, clb_ref, dlg_ref, dlb_ref,
                  dws_ref, dbst_ref, x1_ref, cbp_ref, ccp_ref, ubuf, cbuf, ycin, shwin, gates):
    tm = _TM

    @pl.when(pl.program_id(1) == 0)
    def _():
        ubuf[0:_B_HALO, :] = jnp.zeros((_B_HALO, B_W), F32)
        cbuf[0:_C_HALO, :] = jnp.zeros((_C_HALO, C_W), F32)

    x = x_ref[...]
    h = _rms(x, g_ref[...]).astype(BF16)

    zc = _dot(h, wr_ref[:, _R_CA:_R_DU])
    cbuf[_C_HALO:_C_HALO + tm, :] = zc[:, 0:256] * _sigmoid(zc[:, 256:512])
    base = _C_HALO - (C_K - 1)
    gcols = N_BRANCH * D_MODEL // (tm // _CONV_ROWS)
    for it, r in enumerate(range(0, tm, _CONV_ROWS)):
        c0 = _R_GZ + it * gcols
        gates[:, it * gcols:(it + 1) * gcols] = _sigmoid(_dot(h, wr_ref[:, c0:c0 + gcols]))
        acc = jnp.broadcast_to(ccb_ref[...], (_CONV_ROWS, C_W))
        for ph in range(_SUBLANES):
            n_taps = (C_K - 1 - ph) // _SUBLANES + 1
            rows = _CONV_ROWS + _SUBLANES * (n_taps - 1)
            shwin[ph, 0:rows, :] = cbuf[base + r + ph:base + r + ph + rows, :]
            for i in range(n_taps):
                k = ph + _SUBLANES * i
                acc = acc + ccw_ref[k:k + 1, :] * shwin[ph, _SUBLANES * i:_SUBLANES * i + _CONV_ROWS, :]
        y = _layer_norm(acc, clg_ref[...], clb_ref[...])
        ycin[r:r + _CONV_ROWS, :] = (y * _sigmoid(y)).astype(BF16)
    ccp_ref[...] = cbuf[_C_HALO + tm - (C_K - 1):_C_HALO + tm, :]
    cbuf[0:_C_HALO, :] = cbuf[tm:tm + _C_HALO, :]

    def gate(i):
        return gates[:, i * D_MODEL:(i + 1) * D_MODEL]

    merged = gate(2) * _dot(ycin[...], wc_ref[...])

    merged = merged + gate(0) * _dot(o_ref[...], wa_ref[...])

    zb = _dot(h, wr_ref[:, _R_BX:_R_CA])
    u = zb[:, 512:768] * zb[:, 0:256]
    ubuf[_B_HALO:_B_HALO + tm, :] = u
    conv_b = (bcw_ref[0:1, :] * ubuf[_B_HALO - 2:_B_HALO - 2 + tm, :]
              + bcw_ref[1:2, :] * ubuf[_B_HALO - 1:_B_HALO - 1 + tm, :]
              + bcw_ref[2:3, :] * u + bcb_ref[...])
    yb = _dot((zb[:, 256:512] * conv_b).astype(BF16), wb_ref[...])
    merged = merged + gate(1) * yb
    cbp_ref[...] = ubuf[_B_HALO + tm - (B_K - 1):_B_HALO + tm, :]
    ubuf[0:_B_HALO, :] = ubuf[tm:tm + _B_HALO, :]

    zd = jax.nn.gelu(_dot(h, wr_ref[:, _R_DU:_R_GZ]))
    du = zd[:, 0:256]
    dvn = _layer_norm(zd[:, 256:512], dlg_ref[...], dlb_ref[...])
    row = lax.broadcasted_iota(jnp.int32, (CHUNK, CHUNK), 0)
    col = lax.broadcasted_iota(jnp.int32, (CHUNK, CHUNK), 1)
    tril = (row >= col).astype(F32)
    wcat = jnp.concatenate([dws_ref[g] * tril for g in range(D_GROUPS)], axis=1).astype(BF16)
    masks = [_lane_group_mask(D_W, D_W // D_GROUPS, g) for g in range(D_GROUPS)]
    bsmat = sum(dbst_ref[:, g:g + 1] * masks[g] for g in range(D_GROUPS))
    for c in range(0, tm, CHUNK):
        vch = dvn[c:c + CHUNK]
        rhs = jnp.concatenate([(vch * masks[g]).astype(BF16) for g in range(D_GROUPS)], axis=0)
        s = _dot(wcat, rhs) + bsmat
        ycin[c:c + CHUNK, :] = (du[c:c + CHUNK] * s).astype(BF16)
    yd = _dot(ycin[...], wd_ref[...])
    merged = merged + gate(3) * yd

    x1_ref[...] = x + _dot(merged.astype(BF16), wo_ref[...])


def _mixer(x, o, p):
    b, t, _ = x.shape
    row = lambda w: pl.BlockSpec((None, _TM, w), lambda i, j: (i, j, 0))
    return pl.pallas_call(
        _mixer_kernel,
        grid=(b, t // _TM),
        in_specs=[row(D_MODEL), row(A_W), _full((1, D_MODEL)), _full((D_MODEL, _R_COLS)),
                  _full((A_W, D_MODEL)), _full((B_W, D_MODEL)), _full((C_W, D_MODEL)),
                  _full((D_W, D_MODEL)), _full((D_MODEL, D_MODEL)),
                  _full((B_K, B_W)), _full((1, B_W)), _full((C_K, C_W)), _full((1, C_W)),
                  _full((1, C_W)), _full((1, C_W)), _full((1, D_W)), _full((1, D_W)),
                  _full((D_GROUPS, CHUNK, CHUNK)), _full((CHUNK, D_GROUPS))],
        out_specs=[row(D_MODEL),
                   pl.BlockSpec((None, B_K - 1, B_W), lambda i, j: (i, 0, 0)),
                   pl.BlockSpec((None, C_K - 1, C_W), lambda i, j: (i, 0, 0))],
        out_shape=[jax.ShapeDtypeStruct((b, t, D_MODEL), F32),
                   jax.ShapeDtypeStruct((b, B_K - 1, B_W), F32),
                   jax.ShapeDtypeStruct((b, C_K - 1, C_W), F32)],
        scratch_shapes=[pltpu.VMEM((_B_HALO + _TM, B_W), F32), pltpu.VMEM((_C_HALO + _TM, C_W), F32),
                        pltpu.VMEM((_TM, C_W), BF16),
                        pltpu.VMEM((_SUBLANES, _CONV_ROWS + _C_HALO, C_W), F32),
                        pltpu.VMEM((_TM, N_BRANCH * D_MODEL), F32)],
        compiler_params=_params(("arbitrary", "arbitrary")),
        name="mixer",
    )(x, o, p["norm_mix_g"], p["w_in_r"], p["w_a_out"], p["w_b_out"], p["w_c_out"], p["w_d_out"],
      p["w_o"], p["b_conv_w"], p["b_conv_b"], p["c_conv_w"], p["c_conv_b"], p["c_ln_g"], p["c_ln_b"],
      p["d_ln_g"], p["d_ln_b"], p["d_ws"], p["d_bs_t"])


def _ffn(x2, gf, wup_ref, wdn_ref):
    h3 = _rms(x2, gf).astype(BF16)
    acc = jnp.zeros(x2.shape, F32)
    for c in range(0, D_FF, _FF_CHUNK):
        a = jnp.maximum(_dot(h3, wup_ref[:, c:c + _FF_CHUNK]), 0.0)
        acc = acc + _dot((a * a).astype(BF16), wdn_ref[c:c + _FF_CHUNK, :])
    return x2 + acc


def _cross_attention(x, mk_ref, mv_ref, gx, wxq_ref, qg, wxo_ref):
    h2 = _rms(x, gx).astype(BF16)
    q = _dot(h2, wxq_ref[...])
    oms = []
    for i in range(X_HEADS):
        sl = slice(i * X_HD, (i + 1) * X_HD)
        qh = (_rms(q[:, sl], qg) * (X_HD ** -0.5)).astype(BF16)
        s = _dot_nt(qh, mk_ref[:, sl])
        p = jnp.exp(s - jnp.max(s, axis=-1, keepdims=True))
        oms.append(_dot(p.astype(BF16), mv_ref[:, sl]) / jnp.sum(p, axis=-1, keepdims=True))
    om = jnp.concatenate(oms, axis=1).astype(BF16)
    return x + _dot(om, wxo_ref[...])


def _sample_mixer_kernel(x_ref, sb_ref, sc_ref, g_ref, wa_ref, wr_ref, qg_ref, kg_ref,
                         wb_ref, wc_ref, wd_ref, bcw_ref, bcb_ref, ccw_ref, ccb_ref, clg_ref, clb_ref,
                         dlg_ref, dlb_ref, dws_ref, dbs_ref,
                         q_ref, k_ref, v_ref, cb_ref, cc_ref, dvn_ref, part_ref, g0_ref):
    x = x_ref[...]
    h = _rms(x, g_ref[...]).astype(BF16)
    q, k, v = _qkv_from_h(h, wa_ref, qg_ref[...], kg_ref[...])
    q_ref[...] = q
    k_ref[...] = k
    v_ref[...] = v

    def gate(i):
        return _sigmoid(_dot(h, wr_ref[:, _R_GZ + i * D_MODEL:_R_GZ + (i + 1) * D_MODEL]))

    g0_ref[...] = gate(0)

    zb = _dot(h, wr_ref[:, _R_BX:_R_CA])
    u = zb[:, 512:768] * zb[:, 0:256]
    conv_b = (bcw_ref[0:1, :] * sb_ref[:, 0:B_W] + bcw_ref[1:2, :] * sb_ref[:, B_W:2 * B_W]
              + bcw_ref[2:3, :] * u + bcb_ref[...])
    part = gate(1) * _dot((zb[:, 256:512] * conv_b).astype(BF16), wb_ref[...])
    cb_ref[:, 0:B_W] = sb_ref[:, B_W:2 * B_W]
    cb_ref[:, B_W:2 * B_W] = u

    zc = _dot(h, wr_ref[:, _R_CA:_R_DU])
    uc = zc[:, 0:256] * _sigmoid(zc[:, 256:512])
    acc = ccw_ref[C_K - 1:C_K, :] * uc + ccb_ref[...]
    for kk in range(C_K - 1):
        acc = acc + ccw_ref[kk:kk + 1, :] * sc_ref[:, kk * C_W:(kk + 1) * C_W]
    y = _layer_norm(acc, clg_ref[...], clb_ref[...])
    yc = _dot((y * _sigmoid(y)).astype(BF16), wc_ref[...])
    part = part + gate(2) * yc
    cc_ref[:, 0:(C_K - 2) * C_W] = sc_ref[:, C_W:(C_K - 1) * C_W]
    cc_ref[:, (C_K - 2) * C_W:(C_K - 1) * C_W] = uc

    zd = jax.nn.gelu(_dot(h, wr_ref[:, _R_DU:_R_GZ]))
    dvn = _layer_norm(zd[:, 256:512], dlg_ref[...], dlb_ref[...])
    dvn_ref[...] = dvn
    w00 = sum(dws_ref[g, 0:1, 0:1] * _lane_group_mask(D_W, D_W // D_GROUPS, g) for g in range(D_GROUPS))
    b0 = sum(dbs_ref[g:g + 1, 0:1] * _lane_group_mask(D_W, D_W // D_GROUPS, g) for g in range(D_GROUPS))
    yd = _dot((zd[:, 0:256] * (w00 * dvn + b0)).astype(BF16), wd_ref[...])
    part_ref[...] = part + gate(3) * yd


def _sample_mixer(x, sb, sc, p):
    n = x.shape[0]
    shapes = [(n, A_W), (n, A_W), (n, A_W), (n, (B_K - 1) * B_W), (n, (C_K - 1) * C_W), (n, D_W),
              (n, D_MODEL), (n, D_MODEL)]
    ins = [x, sb, sc, p["norm_mix_g"], p["w_in_a"], p["w_in_r"], p["a_qnorm_g"], p["a_knorm_g"],
           p["w_b_out"], p["w_c_out"], p["w_d_out"], p["b_conv_w"], p["b_conv_b"], p["c_conv_w"],
           p["c_conv_b"], p["c_ln_g"], p["c_ln_b"], p["d_ln_g"], p["d_ln_b"], p["d_ws"], p["d_bs"]]
    return pl.pallas_call(
        _sample_mixer_kernel,
        grid=(1,),
        in_specs=[_full(a.shape) for a in ins],
        out_specs=[_full(s) for s in shapes],
        out_shape=[jax.ShapeDtypeStruct(s, F32) for s in shapes],
        compiler_params=_params(("arbitrary",)),
        name="sample_mixer",
    )(*ins)


_DEC_ROWS = 2 * A_HEADS
_PAGE_ROWS = PAGE_SIZE * A_HEADS


def _head_rows(ref, row, per_head):
    return sum(jnp.where(_idiv(row, per_head) == h, jnp.broadcast_to(ref[h:h + 1, :], row.shape), 0.0)
               for h in range(ref.shape[0]))


def _decode_init(q_ref, qz_scr, m_scr, l_scr, acc_scr):
    row = lax.broadcasted_iota(jnp.int32, (_DEC_ROWS, A_VD), 0)
    lane = lax.broadcasted_iota(jnp.int32, (_DEC_ROWS, A_VD), 1)
    qz_scr[...] = jnp.where(_idiv(lane, A_HD) == (row & 1), _head_rows(q_ref, row, 2), 0.0)
    m_scr[...] = jnp.full(m_scr.shape, NEG_INF, F32)
    l_scr[...] = jnp.zeros(l_scr.shape, F32)
    acc_scr[...] = jnp.zeros(acc_scr.shape, F32)


def _decode_pages(j, k_refs, v_refs, qz_scr, m_scr, l_scr, acc_scr, past_len):
    n_pages = len(k_refs)
    width = n_pages * _PAGE_ROWS
    rowv = lax.broadcasted_iota(jnp.int32, (_DEC_ROWS, 1), 0)
    slope = sum(jnp.where(_idiv(rowv, 2) == i, 2.0 ** (-8.0 * (i + 1) / A_HEADS), 0.0)
                for i in range(A_HEADS))
    qz = qz_scr[...].astype(BF16)
    s = jnp.concatenate([_dot_nt(qz, k_refs[i][...].astype(BF16)) for i in range(n_pages)], axis=1)
    col = lax.broadcasted_iota(jnp.int32, (_DEC_ROWS, width), 1)
    kpos = j * (n_pages * PAGE_SIZE) + _idiv(lax.broadcasted_iota(jnp.int32, (1, width), 1), A_HEADS)
    own_head = (col & (A_HEADS - 1)) == _idiv(lax.broadcasted_iota(jnp.int32, (_DEC_ROWS, width), 0), 2)
    s = jnp.where(own_head, s - slope * (past_len - kpos).astype(F32), NEG_INF)
    m_old = m_scr[...]
    m_new = jnp.maximum(m_old, jnp.max(s, axis=-1, keepdims=True))
    alpha = jnp.exp(m_old - m_new)
    p = jnp.exp(s - m_new)
    l_scr[...] = alpha * l_scr[...] + jnp.sum(p, axis=-1, keepdims=True)
    p = p.astype(BF16)
    pv = sum(_dot(p[:, i * _PAGE_ROWS:(i + 1) * _PAGE_ROWS], v_refs[i][...].astype(BF16))
             for i in range(n_pages))
    acc_scr[...] = alpha * acc_scr[...] + pv
    m_scr[...] = m_new


def _decode_finish(kn_ref, vn_ref, lam_ref, sg_ref, o_ref, qz_scr, m_scr, l_scr, acc_scr, lam_init):
    row = lax.broadcasted_iota(jnp.int32, (_DEC_ROWS, A_VD), 0)
    s_new = jnp.sum(qz_scr[...] * _head_rows(kn_ref, row, 2), axis=-1, keepdims=True)
    m_old = m_scr[...]
    m_fin = jnp.maximum(m_old, s_new)
    alpha = jnp.exp(m_old - m_fin)
    p_new = jnp.exp(s_new - m_fin)
    l_fin = alpha * l_scr[...] + p_new
    o_all = (alpha * acc_scr[...] + p_new * _head_rows(vn_ref, row, 2)) / l_fin
    lam = _lam(lam_ref, lam_init)
    sg = sg_ref[...]
    for h in range(A_HEADS):
        o = o_all[2 * h:2 * h + 1] - lam * o_all[2 * h + 1:2 * h + 2]
        o_ref[h:h + 1, :] = _rms(o, sg) * (1.0 - lam_init)


def _xffn_decode_kernel(pt_ref, x_ref, mk_ref, mv_ref, gx_ref, wxq_ref, qg_ref, wxo_ref, gf_ref, wup_ref,
                        wdn_ref, q_ref, kn_ref, vn_ref, lam_ref, sg_ref, *rest, lam_init, past_len):
    del pt_ref
    k_refs = rest[:_DEC_PAGES]
    v_refs = rest[_DEC_PAGES:2 * _DEC_PAGES]
    out_ref, o_ref, x2_scr, h3_scr, ffn_scr, qz_scr, m_scr, l_scr, acc_scr = rest[2 * _DEC_PAGES:]
    c = pl.program_id(2)
    nc = pl.num_programs(2)
    dec = (qz_scr, m_scr, l_scr, acc_scr)

    @pl.when(c == 0)
    def _():
        x2 = _cross_attention(x_ref[...], mk_ref, mv_ref, gx_ref[...], wxq_ref, qg_ref[...], wxo_ref)
        x2_scr[...] = x2
        h3_scr[...] = _rms(x2, gf_ref[...]).astype(BF16)
        ffn_scr[...] = jnp.zeros(ffn_scr.shape, F32)
        _decode_init(q_ref, *dec)

    _decode_pages(c, k_refs, v_refs, *dec, past_len)
    cols = pl.ds(pl.multiple_of(c * _FF_SLICE, _FF_SLICE), _FF_SLICE)
    a = jnp.maximum(_dot(h3_scr[...], wup_ref[:, cols]), 0.0)
    ffn_scr[...] += _dot((a * a).astype(BF16), wdn_ref[cols, :])

    @pl.when(c == nc - 1)
    def _():
        out_ref[...] = x2_scr[...] + ffn_scr[...]
        _decode_finish(kn_ref, vn_ref, lam_ref, sg_ref, o_ref, *dec, lam_init)


def _xattn_ffn_decode(x, mkb, mvb, p, page_table, q, k_new, v_new, cache_k, cache_v, layer, lam_init):
    b, t, _ = x.shape
    n_mem = mkb.shape[1]
    n, n_pages = page_table.shape
    nj = t // _TM
    nc = n_pages // _DEC_PAGES
    assert n == b * nj and D_FF == nc * _FF_SLICE
    past_len = n_pages * PAGE_SIZE

    def const(shape):
        return pl.BlockSpec(shape, lambda i, j, c, pt: (0,) * len(shape))

    row = pl.BlockSpec((None, _TM, D_MODEL), lambda i, j, c, pt: (i, j, 0))
    mem = pl.BlockSpec((None, n_mem, X_W), lambda i, j, c, pt: (i, 0, 0))
    seq = pl.BlockSpec((None, A_HEADS, A_VD), lambda i, j, c, pt: (i * nj + j, 0, 0))

    def page(k):
        return pl.BlockSpec((None, None, _PAGE_ROWS, A_VD),
                            lambda i, j, c, pt: (layer, pt[i * nj + j, c * _DEC_PAGES + k], 0, 0))

    grid_spec = pltpu.PrefetchScalarGridSpec(
        num_scalar_prefetch=1,
        grid=(b, nj, nc),
        in_specs=[row, mem, mem, const((1, D_MODEL)), const((D_MODEL, X_W)), const((1, X_HD)),
                  const((X_W, D_MODEL)), const((1, D_MODEL)), const((D_MODEL, D_FF)),
                  const((D_FF, D_MODEL)), seq, seq, seq, const((4, A_HD)), const((1, A_VD))]
        + [page(k) for k in range(_DEC_PAGES)] + [page(k) for k in range(_DEC_PAGES)],
        out_specs=[row, seq],
        scratch_shapes=[pltpu.VMEM((_TM, D_MODEL), F32), pltpu.VMEM((_TM, D_MODEL), BF16),
                        pltpu.VMEM((_TM, D_MODEL), F32),
                        pltpu.VMEM((_DEC_ROWS, A_VD), F32), pltpu.VMEM((_DEC_ROWS, 1), F32),
                        pltpu.VMEM((_DEC_ROWS, 1), F32), pltpu.VMEM((_DEC_ROWS, A_VD), F32)],
    )
    heads = lambda a: a.reshape(n, A_HEADS, A_VD)
    return pl.pallas_call(
        functools.partial(_xffn_decode_kernel, lam_init=lam_init, past_len=past_len),
        grid_spec=grid_spec,
        out_shape=[jax.ShapeDtypeStruct((b, t, D_MODEL), F32),
                   jax.ShapeDtypeStruct((n, A_HEADS, A_VD), F32)],
        compiler_params=_params(("arbitrary",) * 3),
        name="xattn_ffn_decode",
    )(page_table, x, mkb, mvb, p["norm_x_g"], p["w_xq"], p["x_qnorm_g"], p["w_xo"], p["norm_ffn_g"],
      p["w_up"], p["w_down"], heads(q), heads(k_new), heads(v_new), p["a_lam"], p["a_subln_g"],
      *([cache_k] * _DEC_PAGES), *([cache_v] * _DEC_PAGES))


def _sample_tail_kernel(x_ref, o_ref, g0_ref, part_ref, mk_ref, mv_ref, wa_ref, wo_ref, gx_ref, wxq_ref,
                        qg_ref, wxo_ref, gf_ref, wup_ref, wdn_ref, out_ref, x1_scr, q_scr, om_scr):
    b = pl.program_id(0)
    nb = pl.num_programs(0)

    @pl.when(b == 0)
    def _():
        merged = g0_ref[...] * _dot(o_ref[...].astype(BF16), wa_ref[...]) + part_ref[...]
        x1 = x_ref[...] + _dot(merged.astype(BF16), wo_ref[...])
        x1_scr[...] = x1
        q = _dot(_rms(x1, gx_ref[...]).astype(BF16), wxq_ref[...])
        qg = qg_ref[...]
        for h in range(X_HEADS):
            q_scr[h] = _rms(q[:, h * X_HD:(h + 1) * X_HD], qg) * (X_HD ** -0.5)

    n_rows = mk_ref.shape[0]
    row = lax.broadcasted_iota(jnp.int32, (8, X_HD), 0)
    qz = sum(jnp.where(row == h, jnp.broadcast_to(q_scr[h, pl.ds(b, 1), :], (8, X_HD)), 0.0)
             for h in range(X_HEADS)).astype(BF16)
    s = _dot_nt(qz, mk_ref[...].astype(BF16))
    col = lax.broadcasted_iota(jnp.int32, (8, n_rows), 1)
    s = jnp.where((col & (X_HEADS - 1)) == lax.broadcasted_iota(jnp.int32, (8, n_rows), 0), s, NEG_INF)
    p = jnp.exp(s - jnp.max(s, axis=-1, keepdims=True))
    p = p / jnp.sum(p, axis=-1, keepdims=True)
    om = _dot(p.astype(BF16), mv_ref[...].astype(BF16))
    for h in range(X_HEADS):
        om_scr[h, pl.ds(b, 1), :] = om[h:h + 1, :]

    @pl.when(b == nb - 1)
    def _():
        x2 = x1_scr[...] + sum(_dot(om_scr[h].astype(BF16), wxo_ref[h * X_HD:(h + 1) * X_HD, :])
                               for h in range(X_HEADS))
        out_ref[...] = _ffn(x2, gf_ref[...], wup_ref, wdn_ref)


def _sample_tail(x, o, g0, part, mem_k, mem_v, layer, p):
    n = x.shape[0]
    n_rows = mem_k.shape[2]
    mem = pl.BlockSpec((None, None, n_rows, X_HD), lambda b: (layer, b, 0, 0))
    ins = [x, o, g0, part]
    ws = [p["w_a_out"], p["w_o"], p["norm_x_g"], p["w_xq"], p["x_qnorm_g"], p["w_xo"], p["norm_ffn_g"],
          p["w_up"], p["w_down"]]
    return pl.pallas_call(
        _sample_tail_kernel,
        grid=(n,),
        in_specs=[_full(a.shape) for a in ins] + [mem, mem] + [_full(a.shape) for a in ws],
        out_specs=_full((n, D_MODEL)),
        out_shape=jax.ShapeDtypeStruct((n, D_MODEL), F32),
        scratch_shapes=[pltpu.VMEM((n, D_MODEL), F32), pltpu.VMEM((X_HEADS, n, X_HD), F32),
                        pltpu.VMEM((X_HEADS, n, X_HD), F32)],
        compiler_params=_params(("arbitrary",)),
        name="sample_tail",
    )(*ins, mem_k, mem_v, *ws)


def _layer_params(l, a):
    row = lambda v: v[l].reshape(1, -1)
    b16 = lambda w: w.astype(BF16)
    w_in = a["w_in"][l]
    return dict(
        norm_mix_g=row(a["norm_mix_g"]),
        w_in_a=b16(w_in[:, :3 * A_W]), w_in_r=b16(w_in[:, 3 * A_W:]),
        a_qnorm_g=jnp.tile(a["a_qnorm_g"][l], A_W // A_HD).reshape(1, A_W),
        a_knorm_g=jnp.tile(a["a_knorm_g"][l], A_W // A_HD).reshape(1, A_W),
        a_lam=a["a_lam"][l], a_subln_g=row(a["a_subln_g"]),
        w_a_out=b16(a["w_a_out"][l]),
        b_conv_w=a["b_conv_w"][l], b_conv_b=row(a["b_conv_b"]), w_b_out=b16(a["w_b_out"][l]),
        c_conv_w=a["c_conv_w"][l], c_conv_b=row(a["c_conv_b"]),
        c_ln_g=row(a["c_ln_g"]), c_ln_b=row(a["c_ln_b"]), w_c_out=b16(a["w_c_out"][l]),
        d_ln_g=row(a["d_ln_g"]), d_ln_b=row(a["d_ln_b"]),
        d_ws=a["d_ws"][l], d_bs=a["d_bs"][l], d_bs_t=a["d_bs"][l].T, w_d_out=b16(a["w_d_out"][l]),
        w_o=b16(a["w_o"][l]),
        norm_x_g=row(a["norm_x_g"]), mem_norm_g=row(a["mem_norm_g"]),
        w_xq=b16(a["w_xq"][l]), w_xk=b16(a["w_xk"][l]), w_xv=b16(a["w_xv"][l]),
        x_qnorm_g=row(a["x_qnorm_g"]), x_knorm_g=row(a["x_knorm_g"]),
        w_xo=b16(a["w_xo"][l]), norm_ffn_g=row(a["norm_ffn_g"]),
        w_up=b16(a["w_up"][l]), w_down=b16(a["w_down"][l]),
    )


def kernel(x_prompt, x_sample, cache_k_a, cache_v_a, state_conv_b, state_conv_c, cache_mem_k, cache_mem_v,
           page_table, mem_prompt, norm_mix_g, w_in, a_qnorm_g, a_knorm_g, a_lam, a_subln_g, w_a_out,
           b_conv_w, b_conv_b, w_b_out, c_conv_w, c_conv_b, c_ln_g, c_ln_b, w_c_out, d_ln_g, d_ln_b,
           d_ws, d_bs, w_d_out, w_o, norm_x_g, mem_norm_g, w_xq, w_xk, w_xv, x_qnorm_g, x_knorm_g,
           w_xo, norm_ffn_g, w_up, w_down):
    weights = dict(norm_mix_g=norm_mix_g, w_in=w_in, a_qnorm_g=a_qnorm_g, a_knorm_g=a_knorm_g, a_lam=a_lam,
                   a_subln_g=a_subln_g, w_a_out=w_a_out, b_conv_w=b_conv_w, b_conv_b=b_conv_b,
                   w_b_out=w_b_out, c_conv_w=c_conv_w, c_conv_b=c_conv_b, c_ln_g=c_ln_g, c_ln_b=c_ln_b,
                   w_c_out=w_c_out, d_ln_g=d_ln_g, d_ln_b=d_ln_b, d_ws=d_ws, d_bs=d_bs, w_d_out=w_d_out,
                   w_o=w_o, norm_x_g=norm_x_g, mem_norm_g=mem_norm_g, w_xq=w_xq, w_xk=w_xk, w_xv=w_xv,
                   x_qnorm_g=x_qnorm_g, x_knorm_g=x_knorm_g, w_xo=w_xo, norm_ffn_g=norm_ffn_g,
                   w_up=w_up, w_down=w_down)
    depth = w_in.shape[0]
    bp, t, _ = x_prompt.shape
    ns = x_sample.shape[0]
    n_pool = cache_k_a.shape[1]
    n_mem = cache_mem_k.shape[2]
    cache_k = cache_k_a.reshape(depth, n_pool, _PAGE_ROWS, A_VD)
    cache_v = cache_v_a.reshape(depth, n_pool, _PAGE_ROWS, A_VD)
    mem_k_s = cache_mem_k.reshape(depth, ns, n_mem * X_HEADS, X_HD)
    mem_v_s = cache_mem_v.reshape(depth, ns, n_mem * X_HEADS, X_HD)
    slopes = jnp.asarray([2.0 ** (-8.0 * (i + 1) / A_HEADS) for i in range(A_HEADS)], F32)

    xp = x_prompt
    xs = x_sample.reshape(ns, D_MODEL)
    outs = [[] for _ in range(11)]
    for l in range(depth):
        p = _layer_params(l, weights)
        lam_init = 0.8 - 0.6 * math.exp(-0.3 * l)

        mk, mv, mkb, mvb = _memory_kv(mem_prompt, p["mem_norm_g"], p["w_xk"], p["w_xv"], p["x_knorm_g"])
        k_p, v_p, qb, kb, vb = _qkv_proj(xp, p["norm_mix_g"], p["w_in_a"], p["a_qnorm_g"], p["a_knorm_g"])
        o_p = _diff_attention(slopes, qb, kb, vb, p["a_lam"], p["a_subln_g"], lam_init)
        xp, cb_p, cc_p = _mixer(xp, o_p, p)

        sb = state_conv_b[l].reshape(ns, (B_K - 1) * B_W)
        sc = state_conv_c[l].reshape(ns, (C_K - 1) * C_W)
        q_s, k_s, v_s, cb_s, cc_s, dvn_s, part, g0 = _sample_mixer(xs, sb, sc, p)
        xp, o_s = _xattn_ffn_decode(xp, mkb, mvb, p, page_table, q_s, k_s, v_s, cache_k, cache_v, l,
                                    lam_init)
        xs = _sample_tail(xs, o_s.reshape(ns, A_W), g0, part, mem_k_s, mem_v_s, l, p)

        for lst, val in zip(outs, (
                k_p, v_p, cb_p, cc_p, mk, mv,
                k_s.reshape(ns, 1, A_HEADS, 2 * A_HD), v_s.reshape(ns, 1, A_HEADS, A_VD),
                cb_s.reshape(ns, B_K - 1, B_W), cc_s.reshape(ns, C_K - 1, C_W),
                dvn_s.reshape(ns, 1, D_W))):
            lst.append(val)
    return (xp, xs.reshape(ns, 1, D_MODEL)) + tuple(jnp.stack(o) for o in outs)
```

```python
import functools
import math

import jax
import jax.numpy as jnp
from jax import lax
from jax.experimental import pallas as pl
from jax.experimental.pallas import tpu as pltpu

F32 = jnp.float32
BF16 = jnp.bfloat16

D_MODEL = 1024
A_HEADS = 4
A_HD = 64
A_VD = 2 * A_HD
A_W = A_HEADS * 2 * A_HD
B_W = 256
B_K = 3
C_W = 256
C_K = 31
D_W = 256
D_GROUPS = 4
CHUNK = 128
N_BRANCH = 4
X_HEADS = 4
X_HD = 128
X_W = X_HEADS * X_HD
D_FF = 4 * D_MODEL
PAGE_SIZE = 128
EPS = 1e-6
NEG_INF = -1e30
LOG2E = math.log2(math.e)

_R_BX, _R_BB, _R_BC = 0, 256, 512
_R_CA, _R_CG = 768, 1024
_R_DU, _R_DV = 1280, 1536
_R_GZ = 1792
_R_COLS = _R_GZ + N_BRANCH * D_MODEL

_V7X_VMEM_BYTES = 64 * 1024 * 1024
_VMEM_LIMIT = _V7X_VMEM_BYTES - 8 * 1024 * 1024

_TM = 512
_TQ = 512
_TK = 512
_ATT_HEADS = 2
_SUBLANES = 8
_CONV_ROWS = 64
_B_HALO = 8
_C_HALO = 32
_FF_CHUNK = 1024
_DEC_PAGES = 16


def _rms(x, g):
    return x * lax.rsqrt(jnp.mean(x * x, axis=-1, keepdims=True) + EPS) * g


def _layer_norm(x, g, b):
    xc = x - jnp.mean(x, axis=-1, keepdims=True)
    var = jnp.mean(xc * xc, axis=-1, keepdims=True)
    return xc * lax.rsqrt(var + EPS) * g + b


def _sigmoid(x):
    return 0.5 * jnp.tanh(0.5 * x) + 0.5


def _dot(a, b):
    return jnp.dot(a, b, preferred_element_type=F32)


def _dot_nt(a, b):
    return lax.dot_general(a, b, (((1,), (1,)), ((), ())), preferred_element_type=F32)


def _idiv(x, d):
    assert d & (d - 1) == 0
    return lax.shift_right_logical(x, int(math.log2(d)))


def _group_mean_matrix(width, group):
    r = _idiv(lax.broadcasted_iota(jnp.int32, (width, width), 0), group)
    c = _idiv(lax.broadcasted_iota(jnp.int32, (width, width), 1), group)
    return jnp.where(r == c, 1.0 / group, 0.0).astype(BF16)


def _group_rms(t, g, gm):
    ms = _dot((t * t).astype(BF16), gm)
    return t * lax.rsqrt(ms + EPS) * g


def _lam(lam_ref, lam_init):
    a = lam_ref[...]
    s1 = jnp.sum(a[0:1] * a[1:2], axis=-1, keepdims=True)
    s2 = jnp.sum(a[2:3] * a[3:4], axis=-1, keepdims=True)
    return jnp.exp(s1) - jnp.exp(s2) + lam_init


def _full(shape):
    return pl.BlockSpec(shape, lambda *_: (0,) * len(shape))


def _params(sem):
    return pltpu.CompilerParams(dimension_semantics=sem, vmem_limit_bytes=_VMEM_LIMIT)


def _memkv_kernel(mem_ref, g_ref, wk_ref, wv_ref, kg_ref, mk_ref, mv_ref, mkb_ref, mvb_ref):
    h = _rms(mem_ref[...], g_ref[...]).astype(BF16)
    k = _dot(h, wk_ref[...])
    v = _dot(h, wv_ref[...])
    kg = kg_ref[...]
    k = jnp.concatenate([_rms(k[:, i * X_HD:(i + 1) * X_HD], kg) for i in range(X_HEADS)], axis=1)
    mk_ref[...] = k.reshape(mk_ref.shape)
    mv_ref[...] = v.reshape(mv_ref.shape)
    mkb_ref[...] = k.astype(BF16)
    mvb_ref[...] = v.astype(BF16)


def _memory_kv(mem, g, wk, wv, kg):
    b, n, _ = mem.shape
    blk = pl.BlockSpec((None, n, X_W), lambda i: (i, 0, 0))
    hblk = pl.BlockSpec((None, n, X_HEADS, X_HD), lambda i: (i, 0, 0, 0))
    return pl.pallas_call(
        _memkv_kernel,
        grid=(b,),
        in_specs=[pl.BlockSpec((None, n, D_MODEL), lambda i: (i, 0, 0)), _full((1, D_MODEL)),
                  _full((D_MODEL, X_W)), _full((D_MODEL, X_W)), _full((1, X_HD))],
        out_specs=[hblk, hblk, blk, blk],
        out_shape=[jax.ShapeDtypeStruct((b, n, X_HEADS, X_HD), F32),
                   jax.ShapeDtypeStruct((b, n, X_HEADS, X_HD), F32),
                   jax.ShapeDtypeStruct((b, n, X_W), BF16), jax.ShapeDtypeStruct((b, n, X_W), BF16)],
        compiler_params=_params(("arbitrary",)),
        name="memory_kv",
    )(mem, g, wk, wv, kg)


def _qkv_from_h(h, w_ref, qg, kg):
    z = _dot(h, w_ref[...])
    gm = _group_mean_matrix(A_W, A_HD)
    q = _group_rms(z[:, :A_W], qg, gm) * (A_HD ** -0.5)
    k = _group_rms(z[:, A_W:2 * A_W], kg, gm)
    v = z[:, 2 * A_W:]
    return q, k, v


def _qkv_kernel(x_ref, g_ref, w_ref, qg_ref, kg_ref, *rest):
    k_ref, v_ref, qb_ref, kb_ref, vb_ref = rest[-5:]
    h = _rms(x_ref[...], g_ref[...]).astype(BF16)
    q, k, v = _qkv_from_h(h, w_ref, qg_ref[...], kg_ref[...])
    k_ref[...] = k.reshape(k_ref.shape)
    v_ref[...] = v.reshape(v_ref.shape)
    qb_ref[...] = (q * LOG2E).astype(BF16)
    kb_ref[...] = k.astype(BF16)
    vb_ref[...] = v.astype(BF16)


def _qkv_proj(x, g, w_a, qg, kg, layer, depth, stacks):
    b, t, _ = x.shape
    blk = pl.BlockSpec((None, _TM, A_W), lambda i, j: (i, j, 0))
    hblk = pl.BlockSpec((None, None, _TM, A_HEADS, A_VD), lambda i, j: (layer, i, j, 0, 0))
    f32 = jax.ShapeDtypeStruct((depth, b, t, A_HEADS, A_VD), F32)
    b16 = jax.ShapeDtypeStruct((b, t, A_W), BF16)
    ins = [x, g, w_a, qg, kg]
    in_specs = [pl.BlockSpec((None, _TM, D_MODEL), lambda i, j: (i, j, 0)), _full((1, D_MODEL)),
                _full((D_MODEL, 3 * A_W)), _full((1, A_W)), _full((1, A_W))]
    aliases = {}
    if stacks is not None:
        aliases = {len(ins): 0, len(ins) + 1: 1}
        ins += list(stacks)
        in_specs += [pl.BlockSpec(memory_space=pl.ANY)] * 2
    return pl.pallas_call(
        _qkv_kernel,
        grid=(b, t // _TM),
        in_specs=in_specs,
        out_specs=[hblk, hblk, blk, blk, blk],
        out_shape=[f32, f32, b16, b16, b16],
        input_output_aliases=aliases,
        compiler_params=_params(("arbitrary", "arbitrary")),
        name="qkv_proj",
    )(*ins)


def _diff_attn_kernel(slope_ref, q_ref, k_ref, v_ref, lam_ref, sg_ref, o_ref,
                      qs_scr, m_scr, l_scr, acc_scr, *, lam_init):
    hg = pl.program_id(1)
    q0 = pl.program_id(2) * _TQ
    rows = 2 * _TQ

    for hh in range(_ATT_HEADS):
        q = q_ref[:, hh * A_VD:(hh + 1) * A_VD]
        lane = lax.broadcasted_iota(jnp.int32, q.shape, 1)
        zero = jnp.zeros_like(q)
        qs_scr[hh, 0:_TQ, :] = jnp.where(lane < A_HD, q, zero)
        qs_scr[hh, _TQ:rows, :] = jnp.where(lane >= A_HD, q, zero)
    m_scr[...] = jnp.full(m_scr.shape, NEG_INF, F32)
    l_scr[...] = jnp.zeros(l_scr.shape, F32)
    acc_scr[...] = jnp.zeros(acc_scr.shape, F32)

    def chunk(j, hh, masked):
        k0 = pl.multiple_of(j * _TK, _TK)
        head = slice(hh * A_VD, (hh + 1) * A_VD)
        s = _dot_nt(qs_scr[hh], k_ref[pl.ds(k0, _TK), head])
        kpos = k0 + lax.broadcasted_iota(jnp.int32, (1, _TK), 1)
        s = s + (slope_ref[hg * _ATT_HEADS + hh] * LOG2E) * kpos.astype(F32)
        if masked:
            row = lax.broadcasted_iota(jnp.int32, (rows, _TK), 0)
            qpos = q0 + jnp.where(row >= _TQ, row - _TQ, row)
            col = k0 + lax.broadcasted_iota(jnp.int32, (rows, _TK), 1)
            s = jnp.where(qpos >= col, s, NEG_INF)
        m_old = m_scr[hh]
        m_new = jnp.maximum(m_old, jnp.max(s, axis=-1, keepdims=True))
        alpha = jnp.exp2(m_old - m_new)
        p = jnp.exp2(s - jnp.concatenate([m_new] * (_TK // A_VD), axis=1))
        l_scr[hh] = alpha * l_scr[hh] + jnp.sum(p, axis=-1, keepdims=True)
        acc_scr[hh] = alpha * acc_scr[hh] + _dot(p.astype(BF16), v_ref[pl.ds(k0, _TK), head])
        m_scr[hh] = m_new

    n_full = q0 // _TK

    def body(j, carry):
        for hh in range(_ATT_HEADS):
            chunk(j, hh, False)
        return carry

    lax.fori_loop(0, n_full, body, 0)
    for d in range(_TQ // _TK):
        for hh in range(_ATT_HEADS):
            chunk(n_full + d, hh, True)

    lam = _lam(lam_ref, lam_init)
    for hh in range(_ATT_HEADS):
        o = acc_scr[hh] / l_scr[hh]
        o = o[0:_TQ] - lam * o[_TQ:rows]
        o_ref[:, hh * A_VD:(hh + 1) * A_VD] = (_rms(o, sg_ref[...]) * (1.0 - lam_init)).astype(BF16)


def _diff_attention(slopes, qb, kb, vb, a_lam, sg, lam_init):
    b, t, _ = qb.shape
    assert _TQ % _TK == 0 and t % _TQ == 0 and A_HEADS % _ATT_HEADS == 0
    width = _ATT_HEADS * A_VD
    qblk = pl.BlockSpec((None, _TQ, width), lambda bi, h, qi: (bi, qi, h))
    kvblk = pl.BlockSpec((None, t, width), lambda bi, h, qi: (bi, 0, h))
    stat = pltpu.VMEM((_ATT_HEADS, 2 * _TQ, A_VD), F32)
    return pl.pallas_call(
        functools.partial(_diff_attn_kernel, lam_init=lam_init),
        grid=(b, A_HEADS // _ATT_HEADS, t // _TQ),
        in_specs=[pl.BlockSpec(memory_space=pltpu.SMEM), qblk, kvblk, kvblk,
                  _full((4, A_HD)), _full((1, A_VD))],
        out_specs=qblk,
        out_shape=jax.ShapeDtypeStruct((b, t, A_W), BF16),
        scratch_shapes=[pltpu.VMEM((_ATT_HEADS, 2 * _TQ, A_VD), BF16), stat, stat, stat],
        compiler_params=_params(("arbitrary",) * 3),
        name="diff_attention",
    )(slopes, qb, kb, vb, a_lam, sg)


def _lane_group_mask(width, group, g):
    lane = lax.broadcasted_iota(jnp.int32, (1, width), 1)
    return (_idiv(lane, group) == g).astype(F32)


def _mixer_kernel(x_ref, o_ref, g_ref, wr_ref, wa_ref, wb_ref, wc_ref, wd_ref, wo_ref,
                  bcw_ref, bcb_ref, ccw_ref, ccb_ref, clg_ref, clb_ref, dlg_ref, dlb_ref,
                  dws_ref, dbst_ref, x1_ref, cbp_ref, ccp_ref, ubuf, cbuf, ycin, shwin, gates):
    tm = _TM

    @pl.when(pl.program_id(1) == 0)
    def _():
        ubuf[0:_B_HALO, :] = jnp.zeros((_B_HALO, B_W), F32)
        cbuf[0:_C_HALO, :] = jnp.zeros((_C_HALO, C_W), F32)

    x = x_ref[...]
    h = _rms(x, g_ref[...]).astype(BF16)

    zc = _dot(h, wr_ref[:, _R_CA:_R_DU])
    cbuf[_C_HALO:_C_HALO + tm, :] = zc[:, 0:256] * _sigmoid(zc[:, 256:512])
    base = _C_HALO - (C_K - 1)
    gcols = N_BRANCH * D_MODEL // (tm // _CONV_ROWS)
    for it, r in enumerate(range(0, tm, _CONV_ROWS)):
        c0 = _R_GZ + it * gcols
        gates[:, it * gcols:(it + 1) * gcols] = _sigmoid(_dot(h, wr_ref[:, c0:c0 + gcols]))
        acc = jnp.broadcast_to(ccb_ref[...], (_CONV_ROWS, C_W))
        for ph in range(_SUBLANES):
            n_taps = (C_K - 1 - ph) // _SUBLANES + 1
            rows = _CONV_ROWS + _SUBLANES * (n_taps - 1)
            shwin[ph, 0:rows, :] = cbuf[base + r + ph:base + r + ph + rows, :]
            for i in range(n_taps):
                k = ph + _SUBLANES * i
                acc = acc + ccw_ref[k:k + 1, :] * shwin[ph, _SUBLANES * i:_SUBLANES * i + _CONV_ROWS, :]
        y = _layer_norm(acc, clg_ref[...], clb_ref[...])
        ycin[r:r + _CONV_ROWS, :] = (y * _sigmoid(y)).astype(BF16)
    ccp_ref[...] = cbuf[_C_HALO + tm - (C_K - 1):_C_HALO + tm, :]
    cbuf[0:_C_HALO, :] = cbuf[tm:tm + _C_HALO, :]

    def gate(i):
        return gates[:, i * D_MODEL:(i + 1) * D_MODEL]

    merged = gate(2) * _dot(ycin[...], wc_ref[...])

    merged = merged + gate(0) * _dot(o_ref[...], wa_ref[...])

    zb = _dot(h, wr_ref[:, _R_BX:_R_CA])
    u = zb[:, 512:768] * zb[:, 0:256]
    ubuf[_B_HALO:_B_HALO + tm, :] = u
    conv_b = (bcw_ref[0:1, :] * ubuf[_B_HALO - 2:_B_HALO - 2 + tm, :]
              + bcw_ref[1:2, :] * ubuf[_B_HALO - 1:_B_HALO - 1 + tm, :]
              + bcw_ref[2:3, :] * u + bcb_ref[...])
    yb = _dot((zb[:, 256:512] * conv_b).astype(BF16), wb_ref[...])
    merged = merged + gate(1) * yb
    cbp_ref[...] = ubuf[_B_HALO + tm - (B_K - 1):_B_HALO + tm, :]
    ubuf[0:_B_HALO, :] = ubuf[tm:tm + _B_HALO, :]

    zd = jax.nn.gelu(_dot(h, wr_ref[:, _R_DU:_R_GZ]))
    du = zd[:, 0:256]
    dvn = _layer_norm(zd[:, 256:512], dlg_ref[...], dlb_ref[...])
    row = lax.broadcasted_iota(jnp.int32, (CHUNK, CHUNK), 0)
    col = lax.broadcasted_iota(jnp.int32, (CHUNK, CHUNK), 1)
    tril = (row >= col).astype(F32)
    wcat = jnp.concatenate([dws_ref[g] * tril for g in range(D_GROUPS)], axis=1).astype(BF16)
    masks = [_lane_group_mask(D_W, D_W // D_GROUPS, g) for g in range(D_GROUPS)]
    bsmat = sum(dbst_ref[:, g:g + 1] * masks[g] for g in range(D_GROUPS))
    for c in range(0, tm, CHUNK):
        vch = dvn[c:c + CHUNK]
        rhs = jnp.concatenate([(vch * masks[g]).astype(BF16) for g in range(D_GROUPS)], axis=0)
        s = _dot(wcat, rhs) + bsmat
        ycin[c:c + CHUNK, :] = (du[c:c + CHUNK] * s).astype(BF16)
    yd = _dot(ycin[...], wd_ref[...])
    merged = merged + gate(3) * yd

    x1_ref[...] = x + _dot(merged.astype(BF16), wo_ref[...])


def _mixer(x, o, p):
    b, t, _ = x.shape
    row = lambda w: pl.BlockSpec((None, _TM, w), lambda i, j: (i, j, 0))
    return pl.pallas_call(
        _mixer_kernel,
        grid=(b, t // _TM),
        in_specs=[row(D_MODEL), row(A_W), _full((1, D_MODEL)), _full((D_MODEL, _R_COLS)),
                  _full((A_W, D_MODEL)), _full((B_W, D_MODEL)), _full((C_W, D_MODEL)),
                  _full((D_W, D_MODEL)), _full((D_MODEL, D_MODEL)),
                  _full((B_K, B_W)), _full((1, B_W)), _full((C_K, C_W)), _full((1, C_W)),
                  _full((1, C_W)), _full((1, C_W)), _full((1, D_W)), _full((1, D_W)),
                  _full((D_GROUPS, CHUNK, CHUNK)), _full((CHUNK, D_GROUPS))],
        out_specs=[row(D_MODEL),
                   pl.BlockSpec((None, B_K - 1, B_W), lambda i, j: (i, 0, 0)),
                   pl.BlockSpec((None, C_K - 1, C_W), lambda i, j: (i, 0, 0))],
        out_shape=[jax.ShapeDtypeStruct((b, t, D_MODEL), F32),
                   jax.ShapeDtypeStruct((b, B_K - 1, B_W), F32),
                   jax.ShapeDtypeStruct((b, C_K - 1, C_W), F32)],
        scratch_shapes=[pltpu.VMEM((_B_HALO + _TM, B_W), F32), pltpu.VMEM((_C_HALO + _TM, C_W), F32),
                        pltpu.VMEM((_TM, C_W), BF16),
                        pltpu.VMEM((_SUBLANES, _CONV_ROWS + _C_HALO, C_W), F32),
                        pltpu.VMEM((_TM, N_BRANCH * D_MODEL), F32)],
        compiler_params=_params(("arbitrary", "arbitrary")),
        name="mixer",
    )(x, o, p["norm_mix_g"], p["w_in_r"], p["w_a_out"], p["w_b_out"], p["w_c_out"], p["w_d_out"],
      p["w_o"], p["b_conv_w"], p["b_conv_b"], p["c_conv_w"], p["c_conv_b"], p["c_ln_g"], p["c_ln_b"],
      p["d_ln_g"], p["d_ln_b"], p["d_ws"], p["d_bs_t"])


def _ffn(x2, gf, wup_ref, wdn_ref):
    h3 = _rms(x2, gf).astype(BF16)
    acc = jnp.zeros(x2.shape, F32)
    for c in range(0, D_FF, _FF_CHUNK):
        a = jnp.maximum(_dot(h3, wup_ref[:, c:c + _FF_CHUNK]), 0.0)
        acc = acc + _dot((a * a).astype(BF16), wdn_ref[c:c + _FF_CHUNK, :])
    return x2 + acc


def _cross_attention(x, mk_ref, mv_ref, gx, wxq_ref, qg, wxo_ref):
    h2 = _rms(x, gx).astype(BF16)
    q = _dot(h2, wxq_ref[...])
    oms = []
    for i in range(X_HEADS):
        sl = slice(i * X_HD, (i + 1) * X_HD)
        qh = (_rms(q[:, sl], qg) * (X_HD ** -0.5)).astype(BF16)
        s = _dot_nt(qh, mk_ref[:, sl])
        p = jnp.exp(s - jnp.max(s, axis=-1, keepdims=True))
        oms.append(_dot(p.astype(BF16), mv_ref[:, sl]) / jnp.sum(p, axis=-1, keepdims=True))
    om = jnp.concatenate(oms, axis=1).astype(BF16)
    return x + _dot(om, wxo_ref[...])


def _xffn_kernel(x_ref, mk_ref, mv_ref, gx_ref, wxq_ref, qg_ref, wxo_ref, gf_ref, wup_ref, wdn_ref,
                 out_ref):
    x2 = _cross_attention(x_ref[...], mk_ref, mv_ref, gx_ref[...], wxq_ref, qg_ref[...], wxo_ref)
    out_ref[...] = _ffn(x2, gf_ref[...], wup_ref, wdn_ref)


def _xattn_ffn(x, mkb, mvb, p):
    b, t, _ = x.shape
    n = mkb.shape[1]
    row = pl.BlockSpec((None, _TM, D_MODEL), lambda i, j: (i, j, 0))
    mem = pl.BlockSpec((None, n, X_W), lambda i, j: (i, 0, 0))
    return pl.pallas_call(
        _xffn_kernel,
        grid=(b, t // _TM),
        in_specs=[row, mem, mem, _full((1, D_MODEL)), _full((D_MODEL, X_W)), _full((1, X_HD)),
                  _full((X_W, D_MODEL)), _full((1, D_MODEL)), _full((D_MODEL, D_FF)),
                  _full((D_FF, D_MODEL))],
        out_specs=row,
        out_shape=jax.ShapeDtypeStruct((b, t, D_MODEL), F32),
        compiler_params=_params(("arbitrary", "arbitrary")),
        name="xattn_ffn",
    )(x, mkb, mvb, p["norm_x_g"], p["w_xq"], p["x_qnorm_g"], p["w_xo"], p["norm_ffn_g"],
      p["w_up"], p["w_down"])


def _sample_mixer_kernel(x_ref, sb_ref, sc_ref, g_ref, wa_ref, wr_ref, qg_ref, kg_ref,
                         wb_ref, wc_ref, wd_ref, bcw_ref, bcb_ref, ccw_ref, ccb_ref, clg_ref, clb_ref,
                         dlg_ref, dlb_ref, dws_ref, dbs_ref,
                         q_ref, k_ref, v_ref, cb_ref, cc_ref, dvn_ref, part_ref, g0_ref):
    x = x_ref[...]
    h = _rms(x, g_ref[...]).astype(BF16)
    q, k, v = _qkv_from_h(h, wa_ref, qg_ref[...], kg_ref[...])
    q_ref[...] = q
    k_ref[...] = k
    v_ref[...] = v

    def gate(i):
        return _sigmoid(_dot(h, wr_ref[:, _R_GZ + i * D_MODEL:_R_GZ + (i + 1) * D_MODEL]))

    g0_ref[...] = gate(0)

    zb = _dot(h, wr_ref[:, _R_BX:_R_CA])
    u = zb[:, 512:768] * zb[:, 0:256]
    conv_b = (bcw_ref[0:1, :] * sb_ref[:, 0:B_W] + bcw_ref[1:2, :] * sb_ref[:, B_W:2 * B_W]
              + bcw_ref[2:3, :] * u + bcb_ref[...])
    part = gate(1) * _dot((zb[:, 256:512] * conv_b).astype(BF16), wb_ref[...])
    cb_ref[:, 0:B_W] = sb_ref[:, B_W:2 * B_W]
    cb_ref[:, B_W:2 * B_W] = u

    zc = _dot(h, wr_ref[:, _R_CA:_R_DU])
    uc = zc[:, 0:256] * _sigmoid(zc[:, 256:512])
    acc = ccw_ref[C_K - 1:C_K, :] * uc + ccb_ref[...]
    for kk in range(C_K - 1):
        acc = acc + ccw_ref[kk:kk + 1, :] * sc_ref[:, kk * C_W:(kk + 1) * C_W]
    y = _layer_norm(acc, clg_ref[...], clb_ref[...])
    yc = _dot((y * _sigmoid(y)).astype(BF16), wc_ref[...])
    part = part + gate(2) * yc
    cc_ref[:, 0:(C_K - 2) * C_W] = sc_ref[:, C_W:(C_K - 1) * C_W]
    cc_ref[:, (C_K - 2) * C_W:(C_K - 1) * C_W] = uc

    zd = jax.nn.gelu(_dot(h, wr_ref[:, _R_DU:_R_GZ]))
    dvn = _layer_norm(zd[:, 256:512], dlg_ref[...], dlb_ref[...])
    dvn_ref[...] = dvn
    w00 = sum(dws_ref[g, 0:1, 0:1] * _lane_group_mask(D_W, D_W // D_GROUPS, g) for g in range(D_GROUPS))
    b0 = sum(dbs_ref[g:g + 1, 0:1] * _lane_group_mask(D_W, D_W // D_GROUPS, g) for g in range(D_GROUPS))
    yd = _dot((zd[:, 0:256] * (w00 * dvn + b0)).astype(BF16), wd_ref[...])
    part_ref[...] = part + gate(3) * yd


def _sample_mixer(x, sb, sc, p):
    n = x.shape[0]
    shapes = [(n, A_W), (n, A_W), (n, A_W), (n, (B_K - 1) * B_W), (n, (C_K - 1) * C_W), (n, D_W),
              (n, D_MODEL), (n, D_MODEL)]
    ins = [x, sb, sc, p["norm_mix_g"], p["w_in_a"], p["w_in_r"], p["a_qnorm_g"], p["a_knorm_g"],
           p["w_b_out"], p["w_c_out"], p["w_d_out"], p["b_conv_w"], p["b_conv_b"], p["c_conv_w"],
           p["c_conv_b"], p["c_ln_g"], p["c_ln_b"], p["d_ln_g"], p["d_ln_b"], p["d_ws"], p["d_bs"]]
    return pl.pallas_call(
        _sample_mixer_kernel,
        grid=(1,),
        in_specs=[_full(a.shape) for a in ins],
        out_specs=[_full(s) for s in shapes],
        out_shape=[jax.ShapeDtypeStruct(s, F32) for s in shapes],
        compiler_params=_params(("arbitrary",)),
        name="sample_mixer",
    )(*ins)


_DEC_ROWS = 2 * A_HEADS
_PAGE_ROWS = PAGE_SIZE * A_HEADS


def _head_rows(ref, row, per_head):
    return sum(jnp.where(_idiv(row, per_head) == h, jnp.broadcast_to(ref[h:h + 1, :], row.shape), 0.0)
               for h in range(ref.shape[0]))


def _decode_init(q_ref, qz_scr, m_scr, l_scr, acc_scr):
    row = lax.broadcasted_iota(jnp.int32, (_DEC_ROWS, A_VD), 0)
    lane = lax.broadcasted_iota(jnp.int32, (_DEC_ROWS, A_VD), 1)
    qz_scr[...] = jnp.where(_idiv(lane, A_HD) == (row & 1), _head_rows(q_ref, row, 2), 0.0)
    m_scr[...] = jnp.full(m_scr.shape, NEG_INF, F32)
    l_scr[...] = jnp.zeros(l_scr.shape, F32)
    acc_scr[...] = jnp.zeros(acc_scr.shape, F32)


def _decode_pages(j, k_refs, v_refs, qz_scr, m_scr, l_scr, acc_scr, past_len):
    n_pages = len(k_refs)
    width = n_pages * _PAGE_ROWS
    rowv = lax.broadcasted_iota(jnp.int32, (_DEC_ROWS, 1), 0)
    slope = sum(jnp.where(_idiv(rowv, 2) == i, 2.0 ** (-8.0 * (i + 1) / A_HEADS), 0.0)
                for i in range(A_HEADS))
    qz = qz_scr[...].astype(BF16)
    s = jnp.concatenate([_dot_nt(qz, k_refs[i][...].astype(BF16)) for i in range(n_pages)], axis=1)
    col = lax.broadcasted_iota(jnp.int32, (_DEC_ROWS, width), 1)
    kpos = j * (n_pages * PAGE_SIZE) + _idiv(lax.broadcasted_iota(jnp.int32, (1, width), 1), A_HEADS)
    own_head = (col & (A_HEADS - 1)) == _idiv(lax.broadcasted_iota(jnp.int32, (_DEC_ROWS, width), 0), 2)
    s = jnp.where(own_head, s - slope * (past_len - kpos).astype(F32), NEG_INF)
    m_old = m_scr[...]
    m_new = jnp.maximum(m_old, jnp.max(s, axis=-1, keepdims=True))
    alpha = jnp.exp(m_old - m_new)
    p = jnp.exp(s - m_new)
    l_scr[...] = alpha * l_scr[...] + jnp.sum(p, axis=-1, keepdims=True)
    p = p.astype(BF16)
    pv = sum(_dot(p[:, i * _PAGE_ROWS:(i + 1) * _PAGE_ROWS], v_refs[i][...].astype(BF16))
             for i in range(n_pages))
    acc_scr[...] = alpha * acc_scr[...] + pv
    m_scr[...] = m_new


def _decode_finish(kn_ref, vn_ref, lam_ref, sg_ref, o_ref, qz_scr, m_scr, l_scr, acc_scr, lam_init):
    row = lax.broadcasted_iota(jnp.int32, (_DEC_ROWS, A_VD), 0)
    s_new = jnp.sum(qz_scr[...] * _head_rows(kn_ref, row, 2), axis=-1, keepdims=True)
    m_old = m_scr[...]
    m_fin = jnp.maximum(m_old, s_new)
    alpha = jnp.exp(m_old - m_fin)
    p_new = jnp.exp(s_new - m_fin)
    l_fin = alpha * l_scr[...] + p_new
    o_all = (alpha * acc_scr[...] + p_new * _head_rows(vn_ref, row, 2)) / l_fin
    lam = _lam(lam_ref, lam_init)
    sg = sg_ref[...]
    for h in range(A_HEADS):
        o = o_all[2 * h:2 * h + 1] - lam * o_all[2 * h + 1:2 * h + 2]
        o_ref[h:h + 1, :] = _rms(o, sg) * (1.0 - lam_init)


def _decode_attn_kernel(pt_ref, q_ref, kn_ref, vn_ref, lam_ref, sg_ref, *rest, lam_init, past_len):
    del pt_ref
    k_refs = rest[:_DEC_PAGES]
    v_refs = rest[_DEC_PAGES:2 * _DEC_PAGES]
    o_ref = rest[2 * _DEC_PAGES]
    dec = rest[2 * _DEC_PAGES + 1:]
    j = pl.program_id(1)

    @pl.when(j == 0)
    def _():
        _decode_init(q_ref, *dec)

    _decode_pages(j, k_refs, v_refs, *dec, past_len)

    @pl.when(j == pl.num_programs(1) - 1)
    def _():
        _decode_finish(kn_ref, vn_ref, lam_ref, sg_ref, o_ref, *dec, lam_init)


def _decode_attention(page_table, q, k_new, v_new, cache_k, cache_v, layer, a_lam, sg, lam_init):
    n, n_pages = page_table.shape
    past_len = n_pages * PAGE_SIZE
    row = pl.BlockSpec((None, A_HEADS, A_VD), lambda b, j, pt: (b, 0, 0))

    def page(i):
        return pl.BlockSpec((None, None, _PAGE_ROWS, A_VD),
                            lambda b, j, pt: (layer, pt[b, j * _DEC_PAGES + i], 0, 0))

    grid_spec = pltpu.PrefetchScalarGridSpec(
        num_scalar_prefetch=1,
        grid=(n, n_pages // _DEC_PAGES),
        in_specs=[row, row, row, pl.BlockSpec((4, A_HD), lambda b, j, pt: (0, 0)),
                  pl.BlockSpec((1, A_VD), lambda b, j, pt: (0, 0))]
        + [page(i) for i in range(_DEC_PAGES)] + [page(i) for i in range(_DEC_PAGES)],
        out_specs=row,
        scratch_shapes=[pltpu.VMEM((_DEC_ROWS, A_VD), F32), pltpu.VMEM((_DEC_ROWS, 1), F32),
                        pltpu.VMEM((_DEC_ROWS, 1), F32), pltpu.VMEM((_DEC_ROWS, A_VD), F32)],
    )
    heads = lambda a: a.reshape(n, A_HEADS, A_VD)
    return pl.pallas_call(
        functools.partial(_decode_attn_kernel, lam_init=lam_init, past_len=past_len),
        grid_spec=grid_spec,
        out_shape=jax.ShapeDtypeStruct((n, A_HEADS, A_VD), F32),
        compiler_params=_params(("arbitrary", "arbitrary")),
        name="decode_attention",
    )(page_table, heads(q), heads(k_new), heads(v_new), a_lam, sg,
      *([cache_k] * _DEC_PAGES), *([cache_v] * _DEC_PAGES))


def _sample_tail_kernel(x_ref, o_ref, g0_ref, part_ref, mk_ref, mv_ref, wa_ref, wo_ref, gx_ref, wxq_ref,
                        qg_ref, wxo_ref, gf_ref, wup_ref, wdn_ref, out_ref, x1_scr, q_scr, om_scr):
    b = pl.program_id(0)
    nb = pl.num_programs(0)

    @pl.when(b == 0)
    def _():
        merged = g0_ref[...] * _dot(o_ref[...].astype(BF16), wa_ref[...]) + part_ref[...]
        x1 = x_ref[...] + _dot(merged.astype(BF16), wo_ref[...])
        x1_scr[...] = x1
        q = _dot(_rms(x1, gx_ref[...]).astype(BF16), wxq_ref[...])
        qg = qg_ref[...]
        for h in range(X_HEADS):
            q_scr[h] = _rms(q[:, h * X_HD:(h + 1) * X_HD], qg) * (X_HD ** -0.5)

    n_rows = mk_ref.shape[0]
    row = lax.broadcasted_iota(jnp.int32, (8, X_HD), 0)
    qz = sum(jnp.where(row == h, jnp.broadcast_to(q_scr[h, pl.ds(b, 1), :], (8, X_HD)), 0.0)
             for h in range(X_HEADS)).astype(BF16)
    s = _dot_nt(qz, mk_ref[...].astype(BF16))
    col = lax.broadcasted_iota(jnp.int32, (8, n_rows), 1)
    s = jnp.where((col & (X_HEADS - 1)) == lax.broadcasted_iota(jnp.int32, (8, n_rows), 0), s, NEG_INF)
    p = jnp.exp(s - jnp.max(s, axis=-1, keepdims=True))
    p = p / jnp.sum(p, axis=-1, keepdims=True)
    om = _dot(p.astype(BF16), mv_ref[...].astype(BF16))
    for h in range(X_HEADS):
        om_scr[h, pl.ds(b, 1), :] = om[h:h + 1, :]

    @pl.when(b == nb - 1)
    def _():
        x2 = x1_scr[...] + sum(_dot(om_scr[h].astype(BF16), wxo_ref[h * X_HD:(h + 1) * X_HD, :])
                               for h in range(X_HEADS))
        out_ref[...] = _ffn(x2, gf_ref[...], wup_ref, wdn_ref)


def _sample_tail(x, o, g0, part, mem_k, mem_v, layer, p):
    n = x.shape[0]
    n_rows = mem_k.shape[2]
    mem = pl.BlockSpec((None, None, n_rows, X_HD), lambda b: (layer, b, 0, 0))
    ins = [x, o, g0, part]
    ws = [p["w_a_out"], p["w_o"], p["norm_x_g"], p["w_xq"], p["x_qnorm_g"], p["w_xo"], p["norm_ffn_g"],
          p["w_up"], p["w_down"]]
    return pl.pallas_call(
        _sample_tail_kernel,
        grid=(n,),
        in_specs=[_full(a.shape) for a in ins] + [mem, mem] + [_full(a.shape) for a in ws],
        out_specs=_full((n, D_MODEL)),
        out_shape=jax.ShapeDtypeStruct((n, D_MODEL), F32),
        scratch_shapes=[pltpu.VMEM((n, D_MODEL), F32), pltpu.VMEM((X_HEADS, n, X_HD), F32),
                        pltpu.VMEM((X_HEADS, n, X_HD), F32)],
        compiler_params=_params(("arbitrary",)),
        name="sample_tail",
    )(*ins, mem_k, mem_v, *ws)


def _layer_params(l, a):
    row = lambda v: v[l].reshape(1, -1)
    b16 = lambda w: w.astype(BF16)
    w_in = a["w_in"][l]
    return dict(
        norm_mix_g=row(a["norm_mix_g"]),
        w_in_a=b16(w_in[:, :3 * A_W]), w_in_r=b16(w_in[:, 3 * A_W:]),
        a_qnorm_g=jnp.tile(a["a_qnorm_g"][l], A_W // A_HD).reshape(1, A_W),
        a_knorm_g=jnp.tile(a["a_knorm_g"][l], A_W // A_HD).reshape(1, A_W),
        a_lam=a["a_lam"][l], a_subln_g=row(a["a_subln_g"]),
        w_a_out=b16(a["w_a_out"][l]),
        b_conv_w=a["b_conv_w"][l], b_conv_b=row(a["b_conv_b"]), w_b_out=b16(a["w_b_out"][l]),
        c_conv_w=a["c_conv_w"][l], c_conv_b=row(a["c_conv_b"]),
        c_ln_g=row(a["c_ln_g"]), c_ln_b=row(a["c_ln_b"]), w_c_out=b16(a["w_c_out"][l]),
        d_ln_g=row(a["d_ln_g"]), d_ln_b=row(a["d_ln_b"]),
        d_ws=a["d_ws"][l], d_bs=a["d_bs"][l], d_bs_t=a["d_bs"][l].T, w_d_out=b16(a["w_d_out"][l]),
        w_o=b16(a["w_o"][l]),
        norm_x_g=row(a["norm_x_g"]), mem_norm_g=row(a["mem_norm_g"]),
        w_xq=b16(a["w_xq"][l]), w_xk=b16(a["w_xk"][l]), w_xv=b16(a["w_xv"][l]),
        x_qnorm_g=row(a["x_qnorm_g"]), x_knorm_g=row(a["x_knorm_g"]),
        w_xo=b16(a["w_xo"][l]), norm_ffn_g=row(a["norm_ffn_g"]),
        w_up=b16(a["w_up"][l]), w_down=b16(a["w_down"][l]),
    )


def kernel(x_prompt, x_sample, cache_k_a, cache_v_a, state_conv_b, state_conv_c, cache_mem_k, cache_mem_v,
           page_table, mem_prompt, norm_mix_g, w_in, a_qnorm_g, a_knorm_g, a_lam, a_subln_g, w_a_out,
           b_conv_w, b_conv_b, w_b_out, c_conv_w, c_conv_b, c_ln_g, c_ln_b, w_c_out, d_ln_g, d_ln_b,
           d_ws, d_bs, w_d_out, w_o, norm_x_g, mem_norm_g, w_xq, w_xk, w_xv, x_qnorm_g, x_knorm_g,
           w_xo, norm_ffn_g, w_up, w_down):
    weights = dict(norm_mix_g=norm_mix_g, w_in=w_in, a_qnorm_g=a_qnorm_g, a_knorm_g=a_knorm_g, a_lam=a_lam,
                   a_subln_g=a_subln_g, w_a_out=w_a_out, b_conv_w=b_conv_w, b_conv_b=b_conv_b,
                   w_b_out=w_b_out, c_conv_w=c_conv_w, c_conv_b=c_conv_b, c_ln_g=c_ln_g, c_ln_b=c_ln_b,
                   w_c_out=w_c_out, d_ln_g=d_ln_g, d_ln_b=d_ln_b, d_ws=d_ws, d_bs=d_bs, w_d_out=w_d_out,
                   w_o=w_o, norm_x_g=norm_x_g, mem_norm_g=mem_norm_g, w_xq=w_xq, w_xk=w_xk, w_xv=w_xv,
                   x_qnorm_g=x_qnorm_g, x_knorm_g=x_knorm_g, w_xo=w_xo, norm_ffn_g=norm_ffn_g,
                   w_up=w_up, w_down=w_down)
    depth = w_in.shape[0]
    bp, t, _ = x_prompt.shape
    ns = x_sample.shape[0]
    n_pool = cache_k_a.shape[1]
    n_mem = cache_mem_k.shape[2]
    cache_k = cache_k_a.reshape(depth, n_pool, _PAGE_ROWS, A_VD)
    cache_v = cache_v_a.reshape(depth, n_pool, _PAGE_ROWS, A_VD)
    mem_k_s = cache_mem_k.reshape(depth, ns, n_mem * X_HEADS, X_HD)
    mem_v_s = cache_mem_v.reshape(depth, ns, n_mem * X_HEADS, X_HD)
    slopes = jnp.asarray([2.0 ** (-8.0 * (i + 1) / A_HEADS) for i in range(A_HEADS)], F32)

    xp = x_prompt
    xs = x_sample.reshape(ns, D_MODEL)
    outs = [[] for _ in range(9)]
    for l in range(depth):
        p = _layer_params(l, weights)
        lam_init = 0.8 - 0.6 * math.exp(-0.3 * l)

        mk, mv, mkb, mvb = _memory_kv(mem_prompt, p["mem_norm_g"], p["w_xk"], p["w_xv"], p["x_knorm_g"])
        k_p, v_p, qb, kb, vb = _qkv_proj(xp, p["norm_mix_g"], p["w_in_a"], p["a_qnorm_g"], p["a_knorm_g"],
                                         l, depth, None if l == 0 else (k_p, v_p))
        o_p = _diff_attention(slopes, qb, kb, vb, p["a_lam"], p["a_subln_g"], lam_init)
        xp, cb_p, cc_p = _mixer(xp, o_p, p)
        xp = _xattn_ffn(xp, mkb, mvb, p)

        sb = state_conv_b[l].reshape(ns, (B_K - 1) * B_W)
        sc = state_conv_c[l].reshape(ns, (C_K - 1) * C_W)
        q_s, k_s, v_s, cb_s, cc_s, dvn_s, part, g0 = _sample_mixer(xs, sb, sc, p)
        o_s = _decode_attention(page_table, q_s, k_s, v_s, cache_k, cache_v, l, p["a_lam"],
                                p["a_subln_g"], lam_init)
        xs = _sample_tail(xs, o_s.reshape(ns, A_W), g0, part, mem_k_s, mem_v_s, l, p)

        for lst, val in zip(outs, (
                cb_p, cc_p, mk, mv,
                k_s.reshape(ns, 1, A_HEADS, 2 * A_HD), v_s.reshape(ns, 1, A_HEADS, A_VD),
                cb_s.reshape(ns, B_K - 1, B_W), cc_s.reshape(ns, C_K - 1, C_W),
                dvn_s.reshape(ns, 1, D_W))):
            lst.append(val)
    return (xp, xs.reshape(ns, 1, D_MODEL), k_p, v_p) + tuple(jnp.stack(o) for o in outs)
```

```python
import functools
import math

import jax
import jax.numpy as jnp
from jax import lax
from jax.experimental import pallas as pl
from jax.experimental.pallas import tpu as pltpu

F32 = jnp.float32
BF16 = jnp.bfloat16

D_MODEL = 1024
A_HEADS = 4
A_HD = 64
A_VD = 2 * A_HD
A_W = A_HEADS * 2 * A_HD
B_W = 256
B_K = 3
C_W = 256
C_K = 31
D_W = 256
D_GROUPS = 4
CHUNK = 128
N_BRANCH = 4
X_HEADS = 4
X_HD = 128
X_W = X_HEADS * X_HD
D_FF = 4 * D_MODEL
PAGE_SIZE = 128
EPS = 1e-6
NEG_INF = -1e30
LOG2E = math.log2(math.e)

_QKV_COLS = 3 * A_W
_R_BX = _QKV_COLS
_R_CA = _R_BX + 3 * B_W
_R_DU = _R_CA + 2 * C_W
_R_GZ = _R_DU + 2 * D_W
IN_COLS = _R_GZ + N_BRANCH * D_MODEL

_V7X_VMEM_BYTES = 64 * 1024 * 1024
_VMEM_LIMIT = _V7X_VMEM_BYTES - 8 * 1024 * 1024

_TM = 512
_TQ = 512
_TK = 512
_ATT_HEADS = 4
_SUBLANES = 8
_CONV_ROWS = 64
_B_HALO = 8
_C_HALO = 32
_FF_CHUNK = 1024
_DEC_PAGES = 32


def _rms(x, g):
    return x * lax.rsqrt(jnp.mean(x * x, axis=-1, keepdims=True) + EPS) * g


def _layer_norm(x, g, b):
    xc = x - jnp.mean(x, axis=-1, keepdims=True)
    var = jnp.mean(xc * xc, axis=-1, keepdims=True)
    return xc * lax.rsqrt(var + EPS) * g + b


def _sigmoid(x):
    return 0.5 * jnp.tanh(0.5 * x) + 0.5


def _dot(a, b):
    return jnp.dot(a, b, preferred_element_type=F32)


def _dot_nt(a, b):
    return lax.dot_general(a, b, (((1,), (1,)), ((), ())), preferred_element_type=F32)


def _idiv(x, d):
    assert d & (d - 1) == 0
    return lax.shift_right_logical(x, int(math.log2(d)))


def _group_mean_matrix(width, group):
    r = _idiv(lax.broadcasted_iota(jnp.int32, (width, width), 0), group)
    c = _idiv(lax.broadcasted_iota(jnp.int32, (width, width), 1), group)
    return jnp.where(r == c, 1.0 / group, 0.0).astype(BF16)


def _group_rms(t, g, gm):
    ms = _dot((t * t).astype(BF16), gm)
    return t * lax.rsqrt(ms + EPS) * g


def _lam(lam_ref, lam_init):
    a = lam_ref[...]
    s1 = jnp.sum(a[0:1] * a[1:2], axis=-1, keepdims=True)
    s2 = jnp.sum(a[2:3] * a[3:4], axis=-1, keepdims=True)
    return jnp.exp(s1) - jnp.exp(s2) + lam_init


def _full(shape):
    return pl.BlockSpec(shape, lambda *_: (0,) * len(shape))


def _layer(arr, l, cols=None):
    shape = arr.shape[1:] if cols is None else arr.shape[1:-1] + (cols,)
    return pl.BlockSpec((None,) + shape, lambda *_: (l,) + (0,) * len(shape))


def _params(sem):
    return pltpu.CompilerParams(dimension_semantics=sem, vmem_limit_bytes=_VMEM_LIMIT)


def _memkv_kernel(mem_ref, g_ref, wk_ref, wv_ref, kg_ref, mk_ref, mv_ref, mkb_ref, mvb_ref):
    h = _rms(mem_ref[...], g_ref[...]).astype(BF16)
    k = _dot(h, wk_ref[...])
    v = _dot(h, wv_ref[...])
    kg = kg_ref[...]
    k = jnp.concatenate([_rms(k[:, i * X_HD:(i + 1) * X_HD], kg) for i in range(X_HEADS)], axis=1)
    mk_ref[...] = k.reshape(mk_ref.shape)
    mv_ref[...] = v.reshape(mv_ref.shape)
    mkb_ref[...] = k.astype(BF16)
    mvb_ref[...] = v.astype(BF16)


def _memory_kv(mem, p, l):
    b, n, _ = mem.shape
    ws = [p["mem_norm_g"], p["w_xk"], p["w_xv"], p["x_knorm_g"]]
    blk = pl.BlockSpec((None, n, X_W), lambda i: (i, 0, 0))
    hblk = pl.BlockSpec((None, n, X_HEADS, X_HD), lambda i: (i, 0, 0, 0))
    return pl.pallas_call(
        _memkv_kernel,
        grid=(b,),
        in_specs=[pl.BlockSpec((None, n, D_MODEL), lambda i: (i, 0, 0))] + [_layer(w, l) for w in ws],
        out_specs=[hblk, hblk, blk, blk],
        out_shape=[jax.ShapeDtypeStruct((b, n, X_HEADS, X_HD), F32),
                   jax.ShapeDtypeStruct((b, n, X_HEADS, X_HD), F32),
                   jax.ShapeDtypeStruct((b, n, X_W), BF16), jax.ShapeDtypeStruct((b, n, X_W), BF16)],
        compiler_params=_params(("arbitrary",)),
        name="memory_kv",
    )(mem, *ws)


def _qkv_from_h(h, w, qg, kg):
    z = _dot(h, w)
    gm = _group_mean_matrix(A_W, A_HD)
    q = _group_rms(z[:, :A_W], qg, gm) * (A_HD ** -0.5)
    k = _group_rms(z[:, A_W:2 * A_W], kg, gm)
    v = z[:, 2 * A_W:]
    return q, k, v


def _qkv_kernel(x_ref, g_ref, w_ref, qg_ref, kg_ref, *rest):
    k_ref, v_ref, qb_ref, kb_ref, vb_ref = rest[-5:]
    h = _rms(x_ref[...], g_ref[...]).astype(BF16)
    q, k, v = _qkv_from_h(h, w_ref[...], qg_ref[...], kg_ref[...])
    k_ref[...] = k.reshape(k_ref.shape)
    v_ref[...] = v.reshape(v_ref.shape)
    qb_ref[...] = (q * LOG2E).astype(BF16)
    kb_ref[...] = k.astype(BF16)
    vb_ref[...] = v.astype(BF16)


def _qkv_proj(x, p, layer, depth, stacks):
    b, t, _ = x.shape
    blk = pl.BlockSpec((None, _TM, A_W), lambda i, j: (i, j, 0))
    hblk = pl.BlockSpec((None, None, _TM, A_HEADS, A_VD), lambda i, j: (layer, i, j, 0, 0))
    f32 = jax.ShapeDtypeStruct((depth, b, t, A_HEADS, A_VD), F32)
    b16 = jax.ShapeDtypeStruct((b, t, A_W), BF16)
    ins = [x, p["norm_mix_g"], p["w_in"], p["a_qnorm_g"], p["a_knorm_g"]]
    in_specs = [pl.BlockSpec((None, _TM, D_MODEL), lambda i, j: (i, j, 0)), _layer(ins[1], layer),
                _layer(ins[2], layer, cols=_QKV_COLS), _layer(ins[3], layer), _layer(ins[4], layer)]
    aliases = {}
    if stacks is not None:
        aliases = {len(ins): 0, len(ins) + 1: 1}
        ins += list(stacks)
        in_specs += [pl.BlockSpec(memory_space=pl.ANY)] * 2
    return pl.pallas_call(
        _qkv_kernel,
        grid=(b, t // _TM),
        in_specs=in_specs,
        out_specs=[hblk, hblk, blk, blk, blk],
        out_shape=[f32, f32, b16, b16, b16],
        input_output_aliases=aliases,
        compiler_params=_params(("arbitrary", "arbitrary")),
        name="qkv_proj",
    )(*ins)


def _diff_attn_kernel(slope_ref, q_ref, k_ref, v_ref, lam_ref, sg_ref, o_ref,
                      qs_scr, m_scr, l_scr, acc_scr, *, lam_init):
    hg = pl.program_id(1)
    q0 = pl.program_id(2) * _TQ
    rows = 2 * _TQ

    for hh in range(_ATT_HEADS):
        q = q_ref[:, hh * A_VD:(hh + 1) * A_VD]
        lane = lax.broadcasted_iota(jnp.int32, q.shape, 1)
        zero = jnp.zeros_like(q)
        qs_scr[hh, 0:_TQ, :] = jnp.where(lane < A_HD, q, zero)
        qs_scr[hh, _TQ:rows, :] = jnp.where(lane >= A_HD, q, zero)
    m_scr[...] = jnp.full(m_scr.shape, NEG_INF, F32)
    l_scr[...] = jnp.zeros(l_scr.shape, F32)
    acc_scr[...] = jnp.zeros(acc_scr.shape, F32)

    def chunk(j, hh, masked):
        k0 = pl.multiple_of(j * _TK, _TK)
        head = slice(hh * A_VD, (hh + 1) * A_VD)
        s = _dot_nt(qs_scr[hh], k_ref[pl.ds(k0, _TK), head])
        kpos = k0 + lax.broadcasted_iota(jnp.int32, (1, _TK), 1)
        s = s + (slope_ref[hg * _ATT_HEADS + hh] * LOG2E) * kpos.astype(F32)
        if masked:
            row = lax.broadcasted_iota(jnp.int32, (rows, _TK), 0)
            qpos = q0 + jnp.where(row >= _TQ, row - _TQ, row)
            col = k0 + lax.broadcasted_iota(jnp.int32, (rows, _TK), 1)
            s = jnp.where(qpos >= col, s, NEG_INF)
        m_old = m_scr[hh]
        m_new = jnp.maximum(m_old, jnp.max(s, axis=-1, keepdims=True))
        alpha = jnp.exp2(m_old - m_new)
        p = jnp.exp2(s - jnp.concatenate([m_new] * (_TK // A_VD), axis=1))
        l_scr[hh] = alpha * l_scr[hh] + jnp.sum(p, axis=-1, keepdims=True)
        acc_scr[hh] = alpha * acc_scr[hh] + _dot(p.astype(BF16), v_ref[pl.ds(k0, _TK), head])
        m_scr[hh] = m_new

    n_full = q0 // _TK

    def body(j, carry):
        for hh in range(_ATT_HEADS):
            chunk(j, hh, False)
        return carry

    lax.fori_loop(0, n_full, body, 0)
    for d in range(_TQ // _TK):
        for hh in range(_ATT_HEADS):
            chunk(n_full + d, hh, True)

    lam = _lam(lam_ref, lam_init)
    for hh in range(_ATT_HEADS):
        o = acc_scr[hh] / l_scr[hh]
        o = o[0:_TQ] - lam * o[_TQ:rows]
        o_ref[:, hh * A_VD:(hh + 1) * A_VD] = (_rms(o, sg_ref[...]) * (1.0 - lam_init)).astype(BF16)


def _diff_attention(slopes, qb, kb, vb, p, l, lam_init):
    b, t, _ = qb.shape
    assert _TQ % _TK == 0 and t % _TQ == 0 and A_HEADS % _ATT_HEADS == 0
    width = _ATT_HEADS * A_VD
    qblk = pl.BlockSpec((None, _TQ, width), lambda bi, h, qi: (bi, qi, h))
    kvblk = pl.BlockSpec((None, t, width), lambda bi, h, qi: (bi, 0, h))
    stat = pltpu.VMEM((_ATT_HEADS, 2 * _TQ, A_VD), F32)
    return pl.pallas_call(
        functools.partial(_diff_attn_kernel, lam_init=lam_init),
        grid=(b, A_HEADS // _ATT_HEADS, t // _TQ),
        in_specs=[pl.BlockSpec(memory_space=pltpu.SMEM), qblk, kvblk, kvblk,
                  _layer(p["a_lam"], l), _layer(p["a_subln_g"], l)],
        out_specs=qblk,
        out_shape=jax.ShapeDtypeStruct((b, t, A_W), BF16),
        scratch_shapes=[pltpu.VMEM((_ATT_HEADS, 2 * _TQ, A_VD), BF16), stat, stat, stat],
        compiler_params=_params(("arbitrary",) * 3),
        name="diff_attention",
    )(slopes, qb, kb, vb, p["a_lam"], p["a_subln_g"])


def _lane_group_mask(width, group, g):
    lane = lax.broadcasted_iota(jnp.int32, (1, width), 1)
    return (_idiv(lane, group) == g).astype(F32)


def _mixer_kernel(x_ref, o_ref, g_ref, wr_ref, wa_ref, wb_ref, wc_ref, wd_ref, wo_ref,
                  bcw_ref, bcb_ref, ccw_ref, ccb_ref, clg_ref, clb_ref, dlg_ref, dlb_ref,
                  dws_ref, dbst_ref, x1_ref, cbp_ref, ccp_ref, ubuf, cbuf, ycin, shwin, gates):
    tm = _TM

    @pl.when(pl.program_id(1) == 0)
    def _():
        ubuf[0:_B_HALO, :] = jnp.zeros((_B_HALO, B_W), F32)
        cbuf[0:_C_HALO, :] = jnp.zeros((_C_HALO, C_W), F32)

    x = x_ref[...]
    h = _rms(x, g_ref[...]).astype(BF16)

    zc = _dot(h, wr_ref[:, _R_CA:_R_DU])
    cbuf[_C_HALO:_C_HALO + tm, :] = zc[:, 0:256] * _sigmoid(zc[:, 256:512])
    base = _C_HALO - (C_K - 1)
    gcols = N_BRANCH * D_MODEL // (tm // _CONV_ROWS)
    for it, r in enumerate(range(0, tm, _CONV_ROWS)):
        c0 = _R_GZ + it * gcols
        gates[:, it * gcols:(it + 1) * gcols] = _sigmoid(_dot(h, wr_ref[:, c0:c0 + gcols]))
        acc = jnp.broadcast_to(ccb_ref[...], (_CONV_ROWS, C_W))
        for ph in range(_SUBLANES):
            n_taps = (C_K - 1 - ph) // _SUBLANES + 1
            rows = _CONV_ROWS + _SUBLANES * (n_taps - 1)
            shwin[ph, 0:rows, :] = cbuf[base + r + ph:base + r + ph + rows, :]
            for i in range(n_taps):
                k = ph + _SUBLANES * i
                acc = acc + ccw_ref[k:k + 1, :] * shwin[ph, _SUBLANES * i:_SUBLANES * i + _CONV_ROWS, :]
        y = _layer_norm(acc, clg_ref[...], clb_ref[...])
        ycin[r:r + _CONV_ROWS, :] = (y * _sigmoid(y)).astype(BF16)
    ccp_ref[...] = cbuf[_C_HALO + tm - (C_K - 1):_C_HALO + tm, :]
    cbuf[0:_C_HALO, :] = cbuf[tm:tm + _C_HALO, :]

    def gate(i):
        return gates[:, i * D_MODEL:(i + 1) * D_MODEL]

    merged = gate(2) * _dot(ycin[...], wc_ref[...])

    merged = merged + gate(0) * _dot(o_ref[...], wa_ref[...])

    zb = _dot(h, wr_ref[:, _R_BX:_R_CA])
    u = zb[:, 512:768] * zb[:, 0:256]
    ubuf[_B_HALO:_B_HALO + tm, :] = u
    conv_b = (bcw_ref[0:1, :] * ubuf[_B_HALO - 2:_B_HALO - 2 + tm, :]
              + bcw_ref[1:2, :] * ubuf[_B_HALO - 1:_B_HALO - 1 + tm, :]
              + bcw_ref[2:3, :] * u + bcb_ref[...])
    yb = _dot((zb[:, 256:512] * conv_b).astype(BF16), wb_ref[...])
    merged = merged + gate(1) * yb
    cbp_ref[...] = ubuf[_B_HALO + tm - (B_K - 1):_B_HALO + tm, :]
    ubuf[0:_B_HALO, :] = ubuf[tm:tm + _B_HALO, :]

    zd = jax.nn.gelu(_dot(h, wr_ref[:, _R_DU:_R_GZ]))
    du = zd[:, 0:256]
    dvn = _layer_norm(zd[:, 256:512], dlg_ref[...], dlb_ref[...])
    row = lax.broadcasted_iota(jnp.int32, (CHUNK, CHUNK), 0)
    col = lax.broadcasted_iota(jnp.int32, (CHUNK, CHUNK), 1)
    tril = (row >= col).astype(F32)
    wcat = jnp.concatenate([dws_ref[g] * tril for g in range(D_GROUPS)], axis=1).astype(BF16)
    masks = [_lane_group_mask(D_W, D_W // D_GROUPS, g) for g in range(D_GROUPS)]
    bsmat = sum(dbst_ref[:, g:g + 1] * masks[g] for g in range(D_GROUPS))
    for c in range(0, tm, CHUNK):
        vch = dvn[c:c + CHUNK]
        rhs = jnp.concatenate([(vch * masks[g]).astype(BF16) for g in range(D_GROUPS)], axis=0)
        s = _dot(wcat, rhs) + bsmat
        ycin[c:c + CHUNK, :] = (du[c:c + CHUNK] * s).astype(BF16)
    yd = _dot(ycin[...], wd_ref[...])
    merged = merged + gate(3) * yd

    x1_ref[...] = x + _dot(merged.astype(BF16), wo_ref[...])


def _mixer(x, o, p, l):
    b, t, _ = x.shape
    ws = [p[k] for k in ("norm_mix_g", "w_in", "w_a_out", "w_b_out", "w_c_out", "w_d_out", "w_o", "b_conv_w",
                         "b_conv_b", "c_conv_w", "c_conv_b", "c_ln_g", "c_ln_b", "d_ln_g", "d_ln_b", "d_ws",
                         "d_bs_t")]
    row = lambda w: pl.BlockSpec((None, _TM, w), lambda i, j: (i, j, 0))
    return pl.pallas_call(
        _mixer_kernel,
        grid=(b, t // _TM),
        in_specs=[row(D_MODEL), row(A_W)] + [_layer(w, l) for w in ws],
        out_specs=[row(D_MODEL),
                   pl.BlockSpec((None, B_K - 1, B_W), lambda i, j: (i, 0, 0)),
                   pl.BlockSpec((None, C_K - 1, C_W), lambda i, j: (i, 0, 0))],
        out_shape=[jax.ShapeDtypeStruct((b, t, D_MODEL), F32),
                   jax.ShapeDtypeStruct((b, B_K - 1, B_W), F32),
                   jax.ShapeDtypeStruct((b, C_K - 1, C_W), F32)],
        scratch_shapes=[pltpu.VMEM((_B_HALO + _TM, B_W), F32), pltpu.VMEM((_C_HALO + _TM, C_W), F32),
                        pltpu.VMEM((_TM, C_W), BF16),
                        pltpu.VMEM((_SUBLANES, _CONV_ROWS + _C_HALO, C_W), F32),
                        pltpu.VMEM((_TM, N_BRANCH * D_MODEL), F32)],
        compiler_params=_params(("arbitrary", "arbitrary")),
        name="mixer",
    )(x, o, *ws)


def _ffn(x2, gf, wup_ref, wdn_ref):
    h3 = _rms(x2, gf).astype(BF16)
    acc = jnp.zeros(x2.shape, F32)
    for c in range(0, D_FF, _FF_CHUNK):
        a = jnp.maximum(_dot(h3, wup_ref[:, c:c + _FF_CHUNK]), 0.0)
        acc = acc + _dot((a * a).astype(BF16), wdn_ref[c:c + _FF_CHUNK, :])
    return x2 + acc


def _cross_attention(x, mk_ref, mv_ref, gx, wxq_ref, qg, wxo_ref):
    h2 = _rms(x, gx).astype(BF16)
    q = _dot(h2, wxq_ref[...])
    oms = []
    for i in range(X_HEADS):
        sl = slice(i * X_HD, (i + 1) * X_HD)
        qh = (_rms(q[:, sl], qg) * (X_HD ** -0.5)).astype(BF16)
        s = _dot_nt(qh, mk_ref[:, sl])
        p = jnp.exp(s - jnp.max(s, axis=-1, keepdims=True))
        oms.append(_dot(p.astype(BF16), mv_ref[:, sl]) / jnp.sum(p, axis=-1, keepdims=True))
    om = jnp.concatenate(oms, axis=1).astype(BF16)
    return x + _dot(om, wxo_ref[...])


def _xffn_kernel(x_ref, mk_ref, mv_ref, gx_ref, wxq_ref, qg_ref, wxo_ref, gf_ref, wup_ref, wdn_ref,
                 out_ref):
    x2 = _cross_attention(x_ref[...], mk_ref, mv_ref, gx_ref[...], wxq_ref, qg_ref[...], wxo_ref)
    out_ref[...] = _ffn(x2, gf_ref[...], wup_ref, wdn_ref)


def _xattn_ffn(x, mkb, mvb, p, l):
    b, t, _ = x.shape
    ws = [p[k] for k in ("norm_x_g", "w_xq", "x_qnorm_g", "w_xo", "norm_ffn_g", "w_up", "w_down")]
    n = mkb.shape[1]
    row = pl.BlockSpec((None, _TM, D_MODEL), lambda i, j: (i, j, 0))
    mem = pl.BlockSpec((None, n, X_W), lambda i, j: (i, 0, 0))
    return pl.pallas_call(
        _xffn_kernel,
        grid=(b, t // _TM),
        in_specs=[row, mem, mem] + [_layer(w, l) for w in ws],
        out_specs=row,
        out_shape=jax.ShapeDtypeStruct((b, t, D_MODEL), F32),
        compiler_params=_params(("arbitrary", "arbitrary")),
        name="xattn_ffn",
    )(x, mkb, mvb, *ws)


def _sample_mixer_kernel(x_ref, sb_ref, sc_ref, g_ref, wr_ref, qg_ref, kg_ref,
                         wb_ref, wc_ref, wd_ref, bcw_ref, bcb_ref, ccw_ref, ccb_ref, clg_ref, clb_ref,
                         dlg_ref, dlb_ref, dws_ref, dbs_ref,
                         q_ref, k_ref, v_ref, cb_ref, cc_ref, dvn_ref, part_ref, g0_ref):
    x = x_ref[...]
    h = _rms(x, g_ref[...]).astype(BF16)
    q, k, v = _qkv_from_h(h, wr_ref[:, 0:_QKV_COLS], qg_ref[...], kg_ref[...])
    q_ref[...] = q
    k_ref[...] = k
    v_ref[...] = v

    def gate(i):
        return _sigmoid(_dot(h, wr_ref[:, _R_GZ + i * D_MODEL:_R_GZ + (i + 1) * D_MODEL]))

    g0_ref[...] = gate(0)

    zb = _dot(h, wr_ref[:, _R_BX:_R_CA])
    u = zb[:, 512:768] * zb[:, 0:256]
    conv_b = (bcw_ref[0:1, :] * sb_ref[:, 0:B_W] + bcw_ref[1:2, :] * sb_ref[:, B_W:2 * B_W]
              + bcw_ref[2:3, :] * u + bcb_ref[...])
    part = gate(1) * _dot((zb[:, 256:512] * conv_b).astype(BF16), wb_ref[...])
    cb_ref[:, 0:B_W] = sb_ref[:, B_W:2 * B_W]
    cb_ref[:, B_W:2 * B_W] = u

    zc = _dot(h, wr_ref[:, _R_CA:_R_DU])
    uc = zc[:, 0:256] * _sigmoid(zc[:, 256:512])
    acc = ccw_ref[C_K - 1:C_K, :] * uc + ccb_ref[...]
    for kk in range(C_K - 1):
        acc = acc + ccw_ref[kk:kk + 1, :] * sc_ref[:, kk * C_W:(kk + 1) * C_W]
    y = _layer_norm(acc, clg_ref[...], clb_ref[...])
    yc = _dot((y * _sigmoid(y)).astype(BF16), wc_ref[...])
    part = part + gate(2) * yc
    cc_ref[:, 0:(C_K - 2) * C_W] = sc_ref[:, C_W:(C_K - 1) * C_W]
    cc_ref[:, (C_K - 2) * C_W:(C_K - 1) * C_W] = uc

    zd = jax.nn.gelu(_dot(h, wr_ref[:, _R_DU:_R_GZ]))
    dvn = _layer_norm(zd[:, 256:512], dlg_ref[...], dlb_ref[...])
    dvn_ref[...] = dvn
    w00 = sum(dws_ref[g, 0:1, 0:1] * _lane_group_mask(D_W, D_W // D_GROUPS, g) for g in range(D_GROUPS))
    b0 = sum(dbs_ref[g:g + 1, 0:1] * _lane_group_mask(D_W, D_W // D_GROUPS, g) for g in range(D_GROUPS))
    yd = _dot((zd[:, 0:256] * (w00 * dvn + b0)).astype(BF16), wd_ref[...])
    part_ref[...] = part + gate(3) * yd


def _sample_mixer(x, sb, sc, p, l):
    n = x.shape[0]
    shapes = [(n, A_W), (n, A_W), (n, A_W), (n, (B_K - 1) * B_W), (n, (C_K - 1) * C_W), (n, D_W),
              (n, D_MODEL), (n, D_MODEL)]
    ins = [x, sb, sc]
    ws = [p[k] for k in ("norm_mix_g", "w_in", "a_qnorm_g", "a_knorm_g", "w_b_out", "w_c_out", "w_d_out",
                         "b_conv_w", "b_conv_b", "c_conv_w", "c_conv_b", "c_ln_g", "c_ln_b", "d_ln_g", "d_ln_b",
                         "d_ws", "d_bs")]
    return pl.pallas_call(
        _sample_mixer_kernel,
        grid=(1,),
        in_specs=[_full(a.shape) for a in ins] + [_layer(w, l) for w in ws],
        out_specs=[_full(s) for s in shapes],
        out_shape=[jax.ShapeDtypeStruct(s, F32) for s in shapes],
        compiler_params=_params(("arbitrary",)),
        name="sample_mixer",
    )(*ins, *ws)


_DEC_ROWS = 2 * A_HEADS
_PAGE_ROWS = PAGE_SIZE * A_HEADS


def _head_rows(ref, row, per_head):
    return sum(jnp.where(_idiv(row, per_head) == h, jnp.broadcast_to(ref[h:h + 1, :], row.shape), 0.0)
               for h in range(ref.shape[0]))


def _decode_init(q_ref, qz_scr, m_scr, l_scr, acc_scr):
    row = lax.broadcasted_iota(jnp.int32, (_DEC_ROWS, A_VD), 0)
    lane = lax.broadcasted_iota(jnp.int32, (_DEC_ROWS, A_VD), 1)
    qz_scr[...] = jnp.where(_idiv(lane, A_HD) == (row & 1), _head_rows(q_ref, row, 2), 0.0)
    m_scr[...] = jnp.full(m_scr.shape, NEG_INF, F32)
    l_scr[...] = jnp.zeros(l_scr.shape, F32)
    acc_scr[...] = jnp.zeros(acc_scr.shape, F32)


def _decode_pages(j, k_refs, v_refs, qz_scr, m_scr, l_scr, acc_scr, past_len):
    n_pages = len(k_refs)
    width = n_pages * _PAGE_ROWS
    rowv = lax.broadcasted_iota(jnp.int32, (_DEC_ROWS, 1), 0)
    slope = sum(jnp.where(_idiv(rowv, 2) == i, 2.0 ** (-8.0 * (i + 1) / A_HEADS), 0.0)
                for i in range(A_HEADS))
    qz = qz_scr[...].astype(BF16)
    s = jnp.concatenate([_dot_nt(qz, k_refs[i][...].astype(BF16)) for i in range(n_pages)], axis=1)
    col = lax.broadcasted_iota(jnp.int32, (_DEC_ROWS, width), 1)
    kpos = j * (n_pages * PAGE_SIZE) + _idiv(lax.broadcasted_iota(jnp.int32, (1, width), 1), A_HEADS)
    own_head = (col & (A_HEADS - 1)) == _idiv(lax.broadcasted_iota(jnp.int32, (_DEC_ROWS, width), 0), 2)
    s = jnp.where(own_head, s - slope * (past_len - kpos).astype(F32), NEG_INF)
    m_old = m_scr[...]
    m_new = jnp.maximum(m_old, jnp.max(s, axis=-1, keepdims=True))
    alpha = jnp.exp(m_old - m_new)
    p = jnp.exp(s - m_new)
    l_scr[...] = alpha * l_scr[...] + jnp.sum(p, axis=-1, keepdims=True)
    p = p.astype(BF16)
    pv = sum(_dot(p[:, i * _PAGE_ROWS:(i + 1) * _PAGE_ROWS], v_refs[i][...].astype(BF16))
             for i in range(n_pages))
    acc_scr[...] = alpha * acc_scr[...] + pv
    m_scr[...] = m_new


def _decode_finish(kn_ref, vn_ref, lam_ref, sg_ref, o_ref, qz_scr, m_scr, l_scr, acc_scr, lam_init):
    row = lax.broadcasted_iota(jnp.int32, (_DEC_ROWS, A_VD), 0)
    s_new = jnp.sum(qz_scr[...] * _head_rows(kn_ref, row, 2), axis=-1, keepdims=True)
    m_old = m_scr[...]
    m_fin = jnp.maximum(m_old, s_new)
    alpha = jnp.exp(m_old - m_fin)
    p_new = jnp.exp(s_new - m_fin)
    l_fin = alpha * l_scr[...] + p_new
    o_all = (alpha * acc_scr[...] + p_new * _head_rows(vn_ref, row, 2)) / l_fin
    lam = _lam(lam_ref, lam_init)
    sg = sg_ref[...]
    for h in range(A_HEADS):
        o = o_all[2 * h:2 * h + 1] - lam * o_all[2 * h + 1:2 * h + 2]
        o_ref[h:h + 1, :] = _rms(o, sg) * (1.0 - lam_init)


def _decode_attn_kernel(pt_ref, q_ref, kn_ref, vn_ref, lam_ref, sg_ref, *rest, lam_init, past_len):
    del pt_ref
    k_refs = rest[:_DEC_PAGES]
    v_refs = rest[_DEC_PAGES:2 * _DEC_PAGES]
    o_ref = rest[2 * _DEC_PAGES]
    dec = rest[2 * _DEC_PAGES + 1:]
    j = pl.program_id(1)

    @pl.when(j == 0)
    def _():
        _decode_init(q_ref, *dec)

    _decode_pages(j, k_refs, v_refs, *dec, past_len)

    @pl.when(j == pl.num_programs(1) - 1)
    def _():
        _decode_finish(kn_ref, vn_ref, lam_ref, sg_ref, o_ref, *dec, lam_init)


def _decode_attention(page_table, q, k_new, v_new, cache_k, cache_v, p, layer, lam_init):
    n, n_pages = page_table.shape
    past_len = n_pages * PAGE_SIZE
    row = pl.BlockSpec((None, A_HEADS, A_VD), lambda b, j, pt: (b, 0, 0))

    def page(i):
        return pl.BlockSpec((None, None, _PAGE_ROWS, A_VD),
                            lambda b, j, pt: (layer, pt[b, j * _DEC_PAGES + i], 0, 0))

    grid_spec = pltpu.PrefetchScalarGridSpec(
        num_scalar_prefetch=1,
        grid=(n, n_pages // _DEC_PAGES),
        in_specs=[row, row, row, _layer(p["a_lam"], layer), _layer(p["a_subln_g"], layer)]
        + [page(i) for i in range(_DEC_PAGES)] + [page(i) for i in range(_DEC_PAGES)],
        out_specs=row,
        scratch_shapes=[pltpu.VMEM((_DEC_ROWS, A_VD), F32), pltpu.VMEM((_DEC_ROWS, 1), F32),
                        pltpu.VMEM((_DEC_ROWS, 1), F32), pltpu.VMEM((_DEC_ROWS, A_VD), F32)],
    )
    heads = lambda a: a.reshape(n, A_HEADS, A_VD)
    return pl.pallas_call(
        functools.partial(_decode_attn_kernel, lam_init=lam_init, past_len=past_len),
        grid_spec=grid_spec,
        out_shape=jax.ShapeDtypeStruct((n, A_HEADS, A_VD), F32),
        compiler_params=_params(("arbitrary", "arbitrary")),
        name="decode_attention",
    )(page_table, heads(q), heads(k_new), heads(v_new), p["a_lam"], p["a_subln_g"],
      *([cache_k] * _DEC_PAGES), *([cache_v] * _DEC_PAGES))


def _sample_tail_kernel(x_ref, o_ref, g0_ref, part_ref, mk_ref, mv_ref, wa_ref, wo_ref, gx_ref, wxq_ref,
                        qg_ref, wxo_ref, gf_ref, wup_ref, wdn_ref, out_ref, x1_scr, q_scr, om_scr):
    b = pl.program_id(0)
    nb = pl.num_programs(0)

    @pl.when(b == 0)
    def _():
        merged = g0_ref[...] * _dot(o_ref[...].astype(BF16), wa_ref[...]) + part_ref[...]
        x1 = x_ref[...] + _dot(merged.astype(BF16), wo_ref[...])
        x1_scr[...] = x1
        q = _dot(_rms(x1, gx_ref[...]).astype(BF16), wxq_ref[...])
        qg = qg_ref[...]
        for h in range(X_HEADS):
            q_scr[h] = _rms(q[:, h * X_HD:(h + 1) * X_HD], qg) * (X_HD ** -0.5)

    n_rows = mk_ref.shape[0]
    row = lax.broadcasted_iota(jnp.int32, (8, X_HD), 0)
    qz = sum(jnp.where(row == h, jnp.broadcast_to(q_scr[h, pl.ds(b, 1), :], (8, X_HD)), 0.0)
             for h in range(X_HEADS)).astype(BF16)
    s = _dot_nt(qz, mk_ref[...].astype(BF16))
    col = lax.broadcasted_iota(jnp.int32, (8, n_rows), 1)
    s = jnp.where((col & (X_HEADS - 1)) == lax.broadcasted_iota(jnp.int32, (8, n_rows), 0), s, NEG_INF)
    p = jnp.exp(s - jnp.max(s, axis=-1, keepdims=True))
    p = p / jnp.sum(p, axis=-1, keepdims=True)
    om = _dot(p.astype(BF16), mv_ref[...].astype(BF16))
    for h in range(X_HEADS):
        om_scr[h, pl.ds(b, 1), :] = om[h:h + 1, :]

    @pl.when(b == nb - 1)
    def _():
        x2 = x1_scr[...] + sum(_dot(om_scr[h].astype(BF16), wxo_ref[h * X_HD:(h + 1) * X_HD, :])
                               for h in range(X_HEADS))
        out_ref[...] = _ffn(x2, gf_ref[...], wup_ref, wdn_ref)


def _sample_tail(x, o, g0, part, mem_k, mem_v, layer, p):
    n = x.shape[0]
    n_rows = mem_k.shape[2]
    mem = pl.BlockSpec((None, None, n_rows, X_HD), lambda b: (layer, b, 0, 0))
    ins = [x, o, g0, part]
    ws = [p[k] for k in ("w_a_out", "w_o", "norm_x_g", "w_xq", "x_qnorm_g", "w_xo", "norm_ffn_g", "w_up",
                         "w_down")]
    return pl.pallas_call(
        _sample_tail_kernel,
        grid=(n,),
        in_specs=[_full(a.shape) for a in ins] + [mem, mem] + [_layer(w, layer) for w in ws],
        out_specs=_full((n, D_MODEL)),
        out_shape=jax.ShapeDtypeStruct((n, D_MODEL), F32),
        scratch_shapes=[pltpu.VMEM((n, D_MODEL), F32), pltpu.VMEM((X_HEADS, n, X_HD), F32),
                        pltpu.VMEM((X_HEADS, n, X_HD), F32)],
        compiler_params=_params(("arbitrary",)),
        name="sample_tail",
    )(*ins, mem_k, mem_v, *ws)


def _stacked_params(a):
    row = lambda v: v.reshape(v.shape[0], 1, -1)
    b16 = lambda w: w.astype(BF16)
    tiled = lambda v: row(jnp.tile(v, (1, A_W // A_HD)))
    out = {k: b16(a[k]) for k in ("w_in", "w_a_out", "w_b_out", "w_c_out", "w_d_out", "w_o", "w_xq", "w_xk",
                                  "w_xv", "w_xo", "w_up", "w_down")}
    out.update({k: row(a[k]) for k in ("norm_mix_g", "a_subln_g", "b_conv_b", "c_conv_b", "c_ln_g", "c_ln_b",
                                       "d_ln_g", "d_ln_b", "norm_x_g", "mem_norm_g", "x_qnorm_g", "x_knorm_g",
                                       "norm_ffn_g")})
    out.update({k: a[k] for k in ("a_lam", "b_conv_w", "c_conv_w", "d_ws", "d_bs")})
    out.update(a_qnorm_g=tiled(a["a_qnorm_g"]), a_knorm_g=tiled(a["a_knorm_g"]),
               d_bs_t=jnp.swapaxes(a["d_bs"], 1, 2))
    return out


def kernel(x_prompt, x_sample, cache_k_a, cache_v_a, state_conv_b, state_conv_c, cache_mem_k, cache_mem_v,
           page_table, mem_prompt, norm_mix_g, w_in, a_qnorm_g, a_knorm_g, a_lam, a_subln_g, w_a_out,
           b_conv_w, b_conv_b, w_b_out, c_conv_w, c_conv_b, c_ln_g, c_ln_b, w_c_out, d_ln_g, d_ln_b,
           d_ws, d_bs, w_d_out, w_o, norm_x_g, mem_norm_g, w_xq, w_xk, w_xv, x_qnorm_g, x_knorm_g,
           w_xo, norm_ffn_g, w_up, w_down):
    weights = dict(norm_mix_g=norm_mix_g, w_in=w_in, a_qnorm_g=a_qnorm_g, a_knorm_g=a_knorm_g, a_lam=a_lam,
                   a_subln_g=a_subln_g, w_a_out=w_a_out, b_conv_w=b_conv_w, b_conv_b=b_conv_b,
                   w_b_out=w_b_out, c_conv_w=c_conv_w, c_conv_b=c_conv_b, c_ln_g=c_ln_g, c_ln_b=c_ln_b,
                   w_c_out=w_c_out, d_ln_g=d_ln_g, d_ln_b=d_ln_b, d_ws=d_ws, d_bs=d_bs, w_d_out=w_d_out,
                   w_o=w_o, norm_x_g=norm_x_g, mem_norm_g=mem_norm_g, w_xq=w_xq, w_xk=w_xk, w_xv=w_xv,
                   x_qnorm_g=x_qnorm_g, x_knorm_g=x_knorm_g, w_xo=w_xo, norm_ffn_g=norm_ffn_g,
                   w_up=w_up, w_down=w_down)
    depth = w_in.shape[0]
    bp, t, _ = x_prompt.shape
    ns = x_sample.shape[0]
    n_pool = cache_k_a.shape[1]
    n_mem = cache_mem_k.shape[2]
    cache_k = cache_k_a.reshape(depth, n_pool, _PAGE_ROWS, A_VD)
    cache_v = cache_v_a.reshape(depth, n_pool, _PAGE_ROWS, A_VD)
    mem_k_s = cache_mem_k.reshape(depth, ns, n_mem * X_HEADS, X_HD)
    mem_v_s = cache_mem_v.reshape(depth, ns, n_mem * X_HEADS, X_HD)
    slopes = jnp.asarray([2.0 ** (-8.0 * (i + 1) / A_HEADS) for i in range(A_HEADS)], F32)

    xp = x_prompt
    xs = x_sample.reshape(ns, D_MODEL)
    outs = [[] for _ in range(9)]
    p = _stacked_params(weights)
    for l in range(depth):
        lam_init = 0.8 - 0.6 * math.exp(-0.3 * l)

        mk, mv, mkb, mvb = _memory_kv(mem_prompt, p, l)
        k_p, v_p, qb, kb, vb = _qkv_proj(xp, p, l, depth, None if l == 0 else (k_p, v_p))
        o_p = _diff_attention(slopes, qb, kb, vb, p, l, lam_init)
        xp, cb_p, cc_p = _mixer(xp, o_p, p, l)
        xp = _xattn_ffn(xp, mkb, mvb, p, l)

        sb = state_conv_b[l].reshape(ns, (B_K - 1) * B_W)
        sc = state_conv_c[l].reshape(ns, (C_K - 1) * C_W)
        q_s, k_s, v_s, cb_s, cc_s, dvn_s, part, g0 = _sample_mixer(xs, sb, sc, p, l)
        o_s = _decode_attention(page_table, q_s, k_s, v_s, cache_k, cache_v, p, l, lam_init)
        xs = _sample_tail(xs, o_s.reshape(ns, A_W), g0, part, mem_k_s, mem_v_s, l, p)

        for lst, val in zip(outs, (
                cb_p, cc_p, mk, mv,
                k_s.reshape(ns, 1, A_HEADS, 2 * A_HD), v_s.reshape(ns, 1, A_HEADS, A_VD),
                cb_s.reshape(ns, B_K - 1, B_W), cc_s.reshape(ns, C_K - 1, C_W),
                dvn_s.reshape(ns, 1, D_W))):
            lst.append(val)
    return (xp, xs.reshape(ns, 1, D_MODEL), k_p, v_p) + tuple(jnp.stack(o) for o in outs)
```

```python
import functools
import math

import jax
import jax.numpy as jnp
from jax import lax
from jax.experimental import pallas as pl
from jax.experimental.pallas import tpu as pltpu

F32 = jnp.float32
BF16 = jnp.bfloat16

D_MODEL = 1024
A_HEADS = 4
A_HD = 64
A_VD = 2 * A_HD
A_W = A_HEADS * 2 * A_HD
B_W = 256
B_K = 3
C_W = 256
C_K = 31
D_W = 256
D_GROUPS = 4
CHUNK = 128
N_BRANCH = 4
X_HEADS = 4
X_HD = 128
X_W = X_HEADS * X_HD
D_FF = 4 * D_MODEL
PAGE_SIZE = 128
EPS = 1e-6
NEG_INF = -1e30
LOG2E = math.log2(math.e)

_QKV_COLS = 3 * A_W
_R_BX = _QKV_COLS
_R_CA = _R_BX + 3 * B_W
_R_DU = _R_CA + 2 * C_W
_R_GZ = _R_DU + 2 * D_W
IN_COLS = _R_GZ + N_BRANCH * D_MODEL

_V7X_VMEM_BYTES = 64 * 1024 * 1024
_VMEM_LIMIT = _V7X_VMEM_BYTES - 8 * 1024 * 1024

_TM = 512
_TQ = 512
_TK = 512
_ATT_HEADS = 4
_SUBLANES = 8
_MXU_TILE = 256
_CONV_ROWS = 64
_B_HALO = 8
_C_HALO = 32
_FF_CHUNK = 1024
_DEC_PAGES = 32
_TAIL_SEQS = 4


def _rms(x, g):
    return x * lax.rsqrt(jnp.mean(x * x, axis=-1, keepdims=True) + EPS) * g


def _layer_norm(x, g, b):
    xc = x - jnp.mean(x, axis=-1, keepdims=True)
    var = jnp.mean(xc * xc, axis=-1, keepdims=True)
    return xc * lax.rsqrt(var + EPS) * g + b


def _sigmoid(x):
    return 0.5 * jnp.tanh(0.5 * x) + 0.5


def _dot(a, b):
    return jnp.dot(a, b, preferred_element_type=F32)


def _dot_nt(a, b):
    return lax.dot_general(a, b, (((1,), (1,)), ((), ())), preferred_element_type=F32)


def _idiv(x, d):
    assert d & (d - 1) == 0
    return lax.shift_right_logical(x, int(math.log2(d)))


def _group_mean_matrix(width, group):
    r = _idiv(lax.broadcasted_iota(jnp.int32, (width, width), 0), group)
    c = _idiv(lax.broadcasted_iota(jnp.int32, (width, width), 1), group)
    return jnp.where(r == c, 1.0 / group, 0.0).astype(BF16)


def _group_rms(t, g, gm):
    w = gm.shape[0]
    sq = (t * t).astype(BF16)
    ms = jnp.concatenate([_dot(sq[:, c:c + w], gm) for c in range(0, t.shape[1], w)], axis=1)
    return t * lax.rsqrt(ms + EPS) * g


def _lam(lam_ref, lam_init):
    a = lam_ref[...]
    s1 = jnp.sum(a[0:1] * a[1:2], axis=-1, keepdims=True)
    s2 = jnp.sum(a[2:3] * a[3:4], axis=-1, keepdims=True)
    return jnp.exp(s1) - jnp.exp(s2) + lam_init


def _full(shape):
    return pl.BlockSpec(shape, lambda *_: (0,) * len(shape))


def _layer(arr, l, cols=None):
    shape = arr.shape[1:] if cols is None else arr.shape[1:-1] + (cols,)
    return pl.BlockSpec((None,) + shape, lambda *_: (l,) + (0,) * len(shape))


def _params(sem):
    return pltpu.CompilerParams(dimension_semantics=sem, vmem_limit_bytes=_VMEM_LIMIT)


def _memkv_kernel(mem_ref, g_ref, wk_ref, wv_ref, kg_ref, mk_ref, mv_ref, mkb_ref, mvb_ref):
    h = _rms(mem_ref[...], g_ref[...]).astype(BF16)
    k = _dot(h, wk_ref[...])
    v = _dot(h, wv_ref[...])
    kg = kg_ref[...]
    k = jnp.concatenate([_rms(k[:, i * X_HD:(i + 1) * X_HD], kg) for i in range(X_HEADS)], axis=1)
    mk_ref[...] = k.reshape(mk_ref.shape)
    mv_ref[...] = v.reshape(mv_ref.shape)
    mkb_ref[...] = k.astype(BF16)
    mvb_ref[...] = v.astype(BF16)


def _memory_kv(mem, p, l):
    b, n, _ = mem.shape
    ws = [p["mem_norm_g"], p["w_xk"], p["w_xv"], p["x_knorm_g"]]
    blk = pl.BlockSpec((None, n, X_W), lambda i: (i, 0, 0))
    hblk = pl.BlockSpec((None, n, X_HEADS, X_HD), lambda i: (i, 0, 0, 0))
    return pl.pallas_call(
        _memkv_kernel,
        grid=(b,),
        in_specs=[pl.BlockSpec((None, n, D_MODEL), lambda i: (i, 0, 0))] + [_layer(w, l) for w in ws],
        out_specs=[hblk, hblk, blk, blk],
        out_shape=[jax.ShapeDtypeStruct((b, n, X_HEADS, X_HD), F32),
                   jax.ShapeDtypeStruct((b, n, X_HEADS, X_HD), F32),
                   jax.ShapeDtypeStruct((b, n, X_W), BF16), jax.ShapeDtypeStruct((b, n, X_W), BF16)],
        compiler_params=_params(("arbitrary",)),
        name="memory_kv",
    )(mem, *ws)


def _qkv_from_h(h, w, qg, kg):
    z = _dot(h, w)
    gm = _group_mean_matrix(_MXU_TILE, A_HD)
    q = _group_rms(z[:, :A_W], qg, gm) * (A_HD ** -0.5)
    k = _group_rms(z[:, A_W:2 * A_W], kg, gm)
    v = z[:, 2 * A_W:]
    return q, k, v


def _qkv_kernel(x_ref, g_ref, w_ref, qg_ref, kg_ref, *rest):
    k_ref, v_ref, qb_ref, kb_ref, vb_ref = rest[-5:]
    h = _rms(x_ref[...], g_ref[...]).astype(BF16)
    q, k, v = _qkv_from_h(h, w_ref[...], qg_ref[...], kg_ref[...])
    k_ref[...] = k.reshape(k_ref.shape)
    v_ref[...] = v.reshape(v_ref.shape)
    qb_ref[...] = (q * LOG2E).astype(BF16)
    kb_ref[...] = k.astype(BF16)
    vb_ref[...] = v.astype(BF16)


def _qkv_proj(x, p, layer, depth, stacks):
    b, t, _ = x.shape
    blk = pl.BlockSpec((None, _TM, A_W), lambda i, j: (i, j, 0))
    hblk = pl.BlockSpec((None, None, _TM, A_HEADS, A_VD), lambda i, j: (layer, i, j, 0, 0))
    f32 = jax.ShapeDtypeStruct((depth, b, t, A_HEADS, A_VD), F32)
    b16 = jax.ShapeDtypeStruct((b, t, A_W), BF16)
    ins = [x, p["norm_mix_g"], p["w_in"], p["a_qnorm_g"], p["a_knorm_g"]]
    in_specs = [pl.BlockSpec((None, _TM, D_MODEL), lambda i, j: (i, j, 0)), _layer(ins[1], layer),
                _layer(ins[2], layer, cols=_QKV_COLS), _layer(ins[3], layer), _layer(ins[4], layer)]
    aliases = {}
    if stacks is not None:
        aliases = {len(ins): 0, len(ins) + 1: 1}
        ins += list(stacks)
        in_specs += [pl.BlockSpec(memory_space=pl.ANY)] * 2
    return pl.pallas_call(
        _qkv_kernel,
        grid=(b, t // _TM),
        in_specs=in_specs,
        out_specs=[hblk, hblk, blk, blk, blk],
        out_shape=[f32, f32, b16, b16, b16],
        input_output_aliases=aliases,
        compiler_params=_params(("arbitrary", "arbitrary")),
        name="qkv_proj",
    )(*ins)


def _diff_attn_kernel(slope_ref, q_ref, k_ref, v_ref, lam_ref, sg_ref, o_ref,
                      qs_scr, m_scr, l_scr, acc_scr, *, lam_init):
    hg = pl.program_id(1)
    q0 = pl.program_id(2) * _TQ
    rows = 2 * _TQ

    for hh in range(_ATT_HEADS):
        q = q_ref[:, hh * A_VD:(hh + 1) * A_VD]
        lane = lax.broadcasted_iota(jnp.int32, q.shape, 1)
        zero = jnp.zeros_like(q)
        qs_scr[hh, 0:_TQ, :] = jnp.where(lane < A_HD, q, zero)
        qs_scr[hh, _TQ:rows, :] = jnp.where(lane >= A_HD, q, zero)
    m_scr[...] = jnp.full(m_scr.shape, NEG_INF, F32)
    l_scr[...] = jnp.zeros(l_scr.shape, F32)
    acc_scr[...] = jnp.zeros(acc_scr.shape, F32)

    def chunk(j, hh, masked):
        k0 = pl.multiple_of(j * _TK, _TK)
        head = slice(hh * A_VD, (hh + 1) * A_VD)
        s = _dot_nt(qs_scr[hh], k_ref[pl.ds(k0, _TK), head])
        kpos = k0 + lax.broadcasted_iota(jnp.int32, (1, _TK), 1)
        s = s + (slope_ref[hg * _ATT_HEADS + hh] * LOG2E) * kpos.astype(F32)
        if masked:
            row = lax.broadcasted_iota(jnp.int32, (rows, _TK), 0)
            qpos = q0 + jnp.where(row >= _TQ, row - _TQ, row)
            col = k0 + lax.broadcasted_iota(jnp.int32, (rows, _TK), 1)
            s = jnp.where(qpos >= col, s, NEG_INF)
        m_old = m_scr[hh]
        m_new = jnp.maximum(m_old, jnp.max(s, axis=-1, keepdims=True))
        alpha = jnp.exp2(m_old - m_new)
        p = jnp.exp2(s - jnp.concatenate([m_new] * (_TK // A_VD), axis=1))
        l_scr[hh] = alpha * l_scr[hh] + jnp.sum(p, axis=-1, keepdims=True)
        acc_scr[hh] = alpha * acc_scr[hh] + _dot(p.astype(BF16), v_ref[pl.ds(k0, _TK), head])
        m_scr[hh] = m_new

    n_full = q0 // _TK

    def body(j, carry):
        for hh in range(_ATT_HEADS):
            chunk(j, hh, False)
        return carry

    lax.fori_loop(0, n_full, body, 0)
    for d in range(_TQ // _TK):
        for hh in range(_ATT_HEADS):
            chunk(n_full + d, hh, True)

    lam = _lam(lam_ref, lam_init)
    for hh in range(_ATT_HEADS):
        o = acc_scr[hh] / l_scr[hh]
        o = o[0:_TQ] - lam * o[_TQ:rows]
        o_ref[:, hh * A_VD:(hh + 1) * A_VD] = (_rms(o, sg_ref[...]) * (1.0 - lam_init)).astype(BF16)


def _diff_attention(slopes, qb, kb, vb, p, l, lam_init):
    b, t, _ = qb.shape
    assert _TQ % _TK == 0 and t % _TQ == 0 and A_HEADS % _ATT_HEADS == 0
    width = _ATT_HEADS * A_VD
    qblk = pl.BlockSpec((None, _TQ, width), lambda bi, h, qi: (bi, qi, h))
    kvblk = pl.BlockSpec((None, t, width), lambda bi, h, qi: (bi, 0, h))
    stat = pltpu.VMEM((_ATT_HEADS, 2 * _TQ, A_VD), F32)
    return pl.pallas_call(
        functools.partial(_diff_attn_kernel, lam_init=lam_init),
        grid=(b, A_HEADS // _ATT_HEADS, t // _TQ),
        in_specs=[pl.BlockSpec(memory_space=pltpu.SMEM), qblk, kvblk, kvblk,
                  _layer(p["a_lam"], l), _layer(p["a_subln_g"], l)],
        out_specs=qblk,
        out_shape=jax.ShapeDtypeStruct((b, t, A_W), BF16),
        scratch_shapes=[pltpu.VMEM((_ATT_HEADS, 2 * _TQ, A_VD), BF16), stat, stat, stat],
        compiler_params=_params(("arbitrary",) * 3),
        name="diff_attention",
    )(slopes, qb, kb, vb, p["a_lam"], p["a_subln_g"])


def _lane_group_mask(width, group, g):
    lane = lax.broadcasted_iota(jnp.int32, (1, width), 1)
    return (_idiv(lane, group) == g).astype(F32)


def _mixer_kernel(x_ref, o_ref, g_ref, wr_ref, wa_ref, wb_ref, wc_ref, wd_ref, wo_ref,
                  bcw_ref, bcb_ref, ccw_ref, ccb_ref, clg_ref, clb_ref, dlg_ref, dlb_ref,
                  dws_ref, dbst_ref, x1_ref, cbp_ref, ccp_ref, ubuf, cbuf, ycin, shwin, gates):
    tm = _TM

    @pl.when(pl.program_id(1) == 0)
    def _():
        ubuf[0:_B_HALO, :] = jnp.zeros((_B_HALO, B_W), F32)
        cbuf[0:_C_HALO, :] = jnp.zeros((_C_HALO, C_W), F32)

    x = x_ref[...]
    h = _rms(x, g_ref[...]).astype(BF16)

    zc = _dot(h, wr_ref[:, _R_CA:_R_DU])
    cbuf[_C_HALO:_C_HALO + tm, :] = zc[:, 0:256] * _sigmoid(zc[:, 256:512])
    base = _C_HALO - (C_K - 1)
    gcols = N_BRANCH * D_MODEL // (tm // _CONV_ROWS)
    for it, r in enumerate(range(0, tm, _CONV_ROWS)):
        c0 = _R_GZ + it * gcols
        gates[:, it * gcols:(it + 1) * gcols] = _sigmoid(_dot(h, wr_ref[:, c0:c0 + gcols]))
        acc = jnp.broadcast_to(ccb_ref[...], (_CONV_ROWS, C_W))
        for ph in range(_SUBLANES):
            n_taps = (C_K - 1 - ph) // _SUBLANES + 1
            rows = _CONV_ROWS + _SUBLANES * (n_taps - 1)
            shwin[ph, 0:rows, :] = cbuf[base + r + ph:base + r + ph + rows, :]
            for i in range(n_taps):
                k = ph + _SUBLANES * i
                acc = acc + ccw_ref[k:k + 1, :] * shwin[ph, _SUBLANES * i:_SUBLANES * i + _CONV_ROWS, :]
        y = _layer_norm(acc, clg_ref[...], clb_ref[...])
        ycin[r:r + _CONV_ROWS, :] = (y * _sigmoid(y)).astype(BF16)
    ccp_ref[...] = cbuf[_C_HALO + tm - (C_K - 1):_C_HALO + tm, :]
    cbuf[0:_C_HALO, :] = cbuf[tm:tm + _C_HALO, :]

    def gate(i):
        return gates[:, i * D_MODEL:(i + 1) * D_MODEL]

    merged = gate(2) * _dot(ycin[...], wc_ref[...])

    merged = merged + gate(0) * _dot(o_ref[...], wa_ref[...])

    zb = _dot(h, wr_ref[:, _R_BX:_R_CA])
    u = zb[:, 512:768] * zb[:, 0:256]
    ubuf[_B_HALO:_B_HALO + tm, :] = u
    conv_b = (bcw_ref[0:1, :] * ubuf[_B_HALO - 2:_B_HALO - 2 + tm, :]
              + bcw_ref[1:2, :] * ubuf[_B_HALO - 1:_B_HALO - 1 + tm, :]
              + bcw_ref[2:3, :] * u + bcb_ref[...])
    yb = _dot((zb[:, 256:512] * conv_b).astype(BF16), wb_ref[...])
    merged = merged + gate(1) * yb
    cbp_ref[...] = ubuf[_B_HALO + tm - (B_K - 1):_B_HALO + tm, :]
    ubuf[0:_B_HALO, :] = ubuf[tm:tm + _B_HALO, :]

    zd = jax.nn.gelu(_dot(h, wr_ref[:, _R_DU:_R_GZ]))
    du = zd[:, 0:256]
    dvn = _layer_norm(zd[:, 256:512], dlg_ref[...], dlb_ref[...])
    row = lax.broadcasted_iota(jnp.int32, (CHUNK, CHUNK), 0)
    col = lax.broadcasted_iota(jnp.int32, (CHUNK, CHUNK), 1)
    tril = (row >= col).astype(F32)
    wcat = jnp.concatenate([dws_ref[g] * tril for g in range(D_GROUPS)], axis=1).astype(BF16)
    masks = [_lane_group_mask(D_W, D_W // D_GROUPS, g) for g in range(D_GROUPS)]
    bsmat = sum(dbst_ref[:, g:g + 1] * masks[g] for g in range(D_GROUPS))
    for c in range(0, tm, CHUNK):
        vch = dvn[c:c + CHUNK]
        rhs = jnp.concatenate([(vch * masks[g]).astype(BF16) for g in range(D_GROUPS)], axis=0)
        s = _dot(wcat, rhs) + bsmat
        ycin[c:c + CHUNK, :] = (du[c:c + CHUNK] * s).astype(BF16)
    yd = _dot(ycin[...], wd_ref[...])
    merged = merged + gate(3) * yd

    x1_ref[...] = x + _dot(merged.astype(BF16), wo_ref[...])


def _mixer(x, o, p, l):
    b, t, _ = x.shape
    ws = [p[k] for k in ("norm_mix_g", "w_in", "w_a_out", "w_b_out", "w_c_out", "w_d_out", "w_o", "b_conv_w",
                         "b_conv_b", "c_conv_w", "c_conv_b", "c_ln_g", "c_ln_b", "d_ln_g", "d_ln_b", "d_ws",
                         "d_bs_t")]
    row = lambda w: pl.BlockSpec((None, _TM, w), lambda i, j: (i, j, 0))
    return pl.pallas_call(
        _mixer_kernel,
        grid=(b, t // _TM),
        in_specs=[row(D_MODEL), row(A_W)] + [_layer(w, l) for w in ws],
        out_specs=[row(D_MODEL),
                   pl.BlockSpec((None, B_K - 1, B_W), lambda i, j: (i, 0, 0)),
                   pl.BlockSpec((None, C_K - 1, C_W), lambda i, j: (i, 0, 0))],
        out_shape=[jax.ShapeDtypeStruct((b, t, D_MODEL), F32),
                   jax.ShapeDtypeStruct((b, B_K - 1, B_W), F32),
                   jax.ShapeDtypeStruct((b, C_K - 1, C_W), F32)],
        scratch_shapes=[pltpu.VMEM((_B_HALO + _TM, B_W), F32), pltpu.VMEM((_C_HALO + _TM, C_W), F32),
                        pltpu.VMEM((_TM, C_W), BF16),
                        pltpu.VMEM((_SUBLANES, _CONV_ROWS + _C_HALO, C_W), F32),
                        pltpu.VMEM((_TM, N_BRANCH * D_MODEL), F32)],
        compiler_params=_params(("arbitrary", "arbitrary")),
        name="mixer",
    )(x, o, *ws)


def _ffn(x2, gf, wup_ref, wdn_ref):
    h3 = _rms(x2, gf).astype(BF16)
    acc = jnp.zeros(x2.shape, F32)
    for c in range(0, D_FF, _FF_CHUNK):
        a = jnp.maximum(_dot(h3, wup_ref[:, c:c + _FF_CHUNK]), 0.0)
        acc = acc + _dot((a * a).astype(BF16), wdn_ref[c:c + _FF_CHUNK, :])
    return x2 + acc


def _cross_attention(x, mk_ref, mv_ref, gx, wxq_ref, qg, wxo_ref):
    h2 = _rms(x, gx).astype(BF16)
    q = _dot(h2, wxq_ref[...])
    oms = []
    for i in range(X_HEADS):
        sl = slice(i * X_HD, (i + 1) * X_HD)
        qh = (_rms(q[:, sl], qg) * (X_HD ** -0.5)).astype(BF16)
        s = _dot_nt(qh, mk_ref[:, sl])
        p = jnp.exp(s - jnp.max(s, axis=-1, keepdims=True))
        oms.append(_dot(p.astype(BF16), mv_ref[:, sl]) * (1.0 / jnp.sum(p, axis=-1, keepdims=True)))
    om = jnp.concatenate(oms, axis=1).astype(BF16)
    return x + _dot(om, wxo_ref[...])


def _xffn_kernel(x_ref, mk_ref, mv_ref, gx_ref, wxq_ref, qg_ref, wxo_ref, gf_ref, wup_ref, wdn_ref,
                 out_ref):
    x2 = _cross_attention(x_ref[...], mk_ref, mv_ref, gx_ref[...], wxq_ref, qg_ref[...], wxo_ref)
    out_ref[...] = _ffn(x2, gf_ref[...], wup_ref, wdn_ref)


def _xattn_ffn(x, mkb, mvb, p, l):
    b, t, _ = x.shape
    ws = [p[k] for k in ("norm_x_g", "w_xq", "x_qnorm_g", "w_xo", "norm_ffn_g", "w_up", "w_down")]
    n = mkb.shape[1]
    row = pl.BlockSpec((None, _TM, D_MODEL), lambda i, j: (i, j, 0))
    mem = pl.BlockSpec((None, n, X_W), lambda i, j: (i, 0, 0))
    return pl.pallas_call(
        _xffn_kernel,
        grid=(b, t // _TM),
        in_specs=[row, mem, mem] + [_layer(w, l) for w in ws],
        out_specs=row,
        out_shape=jax.ShapeDtypeStruct((b, t, D_MODEL), F32),
        compiler_params=_params(("arbitrary", "arbitrary")),
        name="xattn_ffn",
    )(x, mkb, mvb, *ws)


def _sample_mixer_kernel(x_ref, sb_ref, sc_ref, g_ref, wr_ref, qg_ref, kg_ref,
                         wb_ref, wc_ref, wd_ref, bcw_ref, bcb_ref, ccw_ref, ccb_ref, clg_ref, clb_ref,
                         dlg_ref, dlb_ref, dws_ref, dbs_ref,
                         q_ref, k_ref, v_ref, cb_ref, cc_ref, dvn_ref, part_ref, g0_ref):
    x = x_ref[...]
    h = _rms(x, g_ref[...]).astype(BF16)
    q, k, v = _qkv_from_h(h, wr_ref[:, 0:_QKV_COLS], qg_ref[...], kg_ref[...])
    q_ref[...] = q
    k_ref[...] = k
    v_ref[...] = v

    def gate(i):
        return _sigmoid(_dot(h, wr_ref[:, _R_GZ + i * D_MODEL:_R_GZ + (i + 1) * D_MODEL]))

    g0_ref[...] = gate(0)

    zb = _dot(h, wr_ref[:, _R_BX:_R_CA])
    u = zb[:, 512:768] * zb[:, 0:256]
    conv_b = (bcw_ref[0:1, :] * sb_ref[:, 0:B_W] + bcw_ref[1:2, :] * sb_ref[:, B_W:2 * B_W]
              + bcw_ref[2:3, :] * u + bcb_ref[...])
    part = gate(1) * _dot((zb[:, 256:512] * conv_b).astype(BF16), wb_ref[...])
    cb_ref[:, 0:B_W] = sb_ref[:, B_W:2 * B_W]
    cb_ref[:, B_W:2 * B_W] = u

    zc = _dot(h, wr_ref[:, _R_CA:_R_DU])
    uc = zc[:, 0:256] * _sigmoid(zc[:, 256:512])
    acc = ccw_ref[C_K - 1:C_K, :] * uc + ccb_ref[...]
    for kk in range(C_K - 1):
        acc = acc + ccw_ref[kk:kk + 1, :] * sc_ref[:, kk * C_W:(kk + 1) * C_W]
    y = _layer_norm(acc, clg_ref[...], clb_ref[...])
    yc = _dot((y * _sigmoid(y)).astype(BF16), wc_ref[...])
    part = part + gate(2) * yc
    cc_ref[:, 0:(C_K - 2) * C_W] = sc_ref[:, C_W:(C_K - 1) * C_W]
    cc_ref[:, (C_K - 2) * C_W:(C_K - 1) * C_W] = uc

    zd = jax.nn.gelu(_dot(h, wr_ref[:, _R_DU:_R_GZ]))
    dvn = _layer_norm(zd[:, 256:512], dlg_ref[...], dlb_ref[...])
    dvn_ref[...] = dvn
    w00 = sum(dws_ref[g, 0:1, 0:1] * _lane_group_mask(D_W, D_W // D_GROUPS, g) for g in range(D_GROUPS))
    b0 = sum(dbs_ref[g:g + 1, 0:1] * _lane_group_mask(D_W, D_W // D_GROUPS, g) for g in range(D_GROUPS))
    yd = _dot((zd[:, 0:256] * (w00 * dvn + b0)).astype(BF16), wd_ref[...])
    part_ref[...] = part + gate(3) * yd


def _sample_mixer(x, sb, sc, p, l):
    n = x.shape[0]
    shapes = [(n, A_W), (n, A_W), (n, A_W), (n, (B_K - 1) * B_W), (n, (C_K - 1) * C_W), (n, D_W),
              (n, D_MODEL), (n, D_MODEL)]
    ins = [x, sb, sc]
    ws = [p[k] for k in ("norm_mix_g", "w_in", "a_qnorm_g", "a_knorm_g", "w_b_out", "w_c_out", "w_d_out",
                         "b_conv_w", "b_conv_b", "c_conv_w", "c_conv_b", "c_ln_g", "c_ln_b", "d_ln_g", "d_ln_b",
                         "d_ws", "d_bs")]
    return pl.pallas_call(
        _sample_mixer_kernel,
        grid=(1,),
        in_specs=[_full(a.shape) for a in ins] + [_layer(w, l) for w in ws],
        out_specs=[_full(s) for s in shapes],
        out_shape=[jax.ShapeDtypeStruct(s, F32) for s in shapes],
        compiler_params=_params(("arbitrary",)),
        name="sample_mixer",
    )(*ins, *ws)


_DEC_ROWS = 2 * A_HEADS
_PAGE_ROWS = PAGE_SIZE * A_HEADS


def _head_rows(ref, row, per_head):
    return sum(jnp.where(_idiv(row, per_head) == h, jnp.broadcast_to(ref[h:h + 1, :], row.shape), 0.0)
               for h in range(ref.shape[0]))


def _decode_init(q_ref, qz_scr, m_scr, l_scr, acc_scr):
    row = lax.broadcasted_iota(jnp.int32, (_DEC_ROWS, A_VD), 0)
    lane = lax.broadcasted_iota(jnp.int32, (_DEC_ROWS, A_VD), 1)
    qz_scr[...] = jnp.where(_idiv(lane, A_HD) == (row & 1), _head_rows(q_ref, row, 2), 0.0)
    m_scr[...] = jnp.full(m_scr.shape, NEG_INF, F32)
    l_scr[...] = jnp.zeros(l_scr.shape, F32)
    acc_scr[...] = jnp.zeros(acc_scr.shape, F32)


def _decode_pages(j, k_refs, v_refs, qz_scr, m_scr, l_scr, acc_scr, past_len):
    n_pages = len(k_refs)
    width = n_pages * _PAGE_ROWS
    rowv = lax.broadcasted_iota(jnp.int32, (_DEC_ROWS, 1), 0)
    slope = sum(jnp.where(_idiv(rowv, 2) == i, 2.0 ** (-8.0 * (i + 1) / A_HEADS), 0.0)
                for i in range(A_HEADS))
    qz = qz_scr[...].astype(BF16)
    s = jnp.concatenate([_dot_nt(qz, k_refs[i][...].astype(BF16)) for i in range(n_pages)], axis=1)
    col = lax.broadcasted_iota(jnp.int32, (_DEC_ROWS, width), 1)
    kpos = j * (n_pages * PAGE_SIZE) + _idiv(lax.broadcasted_iota(jnp.int32, (1, width), 1), A_HEADS)
    own_head = (col & (A_HEADS - 1)) == _idiv(lax.broadcasted_iota(jnp.int32, (_DEC_ROWS, width), 0), 2)
    s = jnp.where(own_head, s - slope * (past_len - kpos).astype(F32), NEG_INF)
    m_old = m_scr[...]
    m_new = jnp.maximum(m_old, jnp.max(s, axis=-1, keepdims=True))
    alpha = jnp.exp(m_old - m_new)
    p = jnp.exp(s - m_new)
    l_scr[...] = alpha * l_scr[...] + jnp.sum(p, axis=-1, keepdims=True)
    p = p.astype(BF16)
    pv = sum(_dot(p[:, i * _PAGE_ROWS:(i + 1) * _PAGE_ROWS], v_refs[i][...].astype(BF16))
             for i in range(n_pages))
    acc_scr[...] = alpha * acc_scr[...] + pv
    m_scr[...] = m_new


def _decode_finish(kn_ref, vn_ref, lam_ref, sg_ref, o_ref, qz_scr, m_scr, l_scr, acc_scr, lam_init):
    row = lax.broadcasted_iota(jnp.int32, (_DEC_ROWS, A_VD), 0)
    s_new = jnp.sum(qz_scr[...] * _head_rows(kn_ref, row, 2), axis=-1, keepdims=True)
    m_old = m_scr[...]
    m_fin = jnp.maximum(m_old, s_new)
    alpha = jnp.exp(m_old - m_fin)
    p_new = jnp.exp(s_new - m_fin)
    l_fin = alpha * l_scr[...] + p_new
    o_all = (alpha * acc_scr[...] + p_new * _head_rows(vn_ref, row, 2)) / l_fin
    lam = _lam(lam_ref, lam_init)
    sg = sg_ref[...]
    for h in range(A_HEADS):
        o = o_all[2 * h:2 * h + 1] - lam * o_all[2 * h + 1:2 * h + 2]
        o_ref[h:h + 1, :] = _rms(o, sg) * (1.0 - lam_init)


def _decode_attn_kernel(pt_ref, q_ref, kn_ref, vn_ref, lam_ref, sg_ref, *rest, lam_init, past_len):
    del pt_ref
    k_refs = rest[:_DEC_PAGES]
    v_refs = rest[_DEC_PAGES:2 * _DEC_PAGES]
    o_ref = rest[2 * _DEC_PAGES]
    dec = rest[2 * _DEC_PAGES + 1:]
    j = pl.program_id(1)

    @pl.when(j == 0)
    def _():
        _decode_init(q_ref, *dec)

    _decode_pages(j, k_refs, v_refs, *dec, past_len)

    @pl.when(j == pl.num_programs(1) - 1)
    def _():
        _decode_finish(kn_ref, vn_ref, lam_ref, sg_ref, o_ref, *dec, lam_init)


def _decode_attention(page_table, q, k_new, v_new, cache_k, cache_v, p, layer, lam_init):
    n, n_pages = page_table.shape
    past_len = n_pages * PAGE_SIZE
    row = pl.BlockSpec((None, A_HEADS, A_VD), lambda b, j, pt: (b, 0, 0))

    def page(i):
        return pl.BlockSpec((None, None, _PAGE_ROWS, A_VD),
                            lambda b, j, pt: (layer, pt[b, j * _DEC_PAGES + i], 0, 0))

    grid_spec = pltpu.PrefetchScalarGridSpec(
        num_scalar_prefetch=1,
        grid=(n, n_pages // _DEC_PAGES),
        in_specs=[row, row, row, _layer(p["a_lam"], layer), _layer(p["a_subln_g"], layer)]
        + [page(i) for i in range(_DEC_PAGES)] + [page(i) for i in range(_DEC_PAGES)],
        out_specs=row,
        scratch_shapes=[pltpu.VMEM((_DEC_ROWS, A_VD), F32), pltpu.VMEM((_DEC_ROWS, 1), F32),
                        pltpu.VMEM((_DEC_ROWS, 1), F32), pltpu.VMEM((_DEC_ROWS, A_VD), F32)],
    )
    heads = lambda a: a.reshape(n, A_HEADS, A_VD)
    return pl.pallas_call(
        functools.partial(_decode_attn_kernel, lam_init=lam_init, past_len=past_len),
        grid_spec=grid_spec,
        out_shape=jax.ShapeDtypeStruct((n, A_HEADS, A_VD), F32),
        compiler_params=_params(("arbitrary", "arbitrary")),
        name="decode_attention",
    )(page_table, heads(q), heads(k_new), heads(v_new), p["a_lam"], p["a_subln_g"],
      *([cache_k] * _DEC_PAGES), *([cache_v] * _DEC_PAGES))


def _sample_tail_kernel(x_ref, o_ref, g0_ref, part_ref, mk_ref, mv_ref, wa_ref, wo_ref, gx_ref, wxq_ref,
                        qg_ref, wxo_ref, gf_ref, wup_ref, wdn_ref, out_ref, x1_scr, q_scr, om_scr):
    b = pl.program_id(0)
    nb = pl.num_programs(0)

    @pl.when(b == 0)
    def _():
        merged = g0_ref[...] * _dot(o_ref[...].astype(BF16), wa_ref[...]) + part_ref[...]
        x1 = x_ref[...] + _dot(merged.astype(BF16), wo_ref[...])
        x1_scr[...] = x1
        q = _dot(_rms(x1, gx_ref[...]).astype(BF16), wxq_ref[...])
        qg = qg_ref[...]
        for h in range(X_HEADS):
            q_scr[h] = _rms(q[:, h * X_HD:(h + 1) * X_HD], qg) * (X_HD ** -0.5)

    n_rows = mk_ref.shape[1]
    row = lax.broadcasted_iota(jnp.int32, (8, X_HD), 0)
    col = lax.broadcasted_iota(jnp.int32, (8, n_rows), 1)
    own_head = (col & (X_HEADS - 1)) == lax.broadcasted_iota(jnp.int32, (8, n_rows), 0)
    for i in range(_TAIL_SEQS):
        seq = b * _TAIL_SEQS + i
        qz = sum(jnp.where(row == h, jnp.broadcast_to(q_scr[h, pl.ds(seq, 1), :], (8, X_HD)), 0.0)
                 for h in range(X_HEADS)).astype(BF16)
        s = jnp.where(own_head, _dot_nt(qz, mk_ref[i].astype(BF16)), NEG_INF)
        p = jnp.exp(s - jnp.max(s, axis=-1, keepdims=True))
        p = p / jnp.sum(p, axis=-1, keepdims=True)
        om = _dot(p.astype(BF16), mv_ref[i].astype(BF16))
        for h in range(X_HEADS):
            om_scr[h, pl.ds(seq, 1), :] = om[h:h + 1, :]

    @pl.when(b == nb - 1)
    def _():
        x2 = x1_scr[...] + sum(_dot(om_scr[h].astype(BF16), wxo_ref[h * X_HD:(h + 1) * X_HD, :])
                               for h in range(X_HEADS))
        out_ref[...] = _ffn(x2, gf_ref[...], wup_ref, wdn_ref)


def _sample_tail(x, o, g0, part, mem_k, mem_v, layer, p):
    n = x.shape[0]
    n_rows = mem_k.shape[2]
    assert n % _TAIL_SEQS == 0
    mem = pl.BlockSpec((None, _TAIL_SEQS, n_rows, X_HD), lambda b: (layer, b, 0, 0))
    ins = [x, o, g0, part]
    ws = [p[k] for k in ("w_a_out", "w_o", "norm_x_g", "w_xq", "x_qnorm_g", "w_xo", "norm_ffn_g", "w_up",
                         "w_down")]
    return pl.pallas_call(
        _sample_tail_kernel,
        grid=(n // _TAIL_SEQS,),
        in_specs=[_full(a.shape) for a in ins] + [mem, mem] + [_layer(w, layer) for w in ws],
        out_specs=_full((n, D_MODEL)),
        out_shape=jax.ShapeDtypeStruct((n, D_MODEL), F32),
        scratch_shapes=[pltpu.VMEM((n, D_MODEL), F32), pltpu.VMEM((X_HEADS, n, X_HD), F32),
                        pltpu.VMEM((X_HEADS, n, X_HD), F32)],
        compiler_params=_params(("arbitrary",)),
        name="sample_tail",
    )(*ins, mem_k, mem_v, *ws)


def _stacked_params(a):
    row = lambda v: v.reshape(v.shape[0], 1, -1)
    b16 = lambda w: w.astype(BF16)
    tiled = lambda v: row(jnp.tile(v, (1, A_W // A_HD)))
    out = {k: b16(a[k]) for k in ("w_in", "w_a_out", "w_b_out", "w_c_out", "w_d_out", "w_o", "w_xq", "w_xk",
                                  "w_xv", "w_xo", "w_up", "w_down")}
    out.update({k: row(a[k]) for k in ("norm_mix_g", "a_subln_g", "b_conv_b", "c_conv_b", "c_ln_g", "c_ln_b",
                                       "d_ln_g", "d_ln_b", "norm_x_g", "mem_norm_g", "x_qnorm_g", "x_knorm_g",
                                       "norm_ffn_g")})
    out.update({k: a[k] for k in ("a_lam", "b_conv_w", "c_conv_w", "d_ws", "d_bs")})
    out.update(a_qnorm_g=tiled(a["a_qnorm_g"]), a_knorm_g=tiled(a["a_knorm_g"]),
               d_bs_t=jnp.swapaxes(a["d_bs"], 1, 2))
    return out


def kernel(x_prompt, x_sample, cache_k_a, cache_v_a, state_conv_b, state_conv_c, cache_mem_k, cache_mem_v,
           page_table, mem_prompt, norm_mix_g, w_in, a_qnorm_g, a_knorm_g, a_lam, a_subln_g, w_a_out,
           b_conv_w, b_conv_b, w_b_out, c_conv_w, c_conv_b, c_ln_g, c_ln_b, w_c_out, d_ln_g, d_ln_b,
           d_ws, d_bs, w_d_out, w_o, norm_x_g, mem_norm_g, w_xq, w_xk, w_xv, x_qnorm_g, x_knorm_g,
           w_xo, norm_ffn_g, w_up, w_down):
    weights = dict(norm_mix_g=norm_mix_g, w_in=w_in, a_qnorm_g=a_qnorm_g, a_knorm_g=a_knorm_g, a_lam=a_lam,
                   a_subln_g=a_subln_g, w_a_out=w_a_out, b_conv_w=b_conv_w, b_conv_b=b_conv_b,
                   w_b_out=w_b_out, c_conv_w=c_conv_w, c_conv_b=c_conv_b, c_ln_g=c_ln_g, c_ln_b=c_ln_b,
                   w_c_out=w_c_out, d_ln_g=d_ln_g, d_ln_b=d_ln_b, d_ws=d_ws, d_bs=d_bs, w_d_out=w_d_out,
                   w_o=w_o, norm_x_g=norm_x_g, mem_norm_g=mem_norm_g, w_xq=w_xq, w_xk=w_xk, w_xv=w_xv,
                   x_qnorm_g=x_qnorm_g, x_knorm_g=x_knorm_g, w_xo=w_xo, norm_ffn_g=norm_ffn_g,
                   w_up=w_up, w_down=w_down)
    depth = w_in.shape[0]
    bp, t, _ = x_prompt.shape
    ns = x_sample.shape[0]
    n_pool = cache_k_a.shape[1]
    n_mem = cache_mem_k.shape[2]
    cache_k = cache_k_a.reshape(depth, n_pool, _PAGE_ROWS, A_VD)
    cache_v = cache_v_a.reshape(depth, n_pool, _PAGE_ROWS, A_VD)
    mem_k_s = cache_mem_k.reshape(depth, ns, n_mem * X_HEADS, X_HD)
    mem_v_s = cache_mem_v.reshape(depth, ns, n_mem * X_HEADS, X_HD)
    slopes = jnp.asarray([2.0 ** (-8.0 * (i + 1) / A_HEADS) for i in range(A_HEADS)], F32)

    xp = x_prompt
    xs = x_sample.reshape(ns, D_MODEL)
    outs = [[] for _ in range(9)]
    p = _stacked_params(weights)
    for l in range(depth):
        lam_init = 0.8 - 0.6 * math.exp(-0.3 * l)

        mk, mv, mkb, mvb = _memory_kv(mem_prompt, p, l)
        k_p, v_p, qb, kb, vb = _qkv_proj(xp, p, l, depth, None if l == 0 else (k_p, v_p))
        o_p = _diff_attention(slopes, qb, kb, vb, p, l, lam_init)
        xp, cb_p, cc_p = _mixer(xp, o_p, p, l)
        xp = _xattn_ffn(xp, mkb, mvb, p, l)

        sb = state_conv_b[l].reshape(ns, (B_K - 1) * B_W)
        sc = state_conv_c[l].reshape(ns, (C_K - 1) * C_W)
        q_s, k_s, v_s, cb_s, cc_s, dvn_s, part, g0 = _sample_mixer(xs, sb, sc, p, l)
        o_s = _decode_attention(page_table, q_s, k_s, v_s, cache_k, cache_v, p, l, lam_init)
        xs = _sample_tail(xs, o_s.reshape(ns, A_W), g0, part, mem_k_s, mem_v_s, l, p)

        for lst, val in zip(outs, (
                cb_p, cc_p, mk, mv,
                k_s.reshape(ns, 1, A_HEADS, 2 * A_HD), v_s.reshape(ns, 1, A_HEADS, A_VD),
                cb_s.reshape(ns, B_K - 1, B_W), cc_s.reshape(ns, C_K - 1, C_W),
                dvn_s.reshape(ns, 1, D_W))):
            lst.append(val)
    return (xp, xs.reshape(ns, 1, D_MODEL), k_p, v_p) + tuple(jnp.stack(o) for o in outs)
```

```python
import functools
import math

import jax
import jax.numpy as jnp
from jax import lax
from jax.experimental import pallas as pl
from jax.experimental.pallas import tpu as pltpu

F32 = jnp.float32
BF16 = jnp.bfloat16

D_MODEL = 1024
A_HEADS = 4
A_HD = 64
A_VD = 2 * A_HD
A_W = A_HEADS * 2 * A_HD
B_W = 256
B_K = 3
C_W = 256
C_K = 31
D_W = 256
D_GROUPS = 4
CHUNK = 128
N_BRANCH = 4
X_HEADS = 4
X_HD = 128
X_W = X_HEADS * X_HD
D_FF = 4 * D_MODEL
PAGE_SIZE = 128
EPS = 1e-6
NEG_INF = -1e30
LOG2E = math.log2(math.e)

_QKV_COLS = 3 * A_W
_R_BX = _QKV_COLS
_R_CA = _R_BX + 3 * B_W
_R_DU = _R_CA + 2 * C_W
_R_GZ = _R_DU + 2 * D_W
IN_COLS = _R_GZ + N_BRANCH * D_MODEL

_V7X_VMEM_BYTES = 64 * 1024 * 1024
_VMEM_LIMIT = _V7X_VMEM_BYTES - 8 * 1024 * 1024

_TM = 512
_TQ = 512
_TK = 512
_ATT_HEADS = 4
_SUBLANES = 8
_MXU_TILE = 256
_CONV_ROWS = 64
_B_HALO = 8
_C_HALO = 32
_FF_CHUNK = 1024
_DEC_PAGES = 32
_TAIL_SEQS = 4


def _rms(x, g):
    return x * lax.rsqrt(jnp.mean(x * x, axis=-1, keepdims=True) + EPS) * g


def _layer_norm(x, g, b):
    xc = x - jnp.mean(x, axis=-1, keepdims=True)
    var = jnp.mean(xc * xc, axis=-1, keepdims=True)
    return xc * lax.rsqrt(var + EPS) * g + b


def _sigmoid(x):
    return 0.5 * jnp.tanh(0.5 * x) + 0.5


def _gated(t, y_half):
    return t * y_half + y_half


def _dot(a, b):
    return jnp.dot(a, b, preferred_element_type=F32)


def _dot_nt(a, b):
    return lax.dot_general(a, b, (((1,), (1,)), ((), ())), preferred_element_type=F32)


def _idiv(x, d):
    assert d & (d - 1) == 0
    return lax.shift_right_logical(x, int(math.log2(d)))


def _group_mean_matrix(width, group):
    r = _idiv(lax.broadcasted_iota(jnp.int32, (width, width), 0), group)
    c = _idiv(lax.broadcasted_iota(jnp.int32, (width, width), 1), group)
    return jnp.where(r == c, 1.0 / group, 0.0).astype(BF16)


def _group_rms(t, g, gm):
    w = gm.shape[0]
    sq = (t * t).astype(BF16)
    ms = jnp.concatenate([_dot(sq[:, c:c + w], gm) for c in range(0, t.shape[1], w)], axis=1)
    return t * lax.rsqrt(ms + EPS) * g


def _lam(lam_ref, lam_init):
    a = lam_ref[...]
    s1 = jnp.sum(a[0:1] * a[1:2], axis=-1, keepdims=True)
    s2 = jnp.sum(a[2:3] * a[3:4], axis=-1, keepdims=True)
    return jnp.exp(s1) - jnp.exp(s2) + lam_init


def _full(shape):
    return pl.BlockSpec(shape, lambda *_: (0,) * len(shape))


def _layer(arr, l, cols=None):
    shape = arr.shape[1:] if cols is None else arr.shape[1:-1] + (cols,)
    return pl.BlockSpec((None,) + shape, lambda *_: (l,) + (0,) * len(shape))


def _params(sem):
    return pltpu.CompilerParams(dimension_semantics=sem, vmem_limit_bytes=_VMEM_LIMIT)


def _memkv_kernel(mem_ref, g_ref, wk_ref, wv_ref, kg_ref, mk_ref, mv_ref, mkb_ref, mvb_ref):
    h = _rms(mem_ref[...], g_ref[...]).astype(BF16)
    k = _dot(h, wk_ref[...])
    v = _dot(h, wv_ref[...])
    kg = kg_ref[...]
    k = jnp.concatenate([_rms(k[:, i * X_HD:(i + 1) * X_HD], kg) for i in range(X_HEADS)], axis=1)
    mk_ref[...] = k.reshape(mk_ref.shape)
    mv_ref[...] = v.reshape(mv_ref.shape)
    mkb_ref[...] = k.astype(BF16)
    mvb_ref[...] = v.astype(BF16)


def _memory_kv(mem, p, l):
    b, n, _ = mem.shape
    ws = [p["mem_norm_g"], p["w_xk"], p["w_xv"], p["x_knorm_g"]]
    blk = pl.BlockSpec((None, n, X_W), lambda i: (i, 0, 0))
    hblk = pl.BlockSpec((None, n, X_HEADS, X_HD), lambda i: (i, 0, 0, 0))
    return pl.pallas_call(
        _memkv_kernel,
        grid=(b,),
        in_specs=[pl.BlockSpec((None, n, D_MODEL), lambda i: (i, 0, 0))] + [_layer(w, l) for w in ws],
        out_specs=[hblk, hblk, blk, blk],
        out_shape=[jax.ShapeDtypeStruct((b, n, X_HEADS, X_HD), F32),
                   jax.ShapeDtypeStruct((b, n, X_HEADS, X_HD), F32),
                   jax.ShapeDtypeStruct((b, n, X_W), BF16), jax.ShapeDtypeStruct((b, n, X_W), BF16)],
        compiler_params=_params(("arbitrary",)),
        name="memory_kv",
    )(mem, *ws)


def _qkv_from_h(h, w, qg, kg):
    z = _dot(h, w)
    gm = _group_mean_matrix(_MXU_TILE, A_HD)
    q = _group_rms(z[:, :A_W], qg, gm) * (A_HD ** -0.5)
    k = _group_rms(z[:, A_W:2 * A_W], kg, gm)
    v = z[:, 2 * A_W:]
    return q, k, v


def _qkv_kernel(x_ref, g_ref, w_ref, qg_ref, kg_ref, *rest):
    k_ref, v_ref, qb_ref, kb_ref, vb_ref = rest[-5:]
    h = _rms(x_ref[...], g_ref[...]).astype(BF16)
    q, k, v = _qkv_from_h(h, w_ref[...], qg_ref[...], kg_ref[...])
    k_ref[...] = k.reshape(k_ref.shape)
    v_ref[...] = v.reshape(v_ref.shape)
    qb_ref[...] = (q * LOG2E).astype(BF16)
    kb_ref[...] = k.astype(BF16)
    vb_ref[...] = v.astype(BF16)


def _qkv_proj(x, p, layer, depth, stacks):
    b, t, _ = x.shape
    blk = pl.BlockSpec((None, _TM, A_W), lambda i, j: (i, j, 0))
    hblk = pl.BlockSpec((None, None, _TM, A_HEADS, A_VD), lambda i, j: (layer, i, j, 0, 0))
    f32 = jax.ShapeDtypeStruct((depth, b, t, A_HEADS, A_VD), F32)
    b16 = jax.ShapeDtypeStruct((b, t, A_W), BF16)
    ins = [x, p["norm_mix_g"], p["w_in"], p["a_qnorm_g"], p["a_knorm_g"]]
    in_specs = [pl.BlockSpec((None, _TM, D_MODEL), lambda i, j: (i, j, 0)), _layer(ins[1], layer),
                _layer(ins[2], layer, cols=_QKV_COLS), _layer(ins[3], layer), _layer(ins[4], layer)]
    aliases = {}
    if stacks is not None:
        aliases = {len(ins): 0, len(ins) + 1: 1}
        ins += list(stacks)
        in_specs += [pl.BlockSpec(memory_space=pl.ANY)] * 2
    return pl.pallas_call(
        _qkv_kernel,
        grid=(b, t // _TM),
        in_specs=in_specs,
        out_specs=[hblk, hblk, blk, blk, blk],
        out_shape=[f32, f32, b16, b16, b16],
        input_output_aliases=aliases,
        compiler_params=_params(("arbitrary", "arbitrary")),
        name="qkv_proj",
    )(*ins)


def _diff_attn_kernel(slope_ref, q_ref, k_ref, v_ref, lam_ref, sg_ref, o_ref,
                      qs_scr, m_scr, l_scr, acc_scr, *, lam_init):
    hg = pl.program_id(1)
    q0 = pl.program_id(2) * _TQ
    rows = 2 * _TQ

    for hh in range(_ATT_HEADS):
        q = q_ref[:, hh * A_VD:(hh + 1) * A_VD]
        lane = lax.broadcasted_iota(jnp.int32, q.shape, 1)
        zero = jnp.zeros_like(q)
        qs_scr[hh, 0:_TQ, :] = jnp.where(lane < A_HD, q, zero)
        qs_scr[hh, _TQ:rows, :] = jnp.where(lane >= A_HD, q, zero)
    m_scr[...] = jnp.full(m_scr.shape, NEG_INF, F32)
    l_scr[...] = jnp.zeros(l_scr.shape, F32)
    acc_scr[...] = jnp.zeros(acc_scr.shape, F32)

    def chunk(j, hh, masked):
        k0 = pl.multiple_of(j * _TK, _TK)
        head = slice(hh * A_VD, (hh + 1) * A_VD)
        s = _dot_nt(qs_scr[hh], k_ref[pl.ds(k0, _TK), head])
        kpos = k0 + lax.broadcasted_iota(jnp.int32, (1, _TK), 1)
        s = s + (slope_ref[hg * _ATT_HEADS + hh] * LOG2E) * kpos.astype(F32)
        if masked:
            row = lax.broadcasted_iota(jnp.int32, (rows, _TK), 0)
            qpos = q0 + jnp.where(row >= _TQ, row - _TQ, row)
            col = k0 + lax.broadcasted_iota(jnp.int32, (rows, _TK), 1)
            s = jnp.where(qpos >= col, s, NEG_INF)
        m_old = m_scr[hh]
        m_new = jnp.maximum(m_old, jnp.max(s, axis=-1, keepdims=True))
        alpha = jnp.exp2(m_old - m_new)
        p = jnp.exp2(s - jnp.concatenate([m_new] * (_TK // A_VD), axis=1))
        l_scr[hh] = alpha * l_scr[hh] + jnp.sum(p, axis=-1, keepdims=True)
        acc_scr[hh] = alpha * acc_scr[hh] + _dot(p.astype(BF16), v_ref[pl.ds(k0, _TK), head])
        m_scr[hh] = m_new

    n_full = q0 // _TK

    def body(j, carry):
        for hh in range(_ATT_HEADS):
            chunk(j, hh, False)
        return carry

    lax.fori_loop(0, n_full, body, 0)
    for d in range(_TQ // _TK):
        for hh in range(_ATT_HEADS):
            chunk(n_full + d, hh, True)

    lam = _lam(lam_ref, lam_init)
    for hh in range(_ATT_HEADS):
        o = acc_scr[hh] / l_scr[hh]
        o = o[0:_TQ] - lam * o[_TQ:rows]
        o_ref[:, hh * A_VD:(hh + 1) * A_VD] = (_rms(o, sg_ref[...]) * (1.0 - lam_init)).astype(BF16)


def _diff_attention(slopes, qb, kb, vb, p, l, lam_init):
    b, t, _ = qb.shape
    assert _TQ % _TK == 0 and t % _TQ == 0 and A_HEADS % _ATT_HEADS == 0
    width = _ATT_HEADS * A_VD
    qblk = pl.BlockSpec((None, _TQ, width), lambda bi, h, qi: (bi, qi, h))
    kvblk = pl.BlockSpec((None, t, width), lambda bi, h, qi: (bi, 0, h))
    stat = pltpu.VMEM((_ATT_HEADS, 2 * _TQ, A_VD), F32)
    return pl.pallas_call(
        functools.partial(_diff_attn_kernel, lam_init=lam_init),
        grid=(b, A_HEADS // _ATT_HEADS, t // _TQ),
        in_specs=[pl.BlockSpec(memory_space=pltpu.SMEM), qblk, kvblk, kvblk,
                  _layer(p["a_lam"], l), _layer(p["a_subln_g"], l)],
        out_specs=qblk,
        out_shape=jax.ShapeDtypeStruct((b, t, A_W), BF16),
        scratch_shapes=[pltpu.VMEM((_ATT_HEADS, 2 * _TQ, A_VD), BF16), stat, stat, stat],
        compiler_params=_params(("arbitrary",) * 3),
        name="diff_attention",
    )(slopes, qb, kb, vb, p["a_lam"], p["a_subln_g"])


def _lane_group_mask(width, group, g):
    lane = lax.broadcasted_iota(jnp.int32, (1, width), 1)
    return (_idiv(lane, group) == g).astype(F32)


def _mixer_kernel(x_ref, o_ref, g_ref, wr_ref, wa_ref, wb_ref, wc_ref, wd_ref, wo_ref,
                  bcw_ref, bcb_ref, ccw_ref, ccb_ref, clg_ref, clb_ref, dlg_ref, dlb_ref,
                  dws_ref, dbst_ref, x1_ref, cbp_ref, ccp_ref, ubuf, cbuf, ycin, shwin, gates):
    tm = _TM

    @pl.when(pl.program_id(1) == 0)
    def _():
        ubuf[0:_B_HALO, :] = jnp.zeros((_B_HALO, B_W), F32)
        cbuf[0:_C_HALO, :] = jnp.zeros((_C_HALO, C_W), F32)

    x = x_ref[...]
    h = _rms(x, g_ref[...]).astype(BF16)

    zc = _dot(h, wr_ref[:, _R_CA:_R_DU])
    cbuf[_C_HALO:_C_HALO + tm, :] = zc[:, 0:256] * _sigmoid(zc[:, 256:512])
    base = _C_HALO - (C_K - 1)
    gcols = N_BRANCH * D_MODEL // (tm // _CONV_ROWS)
    bias = ccb_ref[...]
    for it, r in enumerate(range(0, tm, _CONV_ROWS)):
        c0 = _R_GZ + it * gcols
        t = jnp.tanh(_dot(h, wr_ref[:, c0:c0 + gcols]))
        gates[:, it * gcols:(it + 1) * gcols] = t
        acc = jnp.broadcast_to(bias, (_CONV_ROWS, C_W))
        bias = ccb_ref[...] + 0.0 * t[0:1, 0:C_W]
        for ph in range(_SUBLANES):
            n_taps = (C_K - 1 - ph) // _SUBLANES + 1
            rows = _CONV_ROWS + _SUBLANES * (n_taps - 1)
            shwin[ph, 0:rows, :] = cbuf[base + r + ph:base + r + ph + rows, :]
            for i in range(n_taps):
                k = ph + _SUBLANES * i
                acc = acc + ccw_ref[k:k + 1, :] * shwin[ph, _SUBLANES * i:_SUBLANES * i + _CONV_ROWS, :]
        y = _layer_norm(acc, clg_ref[...], clb_ref[...])
        ycin[r:r + _CONV_ROWS, :] = (y * _sigmoid(y)).astype(BF16)
    ccp_ref[...] = cbuf[_C_HALO + tm - (C_K - 1):_C_HALO + tm, :]
    cbuf[0:_C_HALO, :] = cbuf[tm:tm + _C_HALO, :]

    def gate(i):
        return gates[:, i * D_MODEL:(i + 1) * D_MODEL]

    merged = _gated(gate(2), _dot(ycin[...], wc_ref[...]))

    merged = merged + _gated(gate(0), _dot(o_ref[...], wa_ref[...]))

    zb = _dot(h, wr_ref[:, _R_BX:_R_CA])
    u = zb[:, 512:768] * zb[:, 0:256]
    ubuf[_B_HALO:_B_HALO + tm, :] = u
    conv_b = (bcw_ref[0:1, :] * ubuf[_B_HALO - 2:_B_HALO - 2 + tm, :]
              + bcw_ref[1:2, :] * ubuf[_B_HALO - 1:_B_HALO - 1 + tm, :]
              + bcw_ref[2:3, :] * u + bcb_ref[...])
    yb = _dot((zb[:, 256:512] * conv_b).astype(BF16), wb_ref[...])
    merged = merged + _gated(gate(1), yb)
    cbp_ref[...] = ubuf[_B_HALO + tm - (B_K - 1):_B_HALO + tm, :]
    ubuf[0:_B_HALO, :] = ubuf[tm:tm + _B_HALO, :]

    zd = jax.nn.gelu(_dot(h, wr_ref[:, _R_DU:_R_GZ]))
    du = zd[:, 0:256]
    dvn = _layer_norm(zd[:, 256:512], dlg_ref[...], dlb_ref[...])
    row = lax.broadcasted_iota(jnp.int32, (CHUNK, CHUNK), 0)
    col = lax.broadcasted_iota(jnp.int32, (CHUNK, CHUNK), 1)
    tril = (row >= col).astype(F32)
    wcat = jnp.concatenate([dws_ref[g] * tril for g in range(D_GROUPS)], axis=1).astype(BF16)
    masks = [_lane_group_mask(D_W, D_W // D_GROUPS, g) for g in range(D_GROUPS)]
    bsmat = sum(dbst_ref[:, g:g + 1] * masks[g] for g in range(D_GROUPS))
    for c in range(0, tm, CHUNK):
        vch = dvn[c:c + CHUNK]
        rhs = jnp.concatenate([(vch * masks[g]).astype(BF16) for g in range(D_GROUPS)], axis=0)
        s = _dot(wcat, rhs) + bsmat
        ycin[c:c + CHUNK, :] = (du[c:c + CHUNK] * s).astype(BF16)
    yd = _dot(ycin[...], wd_ref[...])
    merged = merged + _gated(gate(3), yd)

    x1_ref[...] = x + _dot(merged.astype(BF16), wo_ref[...])


def _mixer(x, o, p, l):
    b, t, _ = x.shape
    ws = [p[k] for k in ("norm_mix_g", "w_in", "w_a_out", "w_b_out", "w_c_out", "w_d_out", "w_o", "b_conv_w",
                         "b_conv_b", "c_conv_w", "c_conv_b", "c_ln_g", "c_ln_b", "d_ln_g", "d_ln_b", "d_ws",
                         "d_bs_t")]
    row = lambda w: pl.BlockSpec((None, _TM, w), lambda i, j: (i, j, 0))
    return pl.pallas_call(
        _mixer_kernel,
        grid=(b, t // _TM),
        in_specs=[row(D_MODEL), row(A_W)] + [_layer(w, l) for w in ws],
        out_specs=[row(D_MODEL),
                   pl.BlockSpec((None, B_K - 1, B_W), lambda i, j: (i, 0, 0)),
                   pl.BlockSpec((None, C_K - 1, C_W), lambda i, j: (i, 0, 0))],
        out_shape=[jax.ShapeDtypeStruct((b, t, D_MODEL), F32),
                   jax.ShapeDtypeStruct((b, B_K - 1, B_W), F32),
                   jax.ShapeDtypeStruct((b, C_K - 1, C_W), F32)],
        scratch_shapes=[pltpu.VMEM((_B_HALO + _TM, B_W), F32), pltpu.VMEM((_C_HALO + _TM, C_W), F32),
                        pltpu.VMEM((_TM, C_W), BF16),
                        pltpu.VMEM((_SUBLANES, _CONV_ROWS + _C_HALO, C_W), F32),
                        pltpu.VMEM((_TM, N_BRANCH * D_MODEL), F32)],
        compiler_params=_params(("arbitrary", "arbitrary")),
        name="mixer",
    )(x, o, *ws)


def _ffn(x2, gf, wup_ref, wdn_ref):
    h3 = _rms(x2, gf).astype(BF16)
    acc = jnp.zeros(x2.shape, F32)
    for c in range(0, D_FF, _FF_CHUNK):
        a = jnp.maximum(_dot(h3, wup_ref[:, c:c + _FF_CHUNK]), 0.0)
        acc = acc + _dot((a * a).astype(BF16), wdn_ref[c:c + _FF_CHUNK, :])
    return x2 + acc


def _cross_attention(x, mk_ref, mv_ref, gx, wxq_ref, qg, wxo_ref):
    h2 = _rms(x, gx).astype(BF16)
    q = _dot(h2, wxq_ref[...])
    oms = []
    for i in range(X_HEADS):
        sl = slice(i * X_HD, (i + 1) * X_HD)
        qh = (_rms(q[:, sl], qg) * (X_HD ** -0.5)).astype(BF16)
        s = _dot_nt(qh, mk_ref[:, sl])
        p = jnp.exp(s - jnp.max(s, axis=-1, keepdims=True))
        oms.append(_dot(p.astype(BF16), mv_ref[:, sl]) * (1.0 / jnp.sum(p, axis=-1, keepdims=True)))
    om = jnp.concatenate(oms, axis=1).astype(BF16)
    return x + _dot(om, wxo_ref[...])


def _xffn_kernel(x_ref, mk_ref, mv_ref, gx_ref, wxq_ref, qg_ref, wxo_ref, gf_ref, wup_ref, wdn_ref,
                 out_ref):
    x2 = _cross_attention(x_ref[...], mk_ref, mv_ref, gx_ref[...], wxq_ref, qg_ref[...], wxo_ref)
    out_ref[...] = _ffn(x2, gf_ref[...], wup_ref, wdn_ref)


def _xattn_ffn(x, mkb, mvb, p, l):
    b, t, _ = x.shape
    ws = [p[k] for k in ("norm_x_g", "w_xq", "x_qnorm_g", "w_xo", "norm_ffn_g", "w_up", "w_down")]
    n = mkb.shape[1]
    row = pl.BlockSpec((None, _TM, D_MODEL), lambda i, j: (i, j, 0))
    mem = pl.BlockSpec((None, n, X_W), lambda i, j: (i, 0, 0))
    return pl.pallas_call(
        _xffn_kernel,
        grid=(b, t // _TM),
        in_specs=[row, mem, mem] + [_layer(w, l) for w in ws],
        out_specs=row,
        out_shape=jax.ShapeDtypeStruct((b, t, D_MODEL), F32),
        compiler_params=_params(("arbitrary", "arbitrary")),
        name="xattn_ffn",
    )(x, mkb, mvb, *ws)


def _sample_mixer_kernel(x_ref, sb_ref, sc_ref, g_ref, wr_ref, qg_ref, kg_ref,
                         wb_ref, wc_ref, wd_ref, bcw_ref, bcb_ref, ccw_ref, ccb_ref, clg_ref, clb_ref,
                         dlg_ref, dlb_ref, dws_ref, dbs_ref,
                         q_ref, k_ref, v_ref, cb_ref, cc_ref, dvn_ref, part_ref, g0_ref):
    x = x_ref[...]
    h = _rms(x, g_ref[...]).astype(BF16)
    q, k, v = _qkv_from_h(h, wr_ref[:, 0:_QKV_COLS], qg_ref[...], kg_ref[...])
    q_ref[...] = q
    k_ref[...] = k
    v_ref[...] = v

    def gate(i):
        return jnp.tanh(_dot(h, wr_ref[:, _R_GZ + i * D_MODEL:_R_GZ + (i + 1) * D_MODEL]))

    g0_ref[...] = gate(0)

    zb = _dot(h, wr_ref[:, _R_BX:_R_CA])
    u = zb[:, 512:768] * zb[:, 0:256]
    conv_b = (bcw_ref[0:1, :] * sb_ref[:, 0:B_W] + bcw_ref[1:2, :] * sb_ref[:, B_W:2 * B_W]
              + bcw_ref[2:3, :] * u + bcb_ref[...])
    part = _gated(gate(1), _dot((zb[:, 256:512] * conv_b).astype(BF16), wb_ref[...]))
    cb_ref[:, 0:B_W] = sb_ref[:, B_W:2 * B_W]
    cb_ref[:, B_W:2 * B_W] = u

    zc = _dot(h, wr_ref[:, _R_CA:_R_DU])
    uc = zc[:, 0:256] * _sigmoid(zc[:, 256:512])
    acc = ccw_ref[C_K - 1:C_K, :] * uc + ccb_ref[...]
    for kk in range(C_K - 1):
        acc = acc + ccw_ref[kk:kk + 1, :] * sc_ref[:, kk * C_W:(kk + 1) * C_W]
    y = _layer_norm(acc, clg_ref[...], clb_ref[...])
    yc = _dot((y * _sigmoid(y)).astype(BF16), wc_ref[...])
    part = part + _gated(gate(2), yc)
    cc_ref[:, 0:(C_K - 2) * C_W] = sc_ref[:, C_W:(C_K - 1) * C_W]
    cc_ref[:, (C_K - 2) * C_W:(C_K - 1) * C_W] = uc

    zd = jax.nn.gelu(_dot(h, wr_ref[:, _R_DU:_R_GZ]))
    dvn = _layer_norm(zd[:, 256:512], dlg_ref[...], dlb_ref[...])
    dvn_ref[...] = dvn
    w00 = sum(dws_ref[g, 0:1, 0:1] * _lane_group_mask(D_W, D_W // D_GROUPS, g) for g in range(D_GROUPS))
    b0 = sum(dbs_ref[g:g + 1, 0:1] * _lane_group_mask(D_W, D_W // D_GROUPS, g) for g in range(D_GROUPS))
    yd = _dot((zd[:, 0:256] * (w00 * dvn + b0)).astype(BF16), wd_ref[...])
    part_ref[...] = part + _gated(gate(3), yd)


def _sample_mixer(x, sb, sc, p, l):
    n = x.shape[0]
    shapes = [(n, A_W), (n, A_W), (n, A_W), (n, (B_K - 1) * B_W), (n, (C_K - 1) * C_W), (n, D_W),
              (n, D_MODEL), (n, D_MODEL)]
    ins = [x, sb, sc]
    ws = [p[k] for k in ("norm_mix_g", "w_in", "a_qnorm_g", "a_knorm_g", "w_b_out", "w_c_out", "w_d_out",
                         "b_conv_w", "b_conv_b", "c_conv_w", "c_conv_b", "c_ln_g", "c_ln_b", "d_ln_g", "d_ln_b",
                         "d_ws", "d_bs")]
    return pl.pallas_call(
        _sample_mixer_kernel,
        grid=(1,),
        in_specs=[_full(a.shape) for a in ins] + [_layer(w, l) for w in ws],
        out_specs=[_full(s) for s in shapes],
        out_shape=[jax.ShapeDtypeStruct(s, F32) for s in shapes],
        compiler_params=_params(("arbitrary",)),
        name="sample_mixer",
    )(*ins, *ws)


_DEC_ROWS = 2 * A_HEADS
_PAGE_ROWS = PAGE_SIZE * A_HEADS


def _head_rows(ref, row, per_head):
    return sum(jnp.where(_idiv(row, per_head) == h, jnp.broadcast_to(ref[h:h + 1, :], row.shape), 0.0)
               for h in range(ref.shape[0]))


def _decode_init(q_ref, qz_scr, m_scr, l_scr, acc_scr):
    row = lax.broadcasted_iota(jnp.int32, (_DEC_ROWS, A_VD), 0)
    lane = lax.broadcasted_iota(jnp.int32, (_DEC_ROWS, A_VD), 1)
    qz_scr[...] = jnp.where(_idiv(lane, A_HD) == (row & 1), _head_rows(q_ref, row, 2), 0.0)
    m_scr[...] = jnp.full(m_scr.shape, NEG_INF, F32)
    l_scr[...] = jnp.zeros(l_scr.shape, F32)
    acc_scr[...] = jnp.zeros(acc_scr.shape, F32)


def _decode_pages(j, k_refs, v_refs, qz_scr, m_scr, l_scr, acc_scr, past_len):
    n_pages = len(k_refs)
    width = n_pages * _PAGE_ROWS
    rowv = lax.broadcasted_iota(jnp.int32, (_DEC_ROWS, 1), 0)
    slope = sum(jnp.where(_idiv(rowv, 2) == i, 2.0 ** (-8.0 * (i + 1) / A_HEADS), 0.0)
                for i in range(A_HEADS))
    qz = qz_scr[...].astype(BF16)
    s = jnp.concatenate([_dot_nt(qz, k_refs[i][...].astype(BF16)) for i in range(n_pages)], axis=1)
    col = lax.broadcasted_iota(jnp.int32, (_DEC_ROWS, width), 1)
    kpos = j * (n_pages * PAGE_SIZE) + _idiv(lax.broadcasted_iota(jnp.int32, (1, width), 1), A_HEADS)
    own_head = (col & (A_HEADS - 1)) == _idiv(lax.broadcasted_iota(jnp.int32, (_DEC_ROWS, width), 0), 2)
    s = jnp.where(own_head, s - slope * (past_len - kpos).astype(F32), NEG_INF)
    m_old = m_scr[...]
    m_new = jnp.maximum(m_old, jnp.max(s, axis=-1, keepdims=True))
    alpha = jnp.exp(m_old - m_new)
    p = jnp.exp(s - m_new)
    l_scr[...] = alpha * l_scr[...] + jnp.sum(p, axis=-1, keepdims=True)
    p = p.astype(BF16)
    pv = sum(_dot(p[:, i * _PAGE_ROWS:(i + 1) * _PAGE_ROWS], v_refs[i][...].astype(BF16))
             for i in range(n_pages))
    acc_scr[...] = alpha * acc_scr[...] + pv
    m_scr[...] = m_new


def _decode_finish(kn_ref, vn_ref, lam_ref, sg_ref, o_ref, qz_scr, m_scr, l_scr, acc_scr, lam_init):
    row = lax.broadcasted_iota(jnp.int32, (_DEC_ROWS, A_VD), 0)
    s_new = jnp.sum(qz_scr[...] * _head_rows(kn_ref, row, 2), axis=-1, keepdims=True)
    m_old = m_scr[...]
    m_fin = jnp.maximum(m_old, s_new)
    alpha = jnp.exp(m_old - m_fin)
    p_new = jnp.exp(s_new - m_fin)
    l_fin = alpha * l_scr[...] + p_new
    o_all = (alpha * acc_scr[...] + p_new * _head_rows(vn_ref, row, 2)) / l_fin
    lam = _lam(lam_ref, lam_init)
    sg = sg_ref[...]
    for h in range(A_HEADS):
        o = o_all[2 * h:2 * h + 1] - lam * o_all[2 * h + 1:2 * h + 2]
        o_ref[h:h + 1, :] = _rms(o, sg) * (1.0 - lam_init)


def _decode_attn_kernel(pt_ref, q_ref, kn_ref, vn_ref, lam_ref, sg_ref, *rest, lam_init, past_len):
    del pt_ref
    k_refs = rest[:_DEC_PAGES]
    v_refs = rest[_DEC_PAGES:2 * _DEC_PAGES]
    o_ref = rest[2 * _DEC_PAGES]
    dec = rest[2 * _DEC_PAGES + 1:]
    j = pl.program_id(1)

    @pl.when(j == 0)
    def _():
        _decode_init(q_ref, *dec)

    _decode_pages(j, k_refs, v_refs, *dec, past_len)

    @pl.when(j == pl.num_programs(1) - 1)
    def _():
        _decode_finish(kn_ref, vn_ref, lam_ref, sg_ref, o_ref, *dec, lam_init)


def _decode_attention(page_table, q, k_new, v_new, cache_k, cache_v, p, layer, lam_init):
    n, n_pages = page_table.shape
    past_len = n_pages * PAGE_SIZE
    row = pl.BlockSpec((None, A_HEADS, A_VD), lambda b, j, pt: (b, 0, 0))

    def page(i):
        return pl.BlockSpec((None, None, _PAGE_ROWS, A_VD),
                            lambda b, j, pt: (layer, pt[b, j * _DEC_PAGES + i], 0, 0))

    grid_spec = pltpu.PrefetchScalarGridSpec(
        num_scalar_prefetch=1,
        grid=(n, n_pages // _DEC_PAGES),
        in_specs=[row, row, row, _layer(p["a_lam"], layer), _layer(p["a_subln_g"], layer)]
        + [page(i) for i in range(_DEC_PAGES)] + [page(i) for i in range(_DEC_PAGES)],
        out_specs=row,
        scratch_shapes=[pltpu.VMEM((_DEC_ROWS, A_VD), F32), pltpu.VMEM((_DEC_ROWS, 1), F32),
                        pltpu.VMEM((_DEC_ROWS, 1), F32), pltpu.VMEM((_DEC_ROWS, A_VD), F32)],
    )
    heads = lambda a: a.reshape(n, A_HEADS, A_VD)
    return pl.pallas_call(
        functools.partial(_decode_attn_kernel, lam_init=lam_init, past_len=past_len),
        grid_spec=grid_spec,
        out_shape=jax.ShapeDtypeStruct((n, A_HEADS, A_VD), F32),
        compiler_params=_params(("arbitrary", "arbitrary")),
        name="decode_attention",
    )(page_table, heads(q), heads(k_new), heads(v_new), p["a_lam"], p["a_subln_g"],
      *([cache_k] * _DEC_PAGES), *([cache_v] * _DEC_PAGES))


def _sample_tail_kernel(x_ref, o_ref, g0_ref, part_ref, mk_ref, mv_ref, wa_ref, wo_ref, gx_ref, wxq_ref,
                        qg_ref, wxo_ref, gf_ref, wup_ref, wdn_ref, out_ref, x1_scr, q_scr, om_scr):
    b = pl.program_id(0)
    nb = pl.num_programs(0)

    @pl.when(b == 0)
    def _():
        merged = _gated(g0_ref[...], _dot(o_ref[...].astype(BF16), wa_ref[...])) + part_ref[...]
        x1 = x_ref[...] + _dot(merged.astype(BF16), wo_ref[...])
        x1_scr[...] = x1
        q = _dot(_rms(x1, gx_ref[...]).astype(BF16), wxq_ref[...])
        qg = qg_ref[...]
        for h in range(X_HEADS):
            q_scr[h] = _rms(q[:, h * X_HD:(h + 1) * X_HD], qg) * (X_HD ** -0.5)

    n_rows = mk_ref.shape[1]
    row = lax.broadcasted_iota(jnp.int32, (8, X_HD), 0)
    col = lax.broadcasted_iota(jnp.int32, (8, n_rows), 1)
    own_head = (col & (X_HEADS - 1)) == lax.broadcasted_iota(jnp.int32, (8, n_rows), 0)
    for i in range(_TAIL_SEQS):
        seq = b * _TAIL_SEQS + i
        qz = sum(jnp.where(row == h, jnp.broadcast_to(q_scr[h, pl.ds(seq, 1), :], (8, X_HD)), 0.0)
                 for h in range(X_HEADS)).astype(BF16)
        s = jnp.where(own_head, _dot_nt(qz, mk_ref[i].astype(BF16)), NEG_INF)
        p = jnp.exp(s - jnp.max(s, axis=-1, keepdims=True))
        p = p / jnp.sum(p, axis=-1, keepdims=True)
        om = _dot(p.astype(BF16), mv_ref[i].astype(BF16))
        for h in range(X_HEADS):
            om_scr[h, pl.ds(seq, 1), :] = om[h:h + 1, :]

    @pl.when(b == nb - 1)
    def _():
        x2 = x1_scr[...] + sum(_dot(om_scr[h].astype(BF16), wxo_ref[h * X_HD:(h + 1) * X_HD, :])
                               for h in range(X_HEADS))
        out_ref[...] = _ffn(x2, gf_ref[...], wup_ref, wdn_ref)


def _sample_tail(x, o, g0, part, mem_k, mem_v, layer, p):
    n = x.shape[0]
    n_rows = mem_k.shape[2]
    assert n % _TAIL_SEQS == 0
    mem = pl.BlockSpec((None, _TAIL_SEQS, n_rows, X_HD), lambda b: (layer, b, 0, 0))
    ins = [x, o, g0, part]
    ws = [p[k] for k in ("w_a_out", "w_o", "norm_x_g", "w_xq", "x_qnorm_g", "w_xo", "norm_ffn_g", "w_up",
                         "w_down")]
    return pl.pallas_call(
        _sample_tail_kernel,
        grid=(n // _TAIL_SEQS,),
        in_specs=[_full(a.shape) for a in ins] + [mem, mem] + [_layer(w, layer) for w in ws],
        out_specs=_full((n, D_MODEL)),
        out_shape=jax.ShapeDtypeStruct((n, D_MODEL), F32),
        scratch_shapes=[pltpu.VMEM((n, D_MODEL), F32), pltpu.VMEM((X_HEADS, n, X_HD), F32),
                        pltpu.VMEM((X_HEADS, n, X_HD), F32)],
        compiler_params=_params(("arbitrary",)),
        name="sample_tail",
    )(*ins, mem_k, mem_v, *ws)


def _stacked_params(a):
    row = lambda v: v.reshape(v.shape[0], 1, -1)
    b16 = lambda w: w.astype(BF16)
    tiled = lambda v: row(jnp.tile(v, (1, A_W // A_HD)))
    out = {k: b16(a[k]) for k in ("w_o", "w_xq", "w_xk", "w_xv", "w_xo", "w_up", "w_down")}
    halve_gates = jnp.where(jnp.arange(IN_COLS) >= _R_GZ, 0.5, 1.0).astype(F32)
    out["w_in"] = b16(a["w_in"] * halve_gates)
    out.update({k: b16(0.5 * a[k]) for k in ("w_a_out", "w_b_out", "w_c_out", "w_d_out")})
    out.update({k: row(a[k]) for k in ("norm_mix_g", "a_subln_g", "b_conv_b", "c_conv_b", "c_ln_g", "c_ln_b",
                                       "d_ln_g", "d_ln_b", "norm_x_g", "mem_norm_g", "x_qnorm_g", "x_knorm_g",
                                       "norm_ffn_g")})
    out.update({k: a[k] for k in ("a_lam", "b_conv_w", "c_conv_w", "d_ws", "d_bs")})
    out.update(a_qnorm_g=tiled(a["a_qnorm_g"]), a_knorm_g=tiled(a["a_knorm_g"]),
               d_bs_t=jnp.swapaxes(a["d_bs"], 1, 2))
    return out


def kernel(x_prompt, x_sample, cache_k_a, cache_v_a, state_conv_b, state_conv_c, cache_mem_k, cache_mem_v,
           page_table, mem_prompt, norm_mix_g, w_in, a_qnorm_g, a_knorm_g, a_lam, a_subln_g, w_a_out,
           b_conv_w, b_conv_b, w_b_out, c_conv_w, c_conv_b, c_ln_g, c_ln_b, w_c_out, d_ln_g, d_ln_b,
           d_ws, d_bs, w_d_out, w_o, norm_x_g, mem_norm_g, w_xq, w_xk, w_xv, x_qnorm_g, x_knorm_g,
           w_xo, norm_ffn_g, w_up, w_down):
    weights = dict(norm_mix_g=norm_mix_g, w_in=w_in, a_qnorm_g=a_qnorm_g, a_knorm_g=a_knorm_g, a_lam=a_lam,
                   a_subln_g=a_subln_g, w_a_out=w_a_out, b_conv_w=b_conv_w, b_conv_b=b_conv_b,
                   w_b_out=w_b_out, c_conv_w=c_conv_w, c_conv_b=c_conv_b, c_ln_g=c_ln_g, c_ln_b=c_ln_b,
                   w_c_out=w_c_out, d_ln_g=d_ln_g, d_ln_b=d_ln_b, d_ws=d_ws, d_bs=d_bs, w_d_out=w_d_out,
                   w_o=w_o, norm_x_g=norm_x_g, mem_norm_g=mem_norm_g, w_xq=w_xq, w_xk=w_xk, w_xv=w_xv,
                   x_qnorm_g=x_qnorm_g, x_knorm_g=x_knorm_g, w_xo=w_xo, norm_ffn_g=norm_ffn_g,
                   w_up=w_up, w_down=w_down)
    depth = w_in.shape[0]
    bp, t, _ = x_prompt.shape
    ns = x_sample.shape[0]
    n_pool = cache_k_a.shape[1]
    n_mem = cache_mem_k.shape[2]
    cache_k = cache_k_a.reshape(depth, n_pool, _PAGE_ROWS, A_VD)
    cache_v = cache_v_a.reshape(depth, n_pool, _PAGE_ROWS, A_VD)
    mem_k_s = cache_mem_k.reshape(depth, ns, n_mem * X_HEADS, X_HD)
    mem_v_s = cache_mem_v.reshape(depth, ns, n_mem * X_HEADS, X_HD)
    slopes = jnp.asarray([2.0 ** (-8.0 * (i + 1) / A_HEADS) for i in range(A_HEADS)], F32)

    xp = x_prompt
    xs = x_sample.reshape(ns, D_MODEL)
    outs = [[] for _ in range(9)]
    p = _stacked_params(weights)
    for l in range(depth):
        lam_init = 0.8 - 0.6 * math.exp(-0.3 * l)

        mk, mv, mkb, mvb = _memory_kv(mem_prompt, p, l)
        k_p, v_p, qb, kb, vb = _qkv_proj(xp, p, l, depth, None if l == 0 else (k_p, v_p))
        o_p = _diff_attention(slopes, qb, kb, vb, p, l, lam_init)
        xp, cb_p, cc_p = _mixer(xp, o_p, p, l)
        xp = _xattn_ffn(xp, mkb, mvb, p, l)

        sb = state_conv_b[l].reshape(ns, (B_K - 1) * B_W)
        sc = state_conv_c[l].reshape(ns, (C_K - 1) * C_W)
        q_s, k_s, v_s, cb_s, cc_s, dvn_s, part, g0 = _sample_mixer(xs, sb, sc, p, l)
        o_s = _decode_attention(page_table, q_s, k_s, v_s, cache_k, cache_v, p, l, lam_init)
        xs = _sample_tail(xs, o_s.reshape(ns, A_W), g0, part, mem_k_s, mem_v_s, l, p)

        for lst, val in zip(outs, (
                cb_p, cc_p, mk, mv,
                k_s.reshape(ns, 1, A_HEADS, 2 * A_HD), v_s.reshape(ns, 1, A_HEADS, A_VD),
                cb_s.reshape(ns, B_K - 1, B_W), cc_s.reshape(ns, C_K - 1, C_W),
                dvn_s.reshape(ns, 1, D_W))):
            lst.append(val)
    return (xp, xs.reshape(ns, 1, D_MODEL), k_p, v_p) + tuple(jnp.stack(o) for o in outs)
```

```python
import functools
import math

import jax
import jax.numpy as jnp
from jax import lax
from jax.experimental import pallas as pl
from jax.experimental.pallas import tpu as pltpu

F32 = jnp.float32
BF16 = jnp.bfloat16

D_MODEL = 1024
A_HEADS = 4
A_HD = 64
A_VD = 2 * A_HD
A_W = A_HEADS * 2 * A_HD
B_W = 256
B_K = 3
C_W = 256
C_K = 31
D_W = 256
D_GROUPS = 4
CHUNK = 128
N_BRANCH = 4
X_HEADS = 4
X_HD = 128
X_W = X_HEADS * X_HD
D_FF = 4 * D_MODEL
PAGE_SIZE = 128
EPS = 1e-6
NEG_INF = -1e30
LOG2E = math.log2(math.e)

_QKV_COLS = 3 * A_W
_R_BX = _QKV_COLS
_R_CA = _R_BX + 3 * B_W
_R_DU = _R_CA + 2 * C_W
_R_GZ = _R_DU + 2 * D_W
IN_COLS = _R_GZ + N_BRANCH * D_MODEL

_V7X_VMEM_BYTES = 64 * 1024 * 1024
_VMEM_LIMIT = _V7X_VMEM_BYTES - 8 * 1024 * 1024

_TM = 512
_TQ = 512
_TK = 512
_ATT_HEADS = 4
_SUBLANES = 8
_MXU_TILE = 256
_CONV_ROWS = 64
_B_HALO = 8
_C_HALO = 32
_FF_CHUNK = 1024
_DEC_PAGES = 32
_TAIL_SEQS = 4


def _rms(x, g):
    return x * lax.rsqrt(jnp.mean(x * x, axis=-1, keepdims=True) + EPS) * g


def _layer_norm(x, g, b):
    xc = x - jnp.mean(x, axis=-1, keepdims=True)
    var = jnp.mean(xc * xc, axis=-1, keepdims=True)
    return xc * lax.rsqrt(var + EPS) * g + b


def _sigmoid(x):
    return 0.5 * jnp.tanh(0.5 * x) + 0.5


def _gated(t, y_half):
    return t * y_half + y_half


def _dot(a, b):
    return jnp.dot(a, b, preferred_element_type=F32)


def _dot_nt(a, b):
    return lax.dot_general(a, b, (((1,), (1,)), ((), ())), preferred_element_type=F32)


def _idiv(x, d):
    assert d & (d - 1) == 0
    return lax.shift_right_logical(x, int(math.log2(d)))


def _group_mean_matrix(width, group):
    r = _idiv(lax.broadcasted_iota(jnp.int32, (width, width), 0), group)
    c = _idiv(lax.broadcasted_iota(jnp.int32, (width, width), 1), group)
    return jnp.where(r == c, 1.0 / group, 0.0).astype(BF16)


def _group_rms(t, g, gm):
    w = gm.shape[0]
    sq = (t * t).astype(BF16)
    ms = jnp.concatenate([_dot(sq[:, c:c + w], gm) for c in range(0, t.shape[1], w)], axis=1)
    return t * lax.rsqrt(ms + EPS) * g


def _lam(lam_ref, lam_init):
    a = lam_ref[...]
    s1 = jnp.sum(a[0:1] * a[1:2], axis=-1, keepdims=True)
    s2 = jnp.sum(a[2:3] * a[3:4], axis=-1, keepdims=True)
    return jnp.exp(s1) - jnp.exp(s2) + lam_init


def _full(shape):
    return pl.BlockSpec(shape, lambda *_: (0,) * len(shape))


def _layer(arr, l, cols=None):
    shape = arr.shape[1:] if cols is None else arr.shape[1:-1] + (cols,)
    return pl.BlockSpec((None,) + shape, lambda *_: (l,) + (0,) * len(shape))


def _params(sem):
    return pltpu.CompilerParams(dimension_semantics=sem, vmem_limit_bytes=_VMEM_LIMIT)


def _memkv_kernel(mem_ref, g_ref, wk_ref, wv_ref, kg_ref, mk_ref, mv_ref, mkb_ref, mvb_ref):
    h = _rms(mem_ref[...], g_ref[...]).astype(BF16)
    k = _dot(h, wk_ref[...])
    v = _dot(h, wv_ref[...])
    kg = kg_ref[...]
    k = jnp.concatenate([_rms(k[:, i * X_HD:(i + 1) * X_HD], kg) for i in range(X_HEADS)], axis=1)
    mk_ref[...] = k.reshape(mk_ref.shape)
    mv_ref[...] = v.reshape(mv_ref.shape)
    mkb_ref[...] = k.astype(BF16)
    mvb_ref[...] = v.astype(BF16)


def _memory_kv(mem, p, l):
    b, n, _ = mem.shape
    ws = [p["mem_norm_g"], p["w_xk"], p["w_xv"], p["x_knorm_g"]]
    blk = pl.BlockSpec((None, n, X_W), lambda i: (i, 0, 0))
    hblk = pl.BlockSpec((None, n, X_HEADS, X_HD), lambda i: (i, 0, 0, 0))
    return pl.pallas_call(
        _memkv_kernel,
        grid=(b,),
        in_specs=[pl.BlockSpec((None, n, D_MODEL), lambda i: (i, 0, 0))] + [_layer(w, l) for w in ws],
        out_specs=[hblk, hblk, blk, blk],
        out_shape=[jax.ShapeDtypeStruct((b, n, X_HEADS, X_HD), F32),
                   jax.ShapeDtypeStruct((b, n, X_HEADS, X_HD), F32),
                   jax.ShapeDtypeStruct((b, n, X_W), BF16), jax.ShapeDtypeStruct((b, n, X_W), BF16)],
        compiler_params=_params(("arbitrary",)),
        name="memory_kv",
    )(mem, *ws)


def _qkv_from_h(h, w, qg, kg):
    z = _dot(h, w)
    gm = _group_mean_matrix(_MXU_TILE, A_HD)
    q = _group_rms(z[:, :A_W], qg, gm) * (A_HD ** -0.5)
    k = _group_rms(z[:, A_W:2 * A_W], kg, gm)
    v = z[:, 2 * A_W:]
    return q, k, v


def _qkv_kernel(x_ref, g_ref, w_ref, qg_ref, kg_ref, *rest):
    k_ref, v_ref, qb_ref, kb_ref, vb_ref = rest[-5:]
    h = _rms(x_ref[...], g_ref[...]).astype(BF16)
    q, k, v = _qkv_from_h(h, w_ref[...], qg_ref[...], kg_ref[...])
    k_ref[...] = k.reshape(k_ref.shape)
    v_ref[...] = v.reshape(v_ref.shape)
    qb_ref[...] = (q * LOG2E).astype(BF16)
    kb_ref[...] = k.astype(BF16)
    vb_ref[...] = v.astype(BF16)


def _qkv_proj(x, p, layer, depth, stacks):
    b, t, _ = x.shape
    blk = pl.BlockSpec((None, _TM, A_W), lambda i, j: (i, j, 0))
    hblk = pl.BlockSpec((None, None, _TM, A_HEADS, A_VD), lambda i, j: (layer, i, j, 0, 0))
    f32 = jax.ShapeDtypeStruct((depth, b, t, A_HEADS, A_VD), F32)
    b16 = jax.ShapeDtypeStruct((b, t, A_W), BF16)
    ins = [x, p["norm_mix_g"], p["w_in"], p["a_qnorm_g"], p["a_knorm_g"]]
    in_specs = [pl.BlockSpec((None, _TM, D_MODEL), lambda i, j: (i, j, 0)), _layer(ins[1], layer),
                _layer(ins[2], layer, cols=_QKV_COLS), _layer(ins[3], layer), _layer(ins[4], layer)]
    aliases = {}
    if stacks is not None:
        aliases = {len(ins): 0, len(ins) + 1: 1}
        ins += list(stacks)
        in_specs += [pl.BlockSpec(memory_space=pl.ANY)] * 2
    return pl.pallas_call(
        _qkv_kernel,
        grid=(b, t // _TM),
        in_specs=in_specs,
        out_specs=[hblk, hblk, blk, blk, blk],
        out_shape=[f32, f32, b16, b16, b16],
        input_output_aliases=aliases,
        compiler_params=_params(("arbitrary", "arbitrary")),
        name="qkv_proj",
    )(*ins)


def _diff_attn_kernel(slope_ref, q_ref, k_ref, v_ref, lam_ref, sg_ref, o_ref,
                      qs_scr, m_scr, l_scr, acc_scr, *, lam_init):
    hg = pl.program_id(1)
    q0 = pl.program_id(2) * _TQ
    rows = 2 * _TQ

    for hh in range(_ATT_HEADS):
        q = q_ref[:, hh * A_VD:(hh + 1) * A_VD]
        lane = lax.broadcasted_iota(jnp.int32, q.shape, 1)
        zero = jnp.zeros_like(q)
        qs_scr[hh, 0:_TQ, :] = jnp.where(lane < A_HD, q, zero)
        qs_scr[hh, _TQ:rows, :] = jnp.where(lane >= A_HD, q, zero)

    def chunk(j, hh, masked, first=False):
        k0 = pl.multiple_of(j * _TK, _TK)
        head = slice(hh * A_VD, (hh + 1) * A_VD)
        s = _dot_nt(qs_scr[hh], k_ref[pl.ds(k0, _TK), head])
        kpos = k0 + lax.broadcasted_iota(jnp.int32, (1, _TK), 1)
        s = s + (slope_ref[hg * _ATT_HEADS + hh] * LOG2E) * kpos.astype(F32)
        if masked:
            row = lax.broadcasted_iota(jnp.int32, (rows, _TK), 0)
            qpos = q0 + jnp.where(row >= _TQ, row - _TQ, row)
            col = k0 + lax.broadcasted_iota(jnp.int32, (rows, _TK), 1)
            s = jnp.where(qpos >= col, s, NEG_INF)
        m_cur = jnp.max(s, axis=-1, keepdims=True)
        if first:
            m_new = jnp.broadcast_to(m_cur, (rows, A_VD))
        else:
            m_old = m_scr[hh]
            m_new = jnp.maximum(m_old, m_cur)
            alpha = jnp.exp2(m_old - m_new)
        p = jnp.exp2(s - jnp.concatenate([m_new] * (_TK // A_VD), axis=1))
        l_cur = jnp.sum(p, axis=-1, keepdims=True)
        pv = _dot(p.astype(BF16), v_ref[pl.ds(k0, _TK), head])
        if first:
            l_scr[hh] = jnp.broadcast_to(l_cur, (rows, A_VD))
            acc_scr[hh] = pv
        else:
            l_scr[hh] = alpha * l_scr[hh] + l_cur
            acc_scr[hh] = alpha * acc_scr[hh] + pv
        m_scr[hh] = m_new

    n_full = q0 // _TK

    def body(j, carry):
        for hh in range(_ATT_HEADS):
            chunk(j, hh, False)
        return carry

    @pl.when(n_full == 0)
    def _():
        for d in range(_TQ // _TK):
            for hh in range(_ATT_HEADS):
                chunk(d, hh, True, first=d == 0)

    @pl.when(n_full > 0)
    def _():
        for hh in range(_ATT_HEADS):
            chunk(0, hh, False, first=True)
        lax.fori_loop(1, n_full, body, 0)
        for d in range(_TQ // _TK):
            for hh in range(_ATT_HEADS):
                chunk(n_full + d, hh, True)

    lam = _lam(lam_ref, lam_init)
    for hh in range(_ATT_HEADS):
        o = acc_scr[hh] / l_scr[hh]
        o = o[0:_TQ] - lam * o[_TQ:rows]
        o_ref[:, hh * A_VD:(hh + 1) * A_VD] = (_rms(o, sg_ref[...]) * (1.0 - lam_init)).astype(BF16)


def _diff_attention(slopes, qb, kb, vb, p, l, lam_init):
    b, t, _ = qb.shape
    assert _TQ % _TK == 0 and t % _TQ == 0 and A_HEADS % _ATT_HEADS == 0
    width = _ATT_HEADS * A_VD
    qblk = pl.BlockSpec((None, _TQ, width), lambda bi, h, qi: (bi, qi, h))
    kvblk = pl.BlockSpec((None, t, width), lambda bi, h, qi: (bi, 0, h))
    stat = pltpu.VMEM((_ATT_HEADS, 2 * _TQ, A_VD), F32)
    return pl.pallas_call(
        functools.partial(_diff_attn_kernel, lam_init=lam_init),
        grid=(b, A_HEADS // _ATT_HEADS, t // _TQ),
        in_specs=[pl.BlockSpec(memory_space=pltpu.SMEM), qblk, kvblk, kvblk,
                  _layer(p["a_lam"], l), _layer(p["a_subln_g"], l)],
        out_specs=qblk,
        out_shape=jax.ShapeDtypeStruct((b, t, A_W), BF16),
        scratch_shapes=[pltpu.VMEM((_ATT_HEADS, 2 * _TQ, A_VD), BF16), stat, stat, stat],
        compiler_params=_params(("arbitrary",) * 3),
        name="diff_attention",
    )(slopes, qb, kb, vb, p["a_lam"], p["a_subln_g"])


def _lane_group_mask(width, group, g):
    lane = lax.broadcasted_iota(jnp.int32, (1, width), 1)
    return (_idiv(lane, group) == g).astype(F32)


def _mixer_kernel(x_ref, o_ref, g_ref, wr_ref, wa_ref, wb_ref, wc_ref, wd_ref, wo_ref,
                  bcw_ref, bcb_ref, ccw_ref, ccb_ref, clg_ref, clb_ref, dlg_ref, dlb_ref,
                  dws_ref, dbst_ref, x1_ref, cbp_ref, ccp_ref, ubuf, cbuf, ycin, ydin, shwin, gates):
    tm = _TM

    @pl.when(pl.program_id(1) == 0)
    def _():
        ubuf[0:_B_HALO, :] = jnp.zeros((_B_HALO, B_W), F32)
        cbuf[0:_C_HALO, :] = jnp.zeros((_C_HALO, C_W), F32)

    x = x_ref[...]
    h = _rms(x, g_ref[...]).astype(BF16)

    zc = _dot(h, wr_ref[:, _R_CA:_R_DU])
    cbuf[_C_HALO:_C_HALO + tm, :] = zc[:, 0:256] * _sigmoid(zc[:, 256:512])
    base = _C_HALO - (C_K - 1)
    gcols = N_BRANCH * D_MODEL // (tm // _CONV_ROWS)
    bias = ccb_ref[...]
    for it, r in enumerate(range(0, tm, _CONV_ROWS)):
        c0 = _R_GZ + it * gcols
        t = jnp.tanh(_dot(h, wr_ref[:, c0:c0 + gcols]))
        gates[:, it * gcols:(it + 1) * gcols] = t
        acc = jnp.broadcast_to(bias, (_CONV_ROWS, C_W))
        bias = ccb_ref[...] + 0.0 * t[0:1, 0:C_W]
        for ph in range(_SUBLANES):
            n_taps = (C_K - 1 - ph) // _SUBLANES + 1
            rows = _CONV_ROWS + _SUBLANES * (n_taps - 1)
            shwin[ph, 0:rows, :] = cbuf[base + r + ph:base + r + ph + rows, :]
            for i in range(n_taps):
                k = ph + _SUBLANES * i
                acc = acc + ccw_ref[k:k + 1, :] * shwin[ph, _SUBLANES * i:_SUBLANES * i + _CONV_ROWS, :]
        y = _layer_norm(acc, clg_ref[...], clb_ref[...])
        ycin[r:r + _CONV_ROWS, :] = (y * _sigmoid(y)).astype(BF16)
    ccp_ref[...] = cbuf[_C_HALO + tm - (C_K - 1):_C_HALO + tm, :]
    cbuf[0:_C_HALO, :] = cbuf[tm:tm + _C_HALO, :]

    zd = jax.nn.gelu(_dot(h, wr_ref[:, _R_DU:_R_GZ]))
    du = zd[:, 0:256]
    dvn = _layer_norm(zd[:, 256:512], dlg_ref[...], dlb_ref[...])
    row = lax.broadcasted_iota(jnp.int32, (CHUNK, CHUNK), 0)
    col = lax.broadcasted_iota(jnp.int32, (CHUNK, CHUNK), 1)
    tril = (row >= col).astype(F32)
    wcat = jnp.concatenate([dws_ref[g] * tril for g in range(D_GROUPS)], axis=1).astype(BF16)
    masks = [_lane_group_mask(D_W, D_W // D_GROUPS, g) for g in range(D_GROUPS)]
    bsmat = sum(dbst_ref[:, g:g + 1] * masks[g] for g in range(D_GROUPS))
    for c in range(0, tm, CHUNK):
        vch = dvn[c:c + CHUNK]
        rhs = jnp.concatenate([(vch * masks[g]).astype(BF16) for g in range(D_GROUPS)], axis=0)
        s = _dot(wcat, rhs) + bsmat
        ydin[c:c + CHUNK, :] = (du[c:c + CHUNK] * s).astype(BF16)

    def gate(i):
        return gates[:, i * D_MODEL:(i + 1) * D_MODEL]

    merged = _gated(gate(2), _dot(ycin[...], wc_ref[...]))

    merged = merged + _gated(gate(0), _dot(o_ref[...], wa_ref[...]))

    zb = _dot(h, wr_ref[:, _R_BX:_R_CA])
    u = zb[:, 512:768] * zb[:, 0:256]
    ubuf[_B_HALO:_B_HALO + tm, :] = u
    conv_b = (bcw_ref[0:1, :] * ubuf[_B_HALO - 2:_B_HALO - 2 + tm, :]
              + bcw_ref[1:2, :] * ubuf[_B_HALO - 1:_B_HALO - 1 + tm, :]
              + bcw_ref[2:3, :] * u + bcb_ref[...])
    yb = _dot((zb[:, 256:512] * conv_b).astype(BF16), wb_ref[...])
    merged = merged + _gated(gate(1), yb)
    cbp_ref[...] = ubuf[_B_HALO + tm - (B_K - 1):_B_HALO + tm, :]
    ubuf[0:_B_HALO, :] = ubuf[tm:tm + _B_HALO, :]

    merged = merged + _gated(gate(3), _dot(ydin[...], wd_ref[...]))

    x1_ref[...] = x + _dot(merged.astype(BF16), wo_ref[...])


def _mixer(x, o, p, l):
    b, t, _ = x.shape
    ws = [p[k] for k in ("norm_mix_g", "w_in", "w_a_out", "w_b_out", "w_c_out", "w_d_out", "w_o", "b_conv_w",
                         "b_conv_b", "c_conv_w", "c_conv_b", "c_ln_g", "c_ln_b", "d_ln_g", "d_ln_b", "d_ws",
                         "d_bs_t")]
    row = lambda w: pl.BlockSpec((None, _TM, w), lambda i, j: (i, j, 0))
    return pl.pallas_call(
        _mixer_kernel,
        grid=(b, t // _TM),
        in_specs=[row(D_MODEL), row(A_W)] + [_layer(w, l) for w in ws],
        out_specs=[row(D_MODEL),
                   pl.BlockSpec((None, B_K - 1, B_W), lambda i, j: (i, 0, 0)),
                   pl.BlockSpec((None, C_K - 1, C_W), lambda i, j: (i, 0, 0))],
        out_shape=[jax.ShapeDtypeStruct((b, t, D_MODEL), F32),
                   jax.ShapeDtypeStruct((b, B_K - 1, B_W), F32),
                   jax.ShapeDtypeStruct((b, C_K - 1, C_W), F32)],
        scratch_shapes=[pltpu.VMEM((_B_HALO + _TM, B_W), F32), pltpu.VMEM((_C_HALO + _TM, C_W), F32),
                        pltpu.VMEM((_TM, C_W), BF16), pltpu.VMEM((_TM, D_W), BF16),
                        pltpu.VMEM((_SUBLANES, _CONV_ROWS + _C_HALO, C_W), F32),
                        pltpu.VMEM((_TM, N_BRANCH * D_MODEL), F32)],
        compiler_params=_params(("arbitrary", "arbitrary")),
        name="mixer",
    )(x, o, *ws)


def _ffn(x2, gf, wup_ref, wdn_ref):
    h3 = _rms(x2, gf).astype(BF16)
    acc = jnp.zeros(x2.shape, F32)
    for c in range(0, D_FF, _FF_CHUNK):
        a = jnp.maximum(_dot(h3, wup_ref[:, c:c + _FF_CHUNK]), 0.0)
        acc = acc + _dot((a * a).astype(BF16), wdn_ref[c:c + _FF_CHUNK, :])
    return x2 + acc


def _cross_attention(x, mk_ref, mv_ref, gx, wxq_ref, qg, wxo_ref):
    h2 = _rms(x, gx).astype(BF16)
    q = _dot(h2, wxq_ref[...])
    oms = []
    for i in range(X_HEADS):
        sl = slice(i * X_HD, (i + 1) * X_HD)
        qh = (_rms(q[:, sl], qg) * (X_HD ** -0.5)).astype(BF16)
        s = _dot_nt(qh, mk_ref[:, sl])
        p = jnp.exp(s - jnp.max(s, axis=-1, keepdims=True))
        oms.append(_dot(p.astype(BF16), mv_ref[:, sl]) * (1.0 / jnp.sum(p, axis=-1, keepdims=True)))
    om = jnp.concatenate(oms, axis=1).astype(BF16)
    return x + _dot(om, wxo_ref[...])


def _xffn_kernel(x_ref, mk_ref, mv_ref, gx_ref, wxq_ref, qg_ref, wxo_ref, gf_ref, wup_ref, wdn_ref,
                 out_ref):
    x2 = _cross_attention(x_ref[...], mk_ref, mv_ref, gx_ref[...], wxq_ref, qg_ref[...], wxo_ref)
    out_ref[...] = _ffn(x2, gf_ref[...], wup_ref, wdn_ref)


def _xattn_ffn(x, mkb, mvb, p, l):
    b, t, _ = x.shape
    ws = [p[k] for k in ("norm_x_g", "w_xq", "x_qnorm_g", "w_xo", "norm_ffn_g", "w_up", "w_down")]
    n = mkb.shape[1]
    row = pl.BlockSpec((None, _TM, D_MODEL), lambda i, j: (i, j, 0))
    mem = pl.BlockSpec((None, n, X_W), lambda i, j: (i, 0, 0))
    return pl.pallas_call(
        _xffn_kernel,
        grid=(b, t // _TM),
        in_specs=[row, mem, mem] + [_layer(w, l) for w in ws],
        out_specs=row,
        out_shape=jax.ShapeDtypeStruct((b, t, D_MODEL), F32),
        compiler_params=_params(("arbitrary", "arbitrary")),
        name="xattn_ffn",
    )(x, mkb, mvb, *ws)


def _sample_mixer_kernel(x_ref, sb_ref, sc_ref, g_ref, wr_ref, qg_ref, kg_ref,
                         wb_ref, wc_ref, wd_ref, bcw_ref, bcb_ref, ccw_ref, ccb_ref, clg_ref, clb_ref,
                         dlg_ref, dlb_ref, dws_ref, dbs_ref,
                         q_ref, k_ref, v_ref, cb_ref, cc_ref, dvn_ref, part_ref, g0_ref):
    x = x_ref[...]
    h = _rms(x, g_ref[...]).astype(BF16)
    q, k, v = _qkv_from_h(h, wr_ref[:, 0:_QKV_COLS], qg_ref[...], kg_ref[...])
    q_ref[...] = q
    k_ref[...] = k
    v_ref[...] = v

    def gate(i):
        return jnp.tanh(_dot(h, wr_ref[:, _R_GZ + i * D_MODEL:_R_GZ + (i + 1) * D_MODEL]))

    g0_ref[...] = gate(0)

    zb = _dot(h, wr_ref[:, _R_BX:_R_CA])
    u = zb[:, 512:768] * zb[:, 0:256]
    conv_b = (bcw_ref[0:1, :] * sb_ref[:, 0:B_W] + bcw_ref[1:2, :] * sb_ref[:, B_W:2 * B_W]
              + bcw_ref[2:3, :] * u + bcb_ref[...])
    part = _gated(gate(1), _dot((zb[:, 256:512] * conv_b).astype(BF16), wb_ref[...]))
    cb_ref[:, 0:B_W] = sb_ref[:, B_W:2 * B_W]
    cb_ref[:, B_W:2 * B_W] = u

    zc = _dot(h, wr_ref[:, _R_CA:_R_DU])
    uc = zc[:, 0:256] * _sigmoid(zc[:, 256:512])
    acc = ccw_ref[C_K - 1:C_K, :] * uc + ccb_ref[...]
    for kk in range(C_K - 1):
        acc = acc + ccw_ref[kk:kk + 1, :] * sc_ref[:, kk * C_W:(kk + 1) * C_W]
    y = _layer_norm(acc, clg_ref[...], clb_ref[...])
    yc = _dot((y * _sigmoid(y)).astype(BF16), wc_ref[...])
    part = part + _gated(gate(2), yc)
    cc_ref[:, 0:(C_K - 2) * C_W] = sc_ref[:, C_W:(C_K - 1) * C_W]
    cc_ref[:, (C_K - 2) * C_W:(C_K - 1) * C_W] = uc

    zd = jax.nn.gelu(_dot(h, wr_ref[:, _R_DU:_R_GZ]))
    dvn = _layer_norm(zd[:, 256:512], dlg_ref[...], dlb_ref[...])
    dvn_ref[...] = dvn
    w00 = sum(dws_ref[g, 0:1, 0:1] * _lane_group_mask(D_W, D_W // D_GROUPS, g) for g in range(D_GROUPS))
    b0 = sum(dbs_ref[g:g + 1, 0:1] * _lane_group_mask(D_W, D_W // D_GROUPS, g) for g in range(D_GROUPS))
    yd = _dot((zd[:, 0:256] * (w00 * dvn + b0)).astype(BF16), wd_ref[...])
    part_ref[...] = part + _gated(gate(3), yd)


def _sample_mixer(x, sb, sc, p, l):
    n = x.shape[0]
    shapes = [(n, A_W), (n, A_W), (n, A_W), (n, (B_K - 1) * B_W), (n, (C_K - 1) * C_W), (n, D_W),
              (n, D_MODEL), (n, D_MODEL)]
    ins = [x, sb, sc]
    ws = [p[k] for k in ("norm_mix_g", "w_in", "a_qnorm_g", "a_knorm_g", "w_b_out", "w_c_out", "w_d_out",
                         "b_conv_w", "b_conv_b", "c_conv_w", "c_conv_b", "c_ln_g", "c_ln_b", "d_ln_g", "d_ln_b",
                         "d_ws", "d_bs")]
    return pl.pallas_call(
        _sample_mixer_kernel,
        grid=(1,),
        in_specs=[_full(a.shape) for a in ins] + [_layer(w, l) for w in ws],
        out_specs=[_full(s) for s in shapes],
        out_shape=[jax.ShapeDtypeStruct(s, F32) for s in shapes],
        compiler_params=_params(("arbitrary",)),
        name="sample_mixer",
    )(*ins, *ws)


_DEC_ROWS = 2 * A_HEADS
_PAGE_ROWS = PAGE_SIZE * A_HEADS


def _head_rows(ref, row, per_head):
    return sum(jnp.where(_idiv(row, per_head) == h, jnp.broadcast_to(ref[h:h + 1, :], row.shape), 0.0)
               for h in range(ref.shape[0]))


def _decode_init(q_ref, qz_scr, m_scr, l_scr, acc_scr):
    row = lax.broadcasted_iota(jnp.int32, (_DEC_ROWS, A_VD), 0)
    lane = lax.broadcasted_iota(jnp.int32, (_DEC_ROWS, A_VD), 1)
    qz_scr[...] = jnp.where(_idiv(lane, A_HD) == (row & 1), _head_rows(q_ref, row, 2), 0.0)
    m_scr[...] = jnp.full(m_scr.shape, NEG_INF, F32)
    l_scr[...] = jnp.zeros(l_scr.shape, F32)
    acc_scr[...] = jnp.zeros(acc_scr.shape, F32)


def _decode_pages(j, k_refs, v_refs, qz_scr, m_scr, l_scr, acc_scr, past_len):
    n_pages = len(k_refs)
    width = n_pages * _PAGE_ROWS
    rowv = lax.broadcasted_iota(jnp.int32, (_DEC_ROWS, 1), 0)
    slope = sum(jnp.where(_idiv(rowv, 2) == i, 2.0 ** (-8.0 * (i + 1) / A_HEADS), 0.0)
                for i in range(A_HEADS))
    qz = qz_scr[...].astype(BF16)
    s = jnp.concatenate([_dot_nt(qz, k_refs[i][...].astype(BF16)) for i in range(n_pages)], axis=1)
    col = lax.broadcasted_iota(jnp.int32, (_DEC_ROWS, width), 1)
    kpos = j * (n_pages * PAGE_SIZE) + _idiv(lax.broadcasted_iota(jnp.int32, (1, width), 1), A_HEADS)
    own_head = (col & (A_HEADS - 1)) == _idiv(lax.broadcasted_iota(jnp.int32, (_DEC_ROWS, width), 0), 2)
    s = jnp.where(own_head, s - slope * (past_len - kpos).astype(F32), NEG_INF)
    m_old = m_scr[...]
    m_new = jnp.maximum(m_old, jnp.max(s, axis=-1, keepdims=True))
    alpha = jnp.exp(m_old - m_new)
    p = jnp.exp(s - m_new)
    l_scr[...] = alpha * l_scr[...] + jnp.sum(p, axis=-1, keepdims=True)
    p = p.astype(BF16)
    pv = sum(_dot(p[:, i * _PAGE_ROWS:(i + 1) * _PAGE_ROWS], v_refs[i][...].astype(BF16))
             for i in range(n_pages))
    acc_scr[...] = alpha * acc_scr[...] + pv
    m_scr[...] = m_new


def _decode_finish(kn_ref, vn_ref, lam_ref, sg_ref, o_ref, qz_scr, m_scr, l_scr, acc_scr, lam_init):
    row = lax.broadcasted_iota(jnp.int32, (_DEC_ROWS, A_VD), 0)
    s_new = jnp.sum(qz_scr[...] * _head_rows(kn_ref, row, 2), axis=-1, keepdims=True)
    m_old = m_scr[...]
    m_fin = jnp.maximum(m_old, s_new)
    alpha = jnp.exp(m_old - m_fin)
    p_new = jnp.exp(s_new - m_fin)
    l_fin = alpha * l_scr[...] + p_new
    o_all = (alpha * acc_scr[...] + p_new * _head_rows(vn_ref, row, 2)) / l_fin
    lam = _lam(lam_ref, lam_init)
    sg = sg_ref[...]
    for h in range(A_HEADS):
        o = o_all[2 * h:2 * h + 1] - lam * o_all[2 * h + 1:2 * h + 2]
        o_ref[h:h + 1, :] = _rms(o, sg) * (1.0 - lam_init)


def _decode_attn_kernel(pt_ref, q_ref, kn_ref, vn_ref, lam_ref, sg_ref, *rest, lam_init, past_len):
    del pt_ref
    k_refs = rest[:_DEC_PAGES]
    v_refs = rest[_DEC_PAGES:2 * _DEC_PAGES]
    o_ref = rest[2 * _DEC_PAGES]
    dec = rest[2 * _DEC_PAGES + 1:]
    j = pl.program_id(1)

    @pl.when(j == 0)
    def _():
        _decode_init(q_ref, *dec)

    _decode_pages(j, k_refs, v_refs, *dec, past_len)

    @pl.when(j == pl.num_programs(1) - 1)
    def _():
        _decode_finish(kn_ref, vn_ref, lam_ref, sg_ref, o_ref, *dec, lam_init)


def _decode_attention(page_table, q, k_new, v_new, cache_k, cache_v, p, layer, lam_init):
    n, n_pages = page_table.shape
    past_len = n_pages * PAGE_SIZE
    row = pl.BlockSpec((None, A_HEADS, A_VD), lambda b, j, pt: (b, 0, 0))

    def page(i):
        return pl.BlockSpec((None, None, _PAGE_ROWS, A_VD),
                            lambda b, j, pt: (layer, pt[b, j * _DEC_PAGES + i], 0, 0))

    grid_spec = pltpu.PrefetchScalarGridSpec(
        num_scalar_prefetch=1,
        grid=(n, n_pages // _DEC_PAGES),
        in_specs=[row, row, row, _layer(p["a_lam"], layer), _layer(p["a_subln_g"], layer)]
        + [page(i) for i in range(_DEC_PAGES)] + [page(i) for i in range(_DEC_PAGES)],
        out_specs=row,
        scratch_shapes=[pltpu.VMEM((_DEC_ROWS, A_VD), F32), pltpu.VMEM((_DEC_ROWS, 1), F32),
                        pltpu.VMEM((_DEC_ROWS, 1), F32), pltpu.VMEM((_DEC_ROWS, A_VD), F32)],
    )
    heads = lambda a: a.reshape(n, A_HEADS, A_VD)
    return pl.pallas_call(
        functools.partial(_decode_attn_kernel, lam_init=lam_init, past_len=past_len),
        grid_spec=grid_spec,
        out_shape=jax.ShapeDtypeStruct((n, A_HEADS, A_VD), F32),
        compiler_params=_params(("arbitrary", "arbitrary")),
        name="decode_attention",
    )(page_table, heads(q), heads(k_new), heads(v_new), p["a_lam"], p["a_subln_g"],
      *([cache_k] * _DEC_PAGES), *([cache_v] * _DEC_PAGES))


def _sample_tail_kernel(x_ref, o_ref, g0_ref, part_ref, mk_ref, mv_ref, wa_ref, wo_ref, gx_ref, wxq_ref,
                        qg_ref, wxo_ref, gf_ref, wup_ref, wdn_ref, out_ref, x1_scr, q_scr, om_scr):
    b = pl.program_id(0)
    nb = pl.num_programs(0)

    @pl.when(b == 0)
    def _():
        merged = _gated(g0_ref[...], _dot(o_ref[...].astype(BF16), wa_ref[...])) + part_ref[...]
        x1 = x_ref[...] + _dot(merged.astype(BF16), wo_ref[...])
        x1_scr[...] = x1
        q = _dot(_rms(x1, gx_ref[...]).astype(BF16), wxq_ref[...])
        qg = qg_ref[...]
        for h in range(X_HEADS):
            q_scr[h] = _rms(q[:, h * X_HD:(h + 1) * X_HD], qg) * (X_HD ** -0.5)

    n_rows = mk_ref.shape[1]
    row = lax.broadcasted_iota(jnp.int32, (8, X_HD), 0)
    col = lax.broadcasted_iota(jnp.int32, (8, n_rows), 1)
    own_head = (col & (X_HEADS - 1)) == lax.broadcasted_iota(jnp.int32, (8, n_rows), 0)
    for i in range(_TAIL_SEQS):
        seq = b * _TAIL_SEQS + i
        qz = sum(jnp.where(row == h, jnp.broadcast_to(q_scr[h, pl.ds(seq, 1), :], (8, X_HD)), 0.0)
                 for h in range(X_HEADS)).astype(BF16)
        s = jnp.where(own_head, _dot_nt(qz, mk_ref[i].astype(BF16)), NEG_INF)
        p = jnp.exp(s - jnp.max(s, axis=-1, keepdims=True))
        p = p / jnp.sum(p, axis=-1, keepdims=True)
        om = _dot(p.astype(BF16), mv_ref[i].astype(BF16))
        for h in range(X_HEADS):
            om_scr[h, pl.ds(seq, 1), :] = om[h:h + 1, :]

    @pl.when(b == nb - 1)
    def _():
        x2 = x1_scr[...] + sum(_dot(om_scr[h].astype(BF16), wxo_ref[h * X_HD:(h + 1) * X_HD, :])
                               for h in range(X_HEADS))
        out_ref[...] = _ffn(x2, gf_ref[...], wup_ref, wdn_ref)


def _sample_tail(x, o, g0, part, mem_k, mem_v, layer, p):
    n = x.shape[0]
    n_rows = mem_k.shape[2]
    assert n % _TAIL_SEQS == 0
    mem = pl.BlockSpec((None, _TAIL_SEQS, n_rows, X_HD), lambda b: (layer, b, 0, 0))
    ins = [x, o, g0, part]
    ws = [p[k] for k in ("w_a_out", "w_o", "norm_x_g", "w_xq", "x_qnorm_g", "w_xo", "norm_ffn_g", "w_up",
                         "w_down")]
    return pl.pallas_call(
        _sample_tail_kernel,
        grid=(n // _TAIL_SEQS,),
        in_specs=[_full(a.shape) for a in ins] + [mem, mem] + [_layer(w, layer) for w in ws],
        out_specs=_full((n, D_MODEL)),
        out_shape=jax.ShapeDtypeStruct((n, D_MODEL), F32),
        scratch_shapes=[pltpu.VMEM((n, D_MODEL), F32), pltpu.VMEM((X_HEADS, n, X_HD), F32),
                        pltpu.VMEM((X_HEADS, n, X_HD), F32)],
        compiler_params=_params(("arbitrary",)),
        name="sample_tail",
    )(*ins, mem_k, mem_v, *ws)


def _stacked_params(a):
    row = lambda v: v.reshape(v.shape[0], 1, -1)
    b16 = lambda w: w.astype(BF16)
    tiled = lambda v: row(jnp.tile(v, (1, A_W // A_HD)))
    out = {k: b16(a[k]) for k in ("w_o", "w_xq", "w_xk", "w_xv", "w_xo", "w_up", "w_down")}
    halve_gates = jnp.where(jnp.arange(IN_COLS) >= _R_GZ, 0.5, 1.0).astype(F32)
    out["w_in"] = b16(a["w_in"] * halve_gates)
    out.update({k: b16(0.5 * a[k]) for k in ("w_a_out", "w_b_out", "w_c_out", "w_d_out")})
    out.update({k: row(a[k]) for k in ("norm_mix_g", "a_subln_g", "b_conv_b", "c_conv_b", "c_ln_g", "c_ln_b",
                                       "d_ln_g", "d_ln_b", "norm_x_g", "mem_norm_g", "x_qnorm_g", "x_knorm_g",
                                       "norm_ffn_g")})
    out.update({k: a[k] for k in ("a_lam", "b_conv_w", "c_conv_w", "d_ws", "d_bs")})
    out.update(a_qnorm_g=tiled(a["a_qnorm_g"]), a_knorm_g=tiled(a["a_knorm_g"]),
               d_bs_t=jnp.swapaxes(a["d_bs"], 1, 2))
    return out


def kernel(x_prompt, x_sample, cache_k_a, cache_v_a, state_conv_b, state_conv_c, cache_mem_k, cache_mem_v,
           page_table, mem_prompt, norm_mix_g, w_in, a_qnorm_g, a_knorm_g, a_lam, a_subln_g, w_a_out,
           b_conv_w, b_conv_b, w_b_out, c_conv_w, c_conv_b, c_ln_g, c_ln_b, w_c_out, d_ln_g, d_ln_b,
           d_ws, d_bs, w_d_out, w_o, norm_x_g, mem_norm_g, w_xq, w_xk, w_xv, x_qnorm_g, x_knorm_g,
           w_xo, norm_ffn_g, w_up, w_down):
    weights = dict(norm_mix_g=norm_mix_g, w_in=w_in, a_qnorm_g=a_qnorm_g, a_knorm_g=a_knorm_g, a_lam=a_lam,
                   a_subln_g=a_subln_g, w_a_out=w_a_out, b_conv_w=b_conv_w, b_conv_b=b_conv_b,
                   w_b_out=w_b_out, c_conv_w=c_conv_w, c_conv_b=c_conv_b, c_ln_g=c_ln_g, c_ln_b=c_ln_b,
                   w_c_out=w_c_out, d_ln_g=d_ln_g, d_ln_b=d_ln_b, d_ws=d_ws, d_bs=d_bs, w_d_out=w_d_out,
                   w_o=w_o, norm_x_g=norm_x_g, mem_norm_g=mem_norm_g, w_xq=w_xq, w_xk=w_xk, w_xv=w_xv,
                   x_qnorm_g=x_qnorm_g, x_knorm_g=x_knorm_g, w_xo=w_xo, norm_ffn_g=norm_ffn_g,
                   w_up=w_up, w_down=w_down)
    depth = w_in.shape[0]
    bp, t, _ = x_prompt.shape
    ns = x_sample.shape[0]
    n_pool = cache_k_a.shape[1]
    n_mem = cache_mem_k.shape[2]
    cache_k = cache_k_a.reshape(depth, n_pool, _PAGE_ROWS, A_VD)
    cache_v = cache_v_a.reshape(depth, n_pool, _PAGE_ROWS, A_VD)
    mem_k_s = cache_mem_k.reshape(depth, ns, n_mem * X_HEADS, X_HD)
    mem_v_s = cache_mem_v.reshape(depth, ns, n_mem * X_HEADS, X_HD)
    slopes = jnp.asarray([2.0 ** (-8.0 * (i + 1) / A_HEADS) for i in range(A_HEADS)], F32)

    xp = x_prompt
    xs = x_sample.reshape(ns, D_MODEL)
    outs = [[] for _ in range(9)]
    p = _stacked_params(weights)
    for l in range(depth):
        lam_init = 0.8 - 0.6 * math.exp(-0.3 * l)

        mk, mv, mkb, mvb = _memory_kv(mem_prompt, p, l)
        k_p, v_p, qb, kb, vb = _qkv_proj(xp, p, l, depth, None if l == 0 else (k_p, v_p))
        o_p = _diff_attention(slopes, qb, kb, vb, p, l, lam_init)
        xp, cb_p, cc_p = _mixer(xp, o_p, p, l)
        xp = _xattn_ffn(xp, mkb, mvb, p, l)

        sb = state_conv_b[l].reshape(ns, (B_K - 1) * B_W)
        sc = state_conv_c[l].reshape(ns, (C_K - 1) * C_W)
        q_s, k_s, v_s, cb_s, cc_s, dvn_s, part, g0 = _sample_mixer(xs, sb, sc, p, l)
        o_s = _decode_attention(page_table, q_s, k_s, v_s, cache_k, cache_v, p, l, lam_init)
        xs = _sample_tail(xs, o_s.reshape(ns, A_W), g0, part, mem_k_s, mem_v_s, l, p)

        for lst, val in zip(outs, (
                cb_p, cc_p, mk, mv,
                k_s.reshape(ns, 1, A_HEADS, 2 * A_HD), v_s.reshape(ns, 1, A_HEADS, A_VD),
                cb_s.reshape(ns, B_K - 1, B_W), cc_s.reshape(ns, C_K - 1, C_W),
                dvn_s.reshape(ns, 1, D_W))):
            lst.append(val)
    return (xp, xs.reshape(ns, 1, D_MODEL), k_p, v_p) + tuple(jnp.stack(o) for o in outs)
```

```python
import functools
import math

import jax
import jax.numpy as jnp
from jax import lax
from jax.experimental import pallas as pl
from jax.experimental.pallas import tpu as pltpu

F32 = jnp.float32
BF16 = jnp.bfloat16

D_MODEL = 1024
A_HEADS = 4
A_HD = 64
A_VD = 2 * A_HD
A_W = A_HEADS * 2 * A_HD
B_W = 256
B_K = 3
C_W = 256
C_K = 31
D_W = 256
D_GROUPS = 4
CHUNK = 128
N_BRANCH = 4
X_HEADS = 4
X_HD = 128
X_W = X_HEADS * X_HD
D_FF = 4 * D_MODEL
PAGE_SIZE = 128
EPS = 1e-6
NEG_INF = -1e30
LOG2E = math.log2(math.e)

_QKV_COLS = 3 * A_W
_R_BX = _QKV_COLS
_R_CA = _R_BX + 3 * B_W
_R_DU = _R_CA + 2 * C_W
_R_GZ = _R_DU + 2 * D_W
IN_COLS = _R_GZ + N_BRANCH * D_MODEL

_V7X_VMEM_BYTES = 64 * 1024 * 1024
_VMEM_LIMIT = _V7X_VMEM_BYTES - 8 * 1024 * 1024

_TM = 512
_TQ = 512
_TK = 512
_ATT_HEADS = 4
_SUBLANES = 8
_MXU_TILE = 256
_CONV_ROWS = 64
_B_HALO = 8
_C_HALO = 32
_FF_CHUNK = 1024
_DEC_PAGES = 32
_TAIL_SEQS = 4


def _rms(x, g):
    return x * lax.rsqrt(jnp.mean(x * x, axis=-1, keepdims=True) + EPS) * g


def _layer_norm(x, g, b):
    xc = x - jnp.mean(x, axis=-1, keepdims=True)
    var = jnp.mean(xc * xc, axis=-1, keepdims=True)
    return xc * lax.rsqrt(var + EPS) * g + b


def _sigmoid(x):
    return 0.5 * jnp.tanh(0.5 * x) + 0.5


def _gated(t, y_half):
    return t * y_half + y_half


def _dot(a, b):
    return jnp.dot(a, b, preferred_element_type=F32)


def _dot_nt(a, b):
    return lax.dot_general(a, b, (((1,), (1,)), ((), ())), preferred_element_type=F32)


def _idiv(x, d):
    assert d & (d - 1) == 0
    return lax.shift_right_logical(x, int(math.log2(d)))


def _group_mean_matrix(width, group):
    r = _idiv(lax.broadcasted_iota(jnp.int32, (width, width), 0), group)
    c = _idiv(lax.broadcasted_iota(jnp.int32, (width, width), 1), group)
    return jnp.where(r == c, 1.0 / group, 0.0).astype(BF16)


def _group_rms(t, g, gm):
    w = gm.shape[0]
    sq = (t * t).astype(BF16)
    ms = jnp.concatenate([_dot(sq[:, c:c + w], gm) for c in range(0, t.shape[1], w)], axis=1)
    return t * lax.rsqrt(ms + EPS) * g


def _lam(lam_ref, lam_init):
    a = lam_ref[...]
    s1 = jnp.sum(a[0:1] * a[1:2], axis=-1, keepdims=True)
    s2 = jnp.sum(a[2:3] * a[3:4], axis=-1, keepdims=True)
    return jnp.exp(s1) - jnp.exp(s2) + lam_init


def _full(shape):
    return pl.BlockSpec(shape, lambda *_: (0,) * len(shape))


def _layer(arr, l, cols=None):
    shape = arr.shape[1:] if cols is None else arr.shape[1:-1] + (cols,)
    return pl.BlockSpec((None,) + shape, lambda *_: (l,) + (0,) * len(shape))


def _params(sem):
    return pltpu.CompilerParams(dimension_semantics=sem, vmem_limit_bytes=_VMEM_LIMIT)


def _memkv_kernel(mem_ref, g_ref, wk_ref, wv_ref, kg_ref, mk_ref, mv_ref, mkb_ref, mvb_ref):
    h = _rms(mem_ref[...], g_ref[...]).astype(BF16)
    k = _dot(h, wk_ref[...])
    v = _dot(h, wv_ref[...])
    kg = kg_ref[...]
    k = jnp.concatenate([_rms(k[:, i * X_HD:(i + 1) * X_HD], kg) for i in range(X_HEADS)], axis=1)
    mk_ref[...] = k.reshape(mk_ref.shape)
    mv_ref[...] = v.reshape(mv_ref.shape)
    mkb_ref[...] = k.astype(BF16)
    mvb_ref[...] = v.astype(BF16)


def _memory_kv(mem, p, l):
    b, n, _ = mem.shape
    ws = [p["mem_norm_g"], p["w_xk"], p["w_xv"], p["x_knorm_g"]]
    blk = pl.BlockSpec((None, n, X_W), lambda i: (i, 0, 0))
    hblk = pl.BlockSpec((None, n, X_HEADS, X_HD), lambda i: (i, 0, 0, 0))
    return pl.pallas_call(
        _memkv_kernel,
        grid=(b,),
        in_specs=[pl.BlockSpec((None, n, D_MODEL), lambda i: (i, 0, 0))] + [_layer(w, l) for w in ws],
        out_specs=[hblk, hblk, blk, blk],
        out_shape=[jax.ShapeDtypeStruct((b, n, X_HEADS, X_HD), F32),
                   jax.ShapeDtypeStruct((b, n, X_HEADS, X_HD), F32),
                   jax.ShapeDtypeStruct((b, n, X_W), BF16), jax.ShapeDtypeStruct((b, n, X_W), BF16)],
        compiler_params=_params(("arbitrary",)),
        name="memory_kv",
    )(mem, *ws)


def _qkv_from_h(h, w, qg, kg):
    z = _dot(h, w)
    gm = _group_mean_matrix(_MXU_TILE, A_HD)
    q = _group_rms(z[:, :A_W], qg, gm) * (A_HD ** -0.5)
    k = _group_rms(z[:, A_W:2 * A_W], kg, gm)
    v = z[:, 2 * A_W:]
    return q, k, v


def _qkv_kernel(x_ref, g_ref, w_ref, qg_ref, kg_ref, *rest):
    k_ref, v_ref, qb_ref, kb_ref, vb_ref = rest[-5:]
    h = _rms(x_ref[...], g_ref[...]).astype(BF16)
    q, k, v = _qkv_from_h(h, w_ref[...], qg_ref[...], kg_ref[...])
    k_ref[...] = k.reshape(k_ref.shape)
    v_ref[...] = v.reshape(v_ref.shape)
    qb_ref[...] = (q * LOG2E).astype(BF16)
    kb_ref[...] = k.astype(BF16)
    vb_ref[...] = v.astype(BF16)


def _qkv_proj(x, p, layer, depth, stacks):
    b, t, _ = x.shape
    blk = pl.BlockSpec((None, _TM, A_W), lambda i, j: (i, j, 0))
    hblk = pl.BlockSpec((None, None, _TM, A_HEADS, A_VD), lambda i, j: (layer, i, j, 0, 0))
    f32 = jax.ShapeDtypeStruct((depth, b, t, A_HEADS, A_VD), F32)
    b16 = jax.ShapeDtypeStruct((b, t, A_W), BF16)
    ins = [x, p["norm_mix_g"], p["w_in"], p["a_qnorm_g"], p["a_knorm_g"]]
    in_specs = [pl.BlockSpec((None, _TM, D_MODEL), lambda i, j: (i, j, 0)), _layer(ins[1], layer),
                _layer(ins[2], layer, cols=_QKV_COLS), _layer(ins[3], layer), _layer(ins[4], layer)]
    aliases = {}
    if stacks is not None:
        aliases = {len(ins): 0, len(ins) + 1: 1}
        ins += list(stacks)
        in_specs += [pl.BlockSpec(memory_space=pl.ANY)] * 2
    return pl.pallas_call(
        _qkv_kernel,
        grid=(b, t // _TM),
        in_specs=in_specs,
        out_specs=[hblk, hblk, blk, blk, blk],
        out_shape=[f32, f32, b16, b16, b16],
        input_output_aliases=aliases,
        compiler_params=_params(("arbitrary", "arbitrary")),
        name="qkv_proj",
    )(*ins)


def _diff_attn_kernel(slope_ref, q_ref, k_ref, v_ref, lam_ref, sg_ref, o_ref,
                      qs_scr, m_scr, l_scr, acc_scr, *, lam_init):
    hg = pl.program_id(1)
    q0 = pl.program_id(2) * _TQ
    rows = 2 * _TQ

    for hh in range(_ATT_HEADS):
        q = q_ref[:, hh * A_VD:(hh + 1) * A_VD]
        lane = lax.broadcasted_iota(jnp.int32, q.shape, 1)
        zero = jnp.zeros_like(q)
        qs_scr[hh, 0:_TQ, :] = jnp.where(lane < A_HD, q, zero)
        qs_scr[hh, _TQ:rows, :] = jnp.where(lane >= A_HD, q, zero)

    def chunk(j, hh, masked, first=False):
        k0 = pl.multiple_of(j * _TK, _TK)
        head = slice(hh * A_VD, (hh + 1) * A_VD)
        s = _dot_nt(qs_scr[hh], k_ref[pl.ds(k0, _TK), head])
        kpos = k0 + lax.broadcasted_iota(jnp.int32, (1, _TK), 1)
        s = s + (slope_ref[hg * _ATT_HEADS + hh] * LOG2E) * kpos.astype(F32)
        if masked:
            row = lax.broadcasted_iota(jnp.int32, (rows, _TK), 0)
            qpos = q0 + jnp.where(row >= _TQ, row - _TQ, row)
            col = k0 + lax.broadcasted_iota(jnp.int32, (rows, _TK), 1)
            s = jnp.where(qpos >= col, s, NEG_INF)
        m_cur = jnp.max(s, axis=-1, keepdims=True)
        if first:
            m_new = jnp.broadcast_to(m_cur, (rows, A_VD))
        else:
            m_old = m_scr[hh]
            m_new = jnp.maximum(m_old, m_cur)
            alpha = jnp.exp2(m_old - m_new)
        p = jnp.exp2(s - jnp.concatenate([m_new] * (_TK // A_VD), axis=1))
        l_cur = jnp.sum(p, axis=-1, keepdims=True)
        pv = _dot(p.astype(BF16), v_ref[pl.ds(k0, _TK), head])
        if first:
            l_scr[hh] = jnp.broadcast_to(l_cur, (rows, A_VD))
            acc_scr[hh] = pv
        else:
            l_scr[hh] = alpha * l_scr[hh] + l_cur
            acc_scr[hh] = alpha * acc_scr[hh] + pv
        m_scr[hh] = m_new

    n_full = q0 // _TK

    def body(j, carry):
        for hh in range(_ATT_HEADS):
            chunk(j, hh, False)
        return carry

    @pl.when(n_full == 0)
    def _():
        for d in range(_TQ // _TK):
            for hh in range(_ATT_HEADS):
                chunk(d, hh, True, first=d == 0)

    @pl.when(n_full > 0)
    def _():
        for hh in range(_ATT_HEADS):
            chunk(0, hh, False, first=True)
        lax.fori_loop(1, n_full, body, 0)
        for d in range(_TQ // _TK):
            for hh in range(_ATT_HEADS):
                chunk(n_full + d, hh, True)

    lam = _lam(lam_ref, lam_init)
    for hh in range(_ATT_HEADS):
        o = acc_scr[hh] / l_scr[hh]
        o = o[0:_TQ] - lam * o[_TQ:rows]
        o_ref[:, hh * A_VD:(hh + 1) * A_VD] = (_rms(o, sg_ref[...]) * (1.0 - lam_init)).astype(BF16)


def _diff_attention(slopes, qb, kb, vb, p, l, lam_init):
    b, t, _ = qb.shape
    assert _TQ % _TK == 0 and t % _TQ == 0 and A_HEADS % _ATT_HEADS == 0
    width = _ATT_HEADS * A_VD
    qblk = pl.BlockSpec((None, _TQ, width), lambda bi, h, qi: (bi, qi, h))
    kvblk = pl.BlockSpec((None, t, width), lambda bi, h, qi: (bi, 0, h))
    stat = pltpu.VMEM((_ATT_HEADS, 2 * _TQ, A_VD), F32)
    return pl.pallas_call(
        functools.partial(_diff_attn_kernel, lam_init=lam_init),
        grid=(b, A_HEADS // _ATT_HEADS, t // _TQ),
        in_specs=[pl.BlockSpec(memory_space=pltpu.SMEM), qblk, kvblk, kvblk,
                  _layer(p["a_lam"], l), _layer(p["a_subln_g"], l)],
        out_specs=qblk,
        out_shape=jax.ShapeDtypeStruct((b, t, A_W), BF16),
        scratch_shapes=[pltpu.VMEM((_ATT_HEADS, 2 * _TQ, A_VD), BF16), stat, stat, stat],
        compiler_params=_params(("arbitrary",) * 3),
        name="diff_attention",
    )(slopes, qb, kb, vb, p["a_lam"], p["a_subln_g"])


def _lane_group_mask(width, group, g):
    lane = lax.broadcasted_iota(jnp.int32, (1, width), 1)
    return (_idiv(lane, group) == g).astype(F32)


def _mixer_kernel(x_ref, o_ref, g_ref, wr_ref, wa_ref, wb_ref, wc_ref, wd_ref, wo_ref,
                  bcw_ref, bcb_ref, ccw_ref, ccb_ref, clg_ref, clb_ref, dlg_ref, dlb_ref,
                  dws_ref, dbst_ref, x1_ref, cbp_ref, ccp_ref, ubuf, cbuf, ycin, ydin, shwin, gates):
    tm = _TM

    @pl.when(pl.program_id(1) == 0)
    def _():
        ubuf[0:_B_HALO, :] = jnp.zeros((_B_HALO, B_W), F32)
        cbuf[0:_C_HALO, :] = jnp.zeros((_C_HALO, C_W), F32)

    x = x_ref[...]
    h = _rms(x, g_ref[...]).astype(BF16)

    zc = _dot(h, wr_ref[:, _R_CA:_R_DU])
    cbuf[_C_HALO:_C_HALO + tm, :] = zc[:, 0:256] * _sigmoid(zc[:, 256:512])
    base = _C_HALO - (C_K - 1)
    gcols = N_BRANCH * D_MODEL // (tm // _CONV_ROWS)
    bias = ccb_ref[...]
    h_head, h_rest = h[:, 0:_MXU_TILE], h[:, _MXU_TILE:]
    for it, r in enumerate(range(0, tm, _CONV_ROWS)):
        c0 = _R_GZ + it * gcols
        t = jnp.tanh(_dot(h_head, wr_ref[0:_MXU_TILE, c0:c0 + gcols])
                     + _dot(h_rest, wr_ref[_MXU_TILE:, c0:c0 + gcols]))
        gates[:, it * gcols:(it + 1) * gcols] = t
        acc = jnp.broadcast_to(bias, (_CONV_ROWS, C_W))
        bias = ccb_ref[...] + 0.0 * t[0:1, 0:C_W]
        for ph in range(_SUBLANES):
            n_taps = (C_K - 1 - ph) // _SUBLANES + 1
            rows = _CONV_ROWS + _SUBLANES * (n_taps - 1)
            shwin[ph, 0:rows, :] = cbuf[base + r + ph:base + r + ph + rows, :]
            for i in range(n_taps):
                k = ph + _SUBLANES * i
                acc = acc + ccw_ref[k:k + 1, :] * shwin[ph, _SUBLANES * i:_SUBLANES * i + _CONV_ROWS, :]
        y = _layer_norm(acc, clg_ref[...], clb_ref[...])
        y = (y * _sigmoid(y)).astype(BF16)
        ycin[r:r + _CONV_ROWS, :] = y
        h_head = h[:, 0:_MXU_TILE] + (0.0 * y[0:1, :]).astype(BF16)
    ccp_ref[...] = cbuf[_C_HALO + tm - (C_K - 1):_C_HALO + tm, :]
    cbuf[0:_C_HALO, :] = cbuf[tm:tm + _C_HALO, :]

    zd = _dot(h, wr_ref[:, _R_DU:_R_GZ])
    yc = _dot(ycin[...], wc_ref[...])
    ya = _dot(o_ref[...], wa_ref[...])
    zb = _dot(h, wr_ref[:, _R_BX:_R_CA])

    zd = jax.nn.gelu(zd)
    du = zd[:, 0:256]
    dvn = _layer_norm(zd[:, 256:512], dlg_ref[...], dlb_ref[...])
    row = lax.broadcasted_iota(jnp.int32, (CHUNK, CHUNK), 0)
    col = lax.broadcasted_iota(jnp.int32, (CHUNK, CHUNK), 1)
    tril = (row >= col).astype(F32)
    wcat = jnp.concatenate([dws_ref[g] * tril for g in range(D_GROUPS)], axis=1).astype(BF16)
    masks = [_lane_group_mask(D_W, D_W // D_GROUPS, g) for g in range(D_GROUPS)]
    bsmat = sum(dbst_ref[:, g:g + 1] * masks[g] for g in range(D_GROUPS))
    for c in range(0, tm, CHUNK):
        vch = dvn[c:c + CHUNK]
        rhs = jnp.concatenate([(vch * masks[g]).astype(BF16) for g in range(D_GROUPS)], axis=0)
        s = _dot(wcat, rhs) + bsmat
        ydin[c:c + CHUNK, :] = (du[c:c + CHUNK] * s).astype(BF16)

    def gate(i):
        return gates[:, i * D_MODEL:(i + 1) * D_MODEL]

    merged = _gated(gate(2), yc) + _gated(gate(0), ya)

    u = zb[:, 512:768] * zb[:, 0:256]
    ubuf[_B_HALO:_B_HALO + tm, :] = u
    conv_b = (bcw_ref[0:1, :] * ubuf[_B_HALO - 2:_B_HALO - 2 + tm, :]
              + bcw_ref[1:2, :] * ubuf[_B_HALO - 1:_B_HALO - 1 + tm, :]
              + bcw_ref[2:3, :] * u + bcb_ref[...])
    yb = _dot((zb[:, 256:512] * conv_b).astype(BF16), wb_ref[...])
    merged = merged + _gated(gate(1), yb)
    cbp_ref[...] = ubuf[_B_HALO + tm - (B_K - 1):_B_HALO + tm, :]
    ubuf[0:_B_HALO, :] = ubuf[tm:tm + _B_HALO, :]

    merged = merged + _gated(gate(3), _dot(ydin[...], wd_ref[...]))

    x1_ref[...] = x + _dot(merged.astype(BF16), wo_ref[...])


def _mixer(x, o, p, l):
    b, t, _ = x.shape
    ws = [p[k] for k in ("norm_mix_g", "w_in", "w_a_out", "w_b_out", "w_c_out", "w_d_out", "w_o", "b_conv_w",
                         "b_conv_b", "c_conv_w", "c_conv_b", "c_ln_g", "c_ln_b", "d_ln_g", "d_ln_b", "d_ws",
                         "d_bs_t")]
    row = lambda w: pl.BlockSpec((None, _TM, w), lambda i, j: (i, j, 0))
    return pl.pallas_call(
        _mixer_kernel,
        grid=(b, t // _TM),
        in_specs=[row(D_MODEL), row(A_W)] + [_layer(w, l) for w in ws],
        out_specs=[row(D_MODEL),
                   pl.BlockSpec((None, B_K - 1, B_W), lambda i, j: (i, 0, 0)),
                   pl.BlockSpec((None, C_K - 1, C_W), lambda i, j: (i, 0, 0))],
        out_shape=[jax.ShapeDtypeStruct((b, t, D_MODEL), F32),
                   jax.ShapeDtypeStruct((b, B_K - 1, B_W), F32),
                   jax.ShapeDtypeStruct((b, C_K - 1, C_W), F32)],
        scratch_shapes=[pltpu.VMEM((_B_HALO + _TM, B_W), F32), pltpu.VMEM((_C_HALO + _TM, C_W), F32),
                        pltpu.VMEM((_TM, C_W), BF16), pltpu.VMEM((_TM, D_W), BF16),
                        pltpu.VMEM((_SUBLANES, _CONV_ROWS + _C_HALO, C_W), F32),
                        pltpu.VMEM((_TM, N_BRANCH * D_MODEL), F32)],
        compiler_params=_params(("arbitrary", "arbitrary")),
        name="mixer",
    )(x, o, *ws)


def _ffn(x2, gf, wup_ref, wdn_ref):
    h3 = _rms(x2, gf).astype(BF16)
    acc = jnp.zeros(x2.shape, F32)
    for c in range(0, D_FF, _FF_CHUNK):
        a = jnp.maximum(_dot(h3, wup_ref[:, c:c + _FF_CHUNK]), 0.0)
        acc = acc + _dot((a * a).astype(BF16), wdn_ref[c:c + _FF_CHUNK, :])
    return x2 + acc


def _cross_attention(x, mk_ref, mv_ref, gx, wxq_ref, qg, wxo_ref):
    h2 = _rms(x, gx).astype(BF16)
    q = _dot(h2, wxq_ref[...])
    oms = []
    for i in range(X_HEADS):
        sl = slice(i * X_HD, (i + 1) * X_HD)
        qh = (_rms(q[:, sl], qg) * (X_HD ** -0.5)).astype(BF16)
        s = _dot_nt(qh, mk_ref[:, sl])
        p = jnp.exp(s - jnp.max(s, axis=-1, keepdims=True))
        oms.append(_dot(p.astype(BF16), mv_ref[:, sl]) * (1.0 / jnp.sum(p, axis=-1, keepdims=True)))
    om = jnp.concatenate(oms, axis=1).astype(BF16)
    return x + _dot(om, wxo_ref[...])


def _xffn_kernel(x_ref, mk_ref, mv_ref, gx_ref, wxq_ref, qg_ref, wxo_ref, gf_ref, wup_ref, wdn_ref,
                 out_ref):
    x2 = _cross_attention(x_ref[...], mk_ref, mv_ref, gx_ref[...], wxq_ref, qg_ref[...], wxo_ref)
    out_ref[...] = _ffn(x2, gf_ref[...], wup_ref, wdn_ref)


def _xattn_ffn(x, mkb, mvb, p, l):
    b, t, _ = x.shape
    ws = [p[k] for k in ("norm_x_g", "w_xq", "x_qnorm_g", "w_xo", "norm_ffn_g", "w_up", "w_down")]
    n = mkb.shape[1]
    row = pl.BlockSpec((None, _TM, D_MODEL), lambda i, j: (i, j, 0))
    mem = pl.BlockSpec((None, n, X_W), lambda i, j: (i, 0, 0))
    return pl.pallas_call(
        _xffn_kernel,
        grid=(b, t // _TM),
        in_specs=[row, mem, mem] + [_layer(w, l) for w in ws],
        out_specs=row,
        out_shape=jax.ShapeDtypeStruct((b, t, D_MODEL), F32),
        compiler_params=_params(("arbitrary", "arbitrary")),
        name="xattn_ffn",
    )(x, mkb, mvb, *ws)


def _sample_mixer_kernel(x_ref, sb_ref, sc_ref, g_ref, wr_ref, qg_ref, kg_ref,
                         wb_ref, wc_ref, wd_ref, bcw_ref, bcb_ref, ccw_ref, ccb_ref, clg_ref, clb_ref,
                         dlg_ref, dlb_ref, dws_ref, dbs_ref,
                         q_ref, k_ref, v_ref, cb_ref, cc_ref, dvn_ref, part_ref, g0_ref):
    x = x_ref[...]
    h = _rms(x, g_ref[...]).astype(BF16)
    q, k, v = _qkv_from_h(h, wr_ref[:, 0:_QKV_COLS], qg_ref[...], kg_ref[...])
    q_ref[...] = q
    k_ref[...] = k
    v_ref[...] = v

    def gate(i):
        return jnp.tanh(_dot(h, wr_ref[:, _R_GZ + i * D_MODEL:_R_GZ + (i + 1) * D_MODEL]))

    g0_ref[...] = gate(0)

    zb = _dot(h, wr_ref[:, _R_BX:_R_CA])
    u = zb[:, 512:768] * zb[:, 0:256]
    conv_b = (bcw_ref[0:1, :] * sb_ref[:, 0:B_W] + bcw_ref[1:2, :] * sb_ref[:, B_W:2 * B_W]
              + bcw_ref[2:3, :] * u + bcb_ref[...])
    part = _gated(gate(1), _dot((zb[:, 256:512] * conv_b).astype(BF16), wb_ref[...]))
    cb_ref[:, 0:B_W] = sb_ref[:, B_W:2 * B_W]
    cb_ref[:, B_W:2 * B_W] = u

    zc = _dot(h, wr_ref[:, _R_CA:_R_DU])
    uc = zc[:, 0:256] * _sigmoid(zc[:, 256:512])
    acc = ccw_ref[C_K - 1:C_K, :] * uc + ccb_ref[...]
    for kk in range(C_K - 1):
        acc = acc + ccw_ref[kk:kk + 1, :] * sc_ref[:, kk * C_W:(kk + 1) * C_W]
    y = _layer_norm(acc, clg_ref[...], clb_ref[...])
    yc = _dot((y * _sigmoid(y)).astype(BF16), wc_ref[...])
    part = part + _gated(gate(2), yc)
    cc_ref[:, 0:(C_K - 2) * C_W] = sc_ref[:, C_W:(C_K - 1) * C_W]
    cc_ref[:, (C_K - 2) * C_W:(C_K - 1) * C_W] = uc

    zd = jax.nn.gelu(_dot(h, wr_ref[:, _R_DU:_R_GZ]))
    dvn = _layer_norm(zd[:, 256:512], dlg_ref[...], dlb_ref[...])
    dvn_ref[...] = dvn
    w00 = sum(dws_ref[g, 0:1, 0:1] * _lane_group_mask(D_W, D_W // D_GROUPS, g) for g in range(D_GROUPS))
    b0 = sum(dbs_ref[g:g + 1, 0:1] * _lane_group_mask(D_W, D_W // D_GROUPS, g) for g in range(D_GROUPS))
    yd = _dot((zd[:, 0:256] * (w00 * dvn + b0)).astype(BF16), wd_ref[...])
    part_ref[...] = part + _gated(gate(3), yd)


def _sample_mixer(x, sb, sc, p, l):
    n = x.shape[0]
    shapes = [(n, A_W), (n, A_W), (n, A_W), (n, (B_K - 1) * B_W), (n, (C_K - 1) * C_W), (n, D_W),
              (n, D_MODEL), (n, D_MODEL)]
    ins = [x, sb, sc]
    ws = [p[k] for k in ("norm_mix_g", "w_in", "a_qnorm_g", "a_knorm_g", "w_b_out", "w_c_out", "w_d_out",
                         "b_conv_w", "b_conv_b", "c_conv_w", "c_conv_b", "c_ln_g", "c_ln_b", "d_ln_g", "d_ln_b",
                         "d_ws", "d_bs")]
    return pl.pallas_call(
        _sample_mixer_kernel,
        grid=(1,),
        in_specs=[_full(a.shape) for a in ins] + [_layer(w, l) for w in ws],
        out_specs=[_full(s) for s in shapes],
        out_shape=[jax.ShapeDtypeStruct(s, F32) for s in shapes],
        compiler_params=_params(("arbitrary",)),
        name="sample_mixer",
    )(*ins, *ws)


_DEC_ROWS = 2 * A_HEADS
_PAGE_ROWS = PAGE_SIZE * A_HEADS


def _head_rows(ref, row, per_head):
    return sum(jnp.where(_idiv(row, per_head) == h, jnp.broadcast_to(ref[h:h + 1, :], row.shape), 0.0)
               for h in range(ref.shape[0]))


def _decode_init(q_ref, qz_scr, m_scr, l_scr, acc_scr):
    row = lax.broadcasted_iota(jnp.int32, (_DEC_ROWS, A_VD), 0)
    lane = lax.broadcasted_iota(jnp.int32, (_DEC_ROWS, A_VD), 1)
    qz_scr[...] = jnp.where(_idiv(lane, A_HD) == (row & 1), _head_rows(q_ref, row, 2), 0.0)
    m_scr[...] = jnp.full(m_scr.shape, NEG_INF, F32)
    l_scr[...] = jnp.zeros(l_scr.shape, F32)
    acc_scr[...] = jnp.zeros(acc_scr.shape, F32)


def _decode_pages(j, k_refs, v_refs, qz_scr, m_scr, l_scr, acc_scr, past_len):
    n_pages = len(k_refs)
    width = n_pages * _PAGE_ROWS
    rowv = lax.broadcasted_iota(jnp.int32, (_DEC_ROWS, 1), 0)
    slope = sum(jnp.where(_idiv(rowv, 2) == i, 2.0 ** (-8.0 * (i + 1) / A_HEADS), 0.0)
                for i in range(A_HEADS))
    qz = qz_scr[...].astype(BF16)
    s = jnp.concatenate([_dot_nt(qz, k_refs[i][...].astype(BF16)) for i in range(n_pages)], axis=1)
    col = lax.broadcasted_iota(jnp.int32, (_DEC_ROWS, width), 1)
    kpos = j * (n_pages * PAGE_SIZE) + _idiv(lax.broadcasted_iota(jnp.int32, (1, width), 1), A_HEADS)
    own_head = (col & (A_HEADS - 1)) == _idiv(lax.broadcasted_iota(jnp.int32, (_DEC_ROWS, width), 0), 2)
    s = jnp.where(own_head, s - slope * (past_len - kpos).astype(F32), NEG_INF)
    m_old = m_scr[...]
    m_new = jnp.maximum(m_old, jnp.max(s, axis=-1, keepdims=True))
    alpha = jnp.exp(m_old - m_new)
    p = jnp.exp(s - m_new)
    l_scr[...] = alpha * l_scr[...] + jnp.sum(p, axis=-1, keepdims=True)
    p = p.astype(BF16)
    pv = sum(_dot(p[:, i * _PAGE_ROWS:(i + 1) * _PAGE_ROWS], v_refs[i][...].astype(BF16))
             for i in range(n_pages))
    acc_scr[...] = alpha * acc_scr[...] + pv
    m_scr[...] = m_new


def _decode_finish(kn_ref, vn_ref, lam_ref, sg_ref, o_ref, qz_scr, m_scr, l_scr, acc_scr, lam_init):
    row = lax.broadcasted_iota(jnp.int32, (_DEC_ROWS, A_VD), 0)
    s_new = jnp.sum(qz_scr[...] * _head_rows(kn_ref, row, 2), axis=-1, keepdims=True)
    m_old = m_scr[...]
    m_fin = jnp.maximum(m_old, s_new)
    alpha = jnp.exp(m_old - m_fin)
    p_new = jnp.exp(s_new - m_fin)
    l_fin = alpha * l_scr[...] + p_new
    o_all = (alpha * acc_scr[...] + p_new * _head_rows(vn_ref, row, 2)) / l_fin
    lam = _lam(lam_ref, lam_init)
    sg = sg_ref[...]
    for h in range(A_HEADS):
        o = o_all[2 * h:2 * h + 1] - lam * o_all[2 * h + 1:2 * h + 2]
        o_ref[h:h + 1, :] = _rms(o, sg) * (1.0 - lam_init)


def _decode_attn_kernel(pt_ref, q_ref, kn_ref, vn_ref, lam_ref, sg_ref, *rest, lam_init, past_len):
    del pt_ref
    k_refs = rest[:_DEC_PAGES]
    v_refs = rest[_DEC_PAGES:2 * _DEC_PAGES]
    o_ref = rest[2 * _DEC_PAGES]
    dec = rest[2 * _DEC_PAGES + 1:]
    j = pl.program_id(1)

    @pl.when(j == 0)
    def _():
        _decode_init(q_ref, *dec)

    _decode_pages(j, k_refs, v_refs, *dec, past_len)

    @pl.when(j == pl.num_programs(1) - 1)
    def _():
        _decode_finish(kn_ref, vn_ref, lam_ref, sg_ref, o_ref, *dec, lam_init)


def _decode_attention(page_table, q, k_new, v_new, cache_k, cache_v, p, layer, lam_init):
    n, n_pages = page_table.shape
    past_len = n_pages * PAGE_SIZE
    row = pl.BlockSpec((None, A_HEADS, A_VD), lambda b, j, pt: (b, 0, 0))

    def page(i):
        return pl.BlockSpec((None, None, _PAGE_ROWS, A_VD),
                            lambda b, j, pt: (layer, pt[b, j * _DEC_PAGES + i], 0, 0))

    grid_spec = pltpu.PrefetchScalarGridSpec(
        num_scalar_prefetch=1,
        grid=(n, n_pages // _DEC_PAGES),
        in_specs=[row, row, row, _layer(p["a_lam"], layer), _layer(p["a_subln_g"], layer)]
        + [page(i) for i in range(_DEC_PAGES)] + [page(i) for i in range(_DEC_PAGES)],
        out_specs=row,
        scratch_shapes=[pltpu.VMEM((_DEC_ROWS, A_VD), F32), pltpu.VMEM((_DEC_ROWS, 1), F32),
                        pltpu.VMEM((_DEC_ROWS, 1), F32), pltpu.VMEM((_DEC_ROWS, A_VD), F32)],
    )
    heads = lambda a: a.reshape(n, A_HEADS, A_VD)
    return pl.pallas_call(
        functools.partial(_decode_attn_kernel, lam_init=lam_init, past_len=past_len),
        grid_spec=grid_spec,
        out_shape=jax.ShapeDtypeStruct((n, A_HEADS, A_VD), F32),
        compiler_params=_params(("arbitrary", "arbitrary")),
        name="decode_attention",
    )(page_table, heads(q), heads(k_new), heads(v_new), p["a_lam"], p["a_subln_g"],
      *([cache_k] * _DEC_PAGES), *([cache_v] * _DEC_PAGES))


def _sample_tail_kernel(x_ref, o_ref, g0_ref, part_ref, mk_ref, mv_ref, wa_ref, wo_ref, gx_ref, wxq_ref,
                        qg_ref, wxo_ref, gf_ref, wup_ref, wdn_ref, out_ref, x1_scr, q_scr, om_scr):
    b = pl.program_id(0)
    nb = pl.num_programs(0)

    @pl.when(b == 0)
    def _():
        merged = _gated(g0_ref[...], _dot(o_ref[...].astype(BF16), wa_ref[...])) + part_ref[...]
        x1 = x_ref[...] + _dot(merged.astype(BF16), wo_ref[...])
        x1_scr[...] = x1
        q = _dot(_rms(x1, gx_ref[...]).astype(BF16), wxq_ref[...])
        qg = qg_ref[...]
        for h in range(X_HEADS):
            q_scr[h] = _rms(q[:, h * X_HD:(h + 1) * X_HD], qg) * (X_HD ** -0.5)

    n_rows = mk_ref.shape[1]
    row = lax.broadcasted_iota(jnp.int32, (8, X_HD), 0)
    col = lax.broadcasted_iota(jnp.int32, (8, n_rows), 1)
    own_head = (col & (X_HEADS - 1)) == lax.broadcasted_iota(jnp.int32, (8, n_rows), 0)
    for i in range(_TAIL_SEQS):
        seq = b * _TAIL_SEQS + i
        qz = sum(jnp.where(row == h, jnp.broadcast_to(q_scr[h, pl.ds(seq, 1), :], (8, X_HD)), 0.0)
                 for h in range(X_HEADS)).astype(BF16)
        s = jnp.where(own_head, _dot_nt(qz, mk_ref[i].astype(BF16)), NEG_INF)
        p = jnp.exp(s - jnp.max(s, axis=-1, keepdims=True))
        p = p / jnp.sum(p, axis=-1, keepdims=True)
        om = _dot(p.astype(BF16), mv_ref[i].astype(BF16))
        for h in range(X_HEADS):
            om_scr[h, pl.ds(seq, 1), :] = om[h:h + 1, :]

    @pl.when(b == nb - 1)
    def _():
        x2 = x1_scr[...] + sum(_dot(om_scr[h].astype(BF16), wxo_ref[h * X_HD:(h + 1) * X_HD, :])
                               for h in range(X_HEADS))
        out_ref[...] = _ffn(x2, gf_ref[...], wup_ref, wdn_ref)


def _sample_tail(x, o, g0, part, mem_k, mem_v, layer, p):
    n = x.shape[0]
    n_rows = mem_k.shape[2]
    assert n % _TAIL_SEQS == 0
    mem = pl.BlockSpec((None, _TAIL_SEQS, n_rows, X_HD), lambda b: (layer, b, 0, 0))
    ins = [x, o, g0, part]
    ws = [p[k] for k in ("w_a_out", "w_o", "norm_x_g", "w_xq", "x_qnorm_g", "w_xo", "norm_ffn_g", "w_up",
                         "w_down")]
    return pl.pallas_call(
        _sample_tail_kernel,
        grid=(n // _TAIL_SEQS,),
        in_specs=[_full(a.shape) for a in ins] + [mem, mem] + [_layer(w, layer) for w in ws],
        out_specs=_full((n, D_MODEL)),
        out_shape=jax.ShapeDtypeStruct((n, D_MODEL), F32),
        scratch_shapes=[pltpu.VMEM((n, D_MODEL), F32), pltpu.VMEM((X_HEADS, n, X_HD), F32),
                        pltpu.VMEM((X_HEADS, n, X_HD), F32)],
        compiler_params=_params(("arbitrary",)),
        name="sample_tail",
    )(*ins, mem_k, mem_v, *ws)


def _stacked_params(a):
    row = lambda v: v.reshape(v.shape[0], 1, -1)
    b16 = lambda w: w.astype(BF16)
    tiled = lambda v: row(jnp.tile(v, (1, A_W // A_HD)))
    out = {k: b16(a[k]) for k in ("w_o", "w_xq", "w_xk", "w_xv", "w_xo", "w_up", "w_down")}
    halve_gates = jnp.where(jnp.arange(IN_COLS) >= _R_GZ, 0.5, 1.0).astype(F32)
    out["w_in"] = b16(a["w_in"] * halve_gates)
    out.update({k: b16(0.5 * a[k]) for k in ("w_a_out", "w_b_out", "w_c_out", "w_d_out")})
    out.update({k: row(a[k]) for k in ("norm_mix_g", "a_subln_g", "b_conv_b", "c_conv_b", "c_ln_g", "c_ln_b",
                                       "d_ln_g", "d_ln_b", "norm_x_g", "mem_norm_g", "x_qnorm_g", "x_knorm_g",
                                       "norm_ffn_g")})
    out.update({k: a[k] for k in ("a_lam", "b_conv_w", "c_conv_w", "d_ws", "d_bs")})
    out.update(a_qnorm_g=tiled(a["a_qnorm_g"]), a_knorm_g=tiled(a["a_knorm_g"]),
               d_bs_t=jnp.swapaxes(a["d_bs"], 1, 2))
    return out


def kernel(x_prompt, x_sample, cache_k_a, cache_v_a, state_conv_b, state_conv_c, cache_mem_k, cache_mem_v,
           page_table, mem_prompt, norm_mix_g, w_in, a_qnorm_g, a_knorm_g, a_lam, a_subln_g, w_a_out,
           b_conv_w, b_conv_b, w_b_out, c_conv_w, c_conv_b, c_ln_g, c_ln_b, w_c_out, d_ln_g, d_ln_b,
           d_ws, d_bs, w_d_out, w_o, norm_x_g, mem_norm_g, w_xq, w_xk, w_xv, x_qnorm_g, x_knorm_g,
           w_xo, norm_ffn_g, w_up, w_down):
    weights = dict(norm_mix_g=norm_mix_g, w_in=w_in, a_qnorm_g=a_qnorm_g, a_knorm_g=a_knorm_g, a_lam=a_lam,
                   a_subln_g=a_subln_g, w_a_out=w_a_out, b_conv_w=b_conv_w, b_conv_b=b_conv_b,
                   w_b_out=w_b_out, c_conv_w=c_conv_w, c_conv_b=c_conv_b, c_ln_g=c_ln_g, c_ln_b=c_ln_b,
                   w_c_out=w_c_out, d_ln_g=d_ln_g, d_ln_b=d_ln_b, d_ws=d_ws, d_bs=d_bs, w_d_out=w_d_out,
                   w_o=w_o, norm_x_g=norm_x_g, mem_norm_g=mem_norm_g, w_xq=w_xq, w_xk=w_xk, w_xv=w_xv,
                   x_qnorm_g=x_qnorm_g, x_knorm_g=x_knorm_g, w_xo=w_xo, norm_ffn_g=norm_ffn_g,
                   w_up=w_up, w_down=w_down)
    depth = w_in.shape[0]
    bp, t, _ = x_prompt.shape
    ns = x_sample.shape[0]
    n_pool = cache_k_a.shape[1]
    n_mem = cache_mem_k.shape[2]
    cache_k = cache_k_a.reshape(depth, n_pool, _PAGE_ROWS, A_VD)
    cache_v = cache_v_a.reshape(depth, n_pool, _PAGE_ROWS, A_VD)
    mem_k_s = cache_mem_k.reshape(depth, ns, n_mem * X_HEADS, X_HD)
    mem_v_s = cache_mem_v.reshape(depth, ns, n_mem * X_HEADS, X_HD)
    slopes = jnp.asarray([2.0 ** (-8.0 * (i + 1) / A_HEADS) for i in range(A_HEADS)], F32)

    xp = x_prompt
    xs = x_sample.reshape(ns, D_MODEL)
    outs = [[] for _ in range(9)]
    p = _stacked_params(weights)
    for l in range(depth):
        lam_init = 0.8 - 0.6 * math.exp(-0.3 * l)

        mk, mv, mkb, mvb = _memory_kv(mem_prompt, p, l)
        k_p, v_p, qb, kb, vb = _qkv_proj(xp, p, l, depth, None if l == 0 else (k_p, v_p))
        o_p = _diff_attention(slopes, qb, kb, vb, p, l, lam_init)
        xp, cb_p, cc_p = _mixer(xp, o_p, p, l)
        xp = _xattn_ffn(xp, mkb, mvb, p, l)

        sb = state_conv_b[l].reshape(ns, (B_K - 1) * B_W)
        sc = state_conv_c[l].reshape(ns, (C_K - 1) * C_W)
        q_s, k_s, v_s, cb_s, cc_s, dvn_s, part, g0 = _sample_mixer(xs, sb, sc, p, l)
        o_s = _decode_attention(page_table, q_s, k_s, v_s, cache_k, cache_v, p, l, lam_init)
        xs = _sample_tail(xs, o_s.reshape(ns, A_W), g0, part, mem_k_s, mem_v_s, l, p)

        for lst, val in zip(outs, (
                cb_p, cc_p, mk, mv,
                k_s.reshape(ns, 1, A_HEADS, 2 * A_HD), v_s.reshape(ns, 1, A_HEADS, A_VD),
                cb_s.reshape(ns, B_K - 1, B_W), cc_s.reshape(ns, C_K - 1, C_W),
                dvn_s.reshape(ns, 1, D_W))):
            lst.append(val)
    return (xp, xs.reshape(ns, 1, D_MODEL), k_p, v_p) + tuple(jnp.stack(o) for o in outs)
```

```python
import functools
import math

import jax
import jax.numpy as jnp
from jax import lax
from jax.experimental import pallas as pl
from jax.experimental.pallas import tpu as pltpu

F32 = jnp.float32
BF16 = jnp.bfloat16

D_MODEL = 1024
A_HEADS = 4
A_HD = 64
A_VD = 2 * A_HD
A_W = A_HEADS * 2 * A_HD
B_W = 256
B_K = 3
C_W = 256
C_K = 31
D_W = 256
D_GROUPS = 4
CHUNK = 128
N_BRANCH = 4
X_HEADS = 4
X_HD = 128
X_W = X_HEADS * X_HD
D_FF = 4 * D_MODEL
PAGE_SIZE = 128
EPS = 1e-6
NEG_INF = -1e30
LOG2E = math.log2(math.e)

_QKV_COLS = 3 * A_W
_R_BX = _QKV_COLS
_R_CA = _R_BX + 3 * B_W
_R_DU = _R_CA + 2 * C_W
_R_GZ = _R_DU + 2 * D_W
IN_COLS = _R_GZ + N_BRANCH * D_MODEL

_V7X_VMEM_BYTES = 64 * 1024 * 1024
_VMEM_LIMIT = _V7X_VMEM_BYTES - 8 * 1024 * 1024

_TM = 512
_TM_QKV = 1024
_TQ = 512
_TK = 512
_ATT_HEADS = 4
_SUBLANES = 8
_MXU_TILE = 256
_CONV_ROWS = 64
_B_HALO = 8
_C_HALO = 32
_FF_CHUNK = 1024
_DEC_PAGES = 32
_TAIL_SEQS = 8


def _rms(x, g):
    return x * lax.rsqrt(jnp.mean(x * x, axis=-1, keepdims=True) + EPS) * g


def _layer_norm(x, g, b):
    xc = x - jnp.mean(x, axis=-1, keepdims=True)
    var = jnp.mean(xc * xc, axis=-1, keepdims=True)
    return xc * lax.rsqrt(var + EPS) * g + b


def _sigmoid(x):
    return 0.5 * jnp.tanh(0.5 * x) + 0.5


def _gated(t, y_half):
    return t * y_half + y_half


def _dot(a, b):
    return jnp.dot(a, b, preferred_element_type=F32)


def _dot_nt(a, b):
    return lax.dot_general(a, b, (((1,), (1,)), ((), ())), preferred_element_type=F32)


def _idiv(x, d):
    assert d & (d - 1) == 0
    return lax.shift_right_logical(x, int(math.log2(d)))


def _group_mean_matrix(width, group):
    r = _idiv(lax.broadcasted_iota(jnp.int32, (width, width), 0), group)
    c = _idiv(lax.broadcasted_iota(jnp.int32, (width, width), 1), group)
    return jnp.where(r == c, 1.0 / group, 0.0).astype(BF16)


def _group_rms(t, g, gm):
    w = gm.shape[0]
    sq = (t * t).astype(BF16)
    ms = jnp.concatenate([_dot(sq[:, c:c + w], gm) for c in range(0, t.shape[1], w)], axis=1)
    return t * lax.rsqrt(ms + EPS) * g


def _lam(lam_ref, lam_init):
    a = lam_ref[...]
    s1 = jnp.sum(a[0:1] * a[1:2], axis=-1, keepdims=True)
    s2 = jnp.sum(a[2:3] * a[3:4], axis=-1, keepdims=True)
    return jnp.exp(s1) - jnp.exp(s2) + lam_init


def _full(shape):
    return pl.BlockSpec(shape, lambda *_: (0,) * len(shape))


def _layer(arr, l, cols=None):
    shape = arr.shape[1:] if cols is None else arr.shape[1:-1] + (cols,)
    return pl.BlockSpec((None,) + shape, lambda *_: (l,) + (0,) * len(shape))


def _params(sem):
    return pltpu.CompilerParams(dimension_semantics=sem, vmem_limit_bytes=_VMEM_LIMIT)


def _memkv_kernel(mem_ref, g_ref, wk_ref, wv_ref, kg_ref, mk_ref, mv_ref, mkb_ref, mvb_ref):
    h = _rms(mem_ref[...], g_ref[...]).astype(BF16)
    k = _dot(h, wk_ref[...])
    v = _dot(h, wv_ref[...])
    kg = kg_ref[...]
    k = jnp.concatenate([_rms(k[:, i * X_HD:(i + 1) * X_HD], kg) for i in range(X_HEADS)], axis=1)
    mk_ref[...] = k.reshape(mk_ref.shape)
    mv_ref[...] = v.reshape(mv_ref.shape)
    mkb_ref[...] = k.astype(BF16)
    mvb_ref[...] = v.astype(BF16)


def _memory_kv(mem, p, l):
    b, n, _ = mem.shape
    ws = [p["mem_norm_g"], p["w_xk"], p["w_xv"], p["x_knorm_g"]]
    blk = pl.BlockSpec((None, n, X_W), lambda i: (i, 0, 0))
    hblk = pl.BlockSpec((None, n, X_HEADS, X_HD), lambda i: (i, 0, 0, 0))
    return pl.pallas_call(
        _memkv_kernel,
        grid=(b,),
        in_specs=[pl.BlockSpec((None, n, D_MODEL), lambda i: (i, 0, 0))] + [_layer(w, l) for w in ws],
        out_specs=[hblk, hblk, blk, blk],
        out_shape=[jax.ShapeDtypeStruct((b, n, X_HEADS, X_HD), F32),
                   jax.ShapeDtypeStruct((b, n, X_HEADS, X_HD), F32),
                   jax.ShapeDtypeStruct((b, n, X_W), BF16), jax.ShapeDtypeStruct((b, n, X_W), BF16)],
        compiler_params=_params(("arbitrary",)),
        name="memory_kv",
    )(mem, *ws)


def _qkv_from_h(h, w, qg, kg):
    z = _dot(h, w)
    gm = _group_mean_matrix(_MXU_TILE, A_HD)
    q = _group_rms(z[:, :A_W], qg, gm) * (A_HD ** -0.5)
    k = _group_rms(z[:, A_W:2 * A_W], kg, gm)
    v = z[:, 2 * A_W:]
    return q, k, v


def _qkv_kernel(x_ref, g_ref, w_ref, qg_ref, kg_ref, *rest):
    k_ref, v_ref, qb_ref, kb_ref, vb_ref = rest[-5:]
    h = _rms(x_ref[...], g_ref[...]).astype(BF16)
    q, k, v = _qkv_from_h(h, w_ref[...], qg_ref[...], kg_ref[...])
    k_ref[...] = k.reshape(k_ref.shape)
    v_ref[...] = v.reshape(v_ref.shape)
    qb_ref[...] = (q * LOG2E).astype(BF16)
    kb_ref[...] = k.astype(BF16)
    vb_ref[...] = v.astype(BF16)


def _qkv_proj(x, p, layer, depth, stacks):
    b, t, _ = x.shape
    blk = pl.BlockSpec((None, _TM_QKV, A_W), lambda i, j: (i, j, 0))
    hblk = pl.BlockSpec((None, None, _TM_QKV, A_HEADS, A_VD), lambda i, j: (layer, i, j, 0, 0))
    f32 = jax.ShapeDtypeStruct((depth, b, t, A_HEADS, A_VD), F32)
    b16 = jax.ShapeDtypeStruct((b, t, A_W), BF16)
    ins = [x, p["norm_mix_g"], p["w_in"], p["a_qnorm_g"], p["a_knorm_g"]]
    in_specs = [pl.BlockSpec((None, _TM_QKV, D_MODEL), lambda i, j: (i, j, 0)), _layer(ins[1], layer),
                _layer(ins[2], layer, cols=_QKV_COLS), _layer(ins[3], layer), _layer(ins[4], layer)]
    aliases = {}
    if stacks is not None:
        aliases = {len(ins): 0, len(ins) + 1: 1}
        ins += list(stacks)
        in_specs += [pl.BlockSpec(memory_space=pl.ANY)] * 2
    return pl.pallas_call(
        _qkv_kernel,
        grid=(b, t // _TM_QKV),
        in_specs=in_specs,
        out_specs=[hblk, hblk, blk, blk, blk],
        out_shape=[f32, f32, b16, b16, b16],
        input_output_aliases=aliases,
        compiler_params=_params(("arbitrary", "arbitrary")),
        name="qkv_proj",
    )(*ins)


def _diff_attn_kernel(slope_ref, q_ref, k_ref, v_ref, lam_ref, sg_ref, o_ref,
                      qs_scr, m_scr, l_scr, acc_scr, *, lam_init):
    hg = pl.program_id(1)
    q0 = pl.program_id(2) * _TQ
    rows = 2 * _TQ

    for hh in range(_ATT_HEADS):
        q = q_ref[:, hh * A_VD:(hh + 1) * A_VD]
        lane = lax.broadcasted_iota(jnp.int32, q.shape, 1)
        zero = jnp.zeros_like(q)
        qs_scr[hh, 0:_TQ, :] = jnp.where(lane < A_HD, q, zero)
        qs_scr[hh, _TQ:rows, :] = jnp.where(lane >= A_HD, q, zero)

    def chunk(j, hh, masked, first=False):
        k0 = pl.multiple_of(j * _TK, _TK)
        head = slice(hh * A_VD, (hh + 1) * A_VD)
        s = _dot_nt(qs_scr[hh], k_ref[pl.ds(k0, _TK), head])
        kpos = k0 + lax.broadcasted_iota(jnp.int32, (1, _TK), 1)
        s = s + (slope_ref[hg * _ATT_HEADS + hh] * LOG2E) * kpos.astype(F32)
        if masked:
            row = lax.broadcasted_iota(jnp.int32, (rows, _TK), 0)
            qpos = q0 + jnp.where(row >= _TQ, row - _TQ, row)
            col = k0 + lax.broadcasted_iota(jnp.int32, (rows, _TK), 1)
            s = jnp.where(qpos >= col, s, NEG_INF)
        m_cur = jnp.max(s, axis=-1, keepdims=True)
        if first:
            m_new = jnp.broadcast_to(m_cur, (rows, A_VD))
        else:
            m_old = m_scr[hh]
            m_new = jnp.maximum(m_old, m_cur)
            alpha = jnp.exp2(m_old - m_new)
        p = jnp.exp2(s - jnp.concatenate([m_new] * (_TK // A_VD), axis=1))
        l_cur = jnp.sum(p, axis=-1, keepdims=True)
        pv = _dot(p.astype(BF16), v_ref[pl.ds(k0, _TK), head])
        if first:
            l_scr[hh] = jnp.broadcast_to(l_cur, (rows, A_VD))
            acc_scr[hh] = pv
        else:
            l_scr[hh] = alpha * l_scr[hh] + l_cur
            acc_scr[hh] = alpha * acc_scr[hh] + pv
        m_scr[hh] = m_new

    n_full = q0 // _TK

    def body(j, carry):
        for hh in range(_ATT_HEADS):
            chunk(j, hh, False)
        return carry

    @pl.when(n_full == 0)
    def _():
        for d in range(_TQ // _TK):
            for hh in range(_ATT_HEADS):
                chunk(d, hh, True, first=d == 0)

    @pl.when(n_full > 0)
    def _():
        for hh in range(_ATT_HEADS):
            chunk(0, hh, False, first=True)
        lax.fori_loop(1, n_full, body, 0)
        for d in range(_TQ // _TK):
            for hh in range(_ATT_HEADS):
                chunk(n_full + d, hh, True)

    lam = _lam(lam_ref, lam_init)
    for hh in range(_ATT_HEADS):
        o = acc_scr[hh] / l_scr[hh]
        o = o[0:_TQ] - lam * o[_TQ:rows]
        o_ref[:, hh * A_VD:(hh + 1) * A_VD] = (_rms(o, sg_ref[...]) * (1.0 - lam_init)).astype(BF16)


def _diff_attention(slopes, qb, kb, vb, p, l, lam_init):
    b, t, _ = qb.shape
    assert _TQ % _TK == 0 and t % _TQ == 0 and A_HEADS % _ATT_HEADS == 0
    width = _ATT_HEADS * A_VD
    qblk = pl.BlockSpec((None, _TQ, width), lambda bi, h, qi: (bi, qi, h))
    kvblk = pl.BlockSpec((None, t, width), lambda bi, h, qi: (bi, 0, h))
    stat = pltpu.VMEM((_ATT_HEADS, 2 * _TQ, A_VD), F32)
    return pl.pallas_call(
        functools.partial(_diff_attn_kernel, lam_init=lam_init),
        grid=(b, A_HEADS // _ATT_HEADS, t // _TQ),
        in_specs=[pl.BlockSpec(memory_space=pltpu.SMEM), qblk, kvblk, kvblk,
                  _layer(p["a_lam"], l), _layer(p["a_subln_g"], l)],
        out_specs=qblk,
        out_shape=jax.ShapeDtypeStruct((b, t, A_W), BF16),
        scratch_shapes=[pltpu.VMEM((_ATT_HEADS, 2 * _TQ, A_VD), BF16), stat, stat, stat],
        compiler_params=_params(("arbitrary",) * 3),
        name="diff_attention",
    )(slopes, qb, kb, vb, p["a_lam"], p["a_subln_g"])


def _lane_group_mask(width, group, g):
    lane = lax.broadcasted_iota(jnp.int32, (1, width), 1)
    return (_idiv(lane, group) == g).astype(F32)


def _mixer_kernel(x_ref, o_ref, g_ref, wr_ref, wa_ref, wb_ref, wc_ref, wd_ref, wo_ref,
                  bcw_ref, bcb_ref, ccw_ref, ccb_ref, clg_ref, clb_ref, dlg_ref, dlb_ref,
                  dws_ref, dbst_ref, x1_ref, cbp_ref, ccp_ref, ubuf, cbuf, ycin, ydin, shwin, gates):
    tm = _TM

    @pl.when(pl.program_id(1) == 0)
    def _():
        ubuf[0:_B_HALO, :] = jnp.zeros((_B_HALO, B_W), F32)
        cbuf[0:_C_HALO, :] = jnp.zeros((_C_HALO, C_W), F32)

    x = x_ref[...]
    h = _rms(x, g_ref[...]).astype(BF16)

    zc = _dot(h, wr_ref[:, _R_CA:_R_DU])
    cbuf[_C_HALO:_C_HALO + tm, :] = zc[:, 0:C_W] * _sigmoid(zc[:, C_W:2 * C_W])
    base = _C_HALO - (C_K - 1)
    gcols = N_BRANCH * D_MODEL // (tm // _CONV_ROWS)
    bias = ccb_ref[...]
    h_head, h_rest = h[:, 0:_MXU_TILE], h[:, _MXU_TILE:]
    for it, r in enumerate(range(0, tm, _CONV_ROWS)):
        c0 = _R_GZ + it * gcols
        t = jnp.tanh(_dot(h_head, wr_ref[0:_MXU_TILE, c0:c0 + gcols])
                     + _dot(h_rest, wr_ref[_MXU_TILE:, c0:c0 + gcols]))
        gates[:, it * gcols:(it + 1) * gcols] = t
        acc = jnp.broadcast_to(bias, (_CONV_ROWS, C_W))
        bias = ccb_ref[...] + 0.0 * t[0:1, 0:C_W]
        for ph in range(_SUBLANES):
            n_taps = (C_K - 1 - ph) // _SUBLANES + 1
            rows = _CONV_ROWS + _SUBLANES * (n_taps - 1)
            shwin[ph, 0:rows, :] = cbuf[base + r + ph:base + r + ph + rows, :]
            for i in range(n_taps):
                k = ph + _SUBLANES * i
                acc = acc + ccw_ref[k:k + 1, :] * shwin[ph, _SUBLANES * i:_SUBLANES * i + _CONV_ROWS, :]
        y = _layer_norm(acc, clg_ref[...], clb_ref[...])
        y = (y * _sigmoid(y)).astype(BF16)
        ycin[r:r + _CONV_ROWS, :] = y
        h_head = h[:, 0:_MXU_TILE] + (0.0 * y[0:1, :]).astype(BF16)
    ccp_ref[...] = cbuf[_C_HALO + tm - (C_K - 1):_C_HALO + tm, :]
    cbuf[0:_C_HALO, :] = cbuf[tm:tm + _C_HALO, :]

    zd = _dot(h, wr_ref[:, _R_DU:_R_GZ])
    yc = _dot(ycin[...], wc_ref[...])
    ya = _dot(o_ref[...], wa_ref[...])
    zb = _dot(h, wr_ref[:, _R_BX:_R_CA])

    zd = jax.nn.gelu(zd)
    du = zd[:, 0:D_W]
    dvn = _layer_norm(zd[:, D_W:2 * D_W], dlg_ref[...], dlb_ref[...])
    row = lax.broadcasted_iota(jnp.int32, (CHUNK, CHUNK), 0)
    col = lax.broadcasted_iota(jnp.int32, (CHUNK, CHUNK), 1)
    tril = (row >= col).astype(F32)
    wcat = jnp.concatenate([dws_ref[g] * tril for g in range(D_GROUPS)], axis=1).astype(BF16)
    masks = [_lane_group_mask(D_W, D_W // D_GROUPS, g) for g in range(D_GROUPS)]
    bsmat = sum(dbst_ref[:, g:g + 1] * masks[g] for g in range(D_GROUPS))
    for c in range(0, tm, CHUNK):
        vch = dvn[c:c + CHUNK]
        rhs = jnp.concatenate([(vch * masks[g]).astype(BF16) for g in range(D_GROUPS)], axis=0)
        s = _dot(wcat, rhs) + bsmat
        ydin[c:c + CHUNK, :] = (du[c:c + CHUNK] * s).astype(BF16)

    def gate(i):
        return gates[:, i * D_MODEL:(i + 1) * D_MODEL]

    merged = _gated(gate(2), yc) + _gated(gate(0), ya)

    u = zb[:, 2 * B_W:3 * B_W] * zb[:, 0:B_W]
    ubuf[_B_HALO:_B_HALO + tm, :] = u
    conv_b = (bcw_ref[0:1, :] * ubuf[_B_HALO - 2:_B_HALO - 2 + tm, :]
              + bcw_ref[1:2, :] * ubuf[_B_HALO - 1:_B_HALO - 1 + tm, :]
              + bcw_ref[2:3, :] * u + bcb_ref[...])
    yb = _dot((zb[:, B_W:2 * B_W] * conv_b).astype(BF16), wb_ref[...])
    merged = merged + _gated(gate(1), yb)
    cbp_ref[...] = ubuf[_B_HALO + tm - (B_K - 1):_B_HALO + tm, :]
    ubuf[0:_B_HALO, :] = ubuf[tm:tm + _B_HALO, :]

    merged = merged + _gated(gate(3), _dot(ydin[...], wd_ref[...]))

    x1_ref[...] = x + _dot(merged.astype(BF16), wo_ref[...])


def _mixer(x, o, p, l):
    b, t, _ = x.shape
    ws = [p[k] for k in ("norm_mix_g", "w_in", "w_a_out", "w_b_out", "w_c_out", "w_d_out", "w_o", "b_conv_w",
                         "b_conv_b", "c_conv_w", "c_conv_b", "c_ln_g", "c_ln_b", "d_ln_g", "d_ln_b", "d_ws",
                         "d_bs_t")]
    row = lambda w: pl.BlockSpec((None, _TM, w), lambda i, j: (i, j, 0))
    return pl.pallas_call(
        _mixer_kernel,
        grid=(b, t // _TM),
        in_specs=[row(D_MODEL), row(A_W)] + [_layer(w, l) for w in ws],
        out_specs=[row(D_MODEL),
                   pl.BlockSpec((None, B_K - 1, B_W), lambda i, j: (i, 0, 0)),
                   pl.BlockSpec((None, C_K - 1, C_W), lambda i, j: (i, 0, 0))],
        out_shape=[jax.ShapeDtypeStruct((b, t, D_MODEL), F32),
                   jax.ShapeDtypeStruct((b, B_K - 1, B_W), F32),
                   jax.ShapeDtypeStruct((b, C_K - 1, C_W), F32)],
        scratch_shapes=[pltpu.VMEM((_B_HALO + _TM, B_W), F32), pltpu.VMEM((_C_HALO + _TM, C_W), F32),
                        pltpu.VMEM((_TM, C_W), BF16), pltpu.VMEM((_TM, D_W), BF16),
                        pltpu.VMEM((_SUBLANES, _CONV_ROWS + _C_HALO, C_W), F32),
                        pltpu.VMEM((_TM, N_BRANCH * D_MODEL), F32)],
        compiler_params=_params(("arbitrary", "arbitrary")),
        name="mixer",
    )(x, o, *ws)


def _ffn(x2, gf, wup_ref, wdn_ref):
    h3 = _rms(x2, gf).astype(BF16)
    acc = jnp.zeros(x2.shape, F32)
    for c in range(0, D_FF, _FF_CHUNK):
        a = jnp.maximum(_dot(h3, wup_ref[:, c:c + _FF_CHUNK]), 0.0)
        acc = acc + _dot((a * a).astype(BF16), wdn_ref[c:c + _FF_CHUNK, :])
    return x2 + acc


def _cross_attention(x, mk_ref, mv_ref, gx, wxq_ref, qg, wxo_ref):
    h2 = _rms(x, gx).astype(BF16)
    q = _dot(h2, wxq_ref[...])
    oms = []
    for i in range(X_HEADS):
        sl = slice(i * X_HD, (i + 1) * X_HD)
        qh = (_rms(q[:, sl], qg) * (X_HD ** -0.5)).astype(BF16)
        s = _dot_nt(qh, mk_ref[:, sl])
        p = jnp.exp(s - jnp.max(s, axis=-1, keepdims=True))
        oms.append(_dot(p.astype(BF16), mv_ref[:, sl]) * (1.0 / jnp.sum(p, axis=-1, keepdims=True)))
    om = jnp.concatenate(oms, axis=1).astype(BF16)
    return x + _dot(om, wxo_ref[...])


def _xffn_kernel(x_ref, mk_ref, mv_ref, gx_ref, wxq_ref, qg_ref, wxo_ref, gf_ref, wup_ref, wdn_ref,
                 out_ref):
    x2 = _cross_attention(x_ref[...], mk_ref, mv_ref, gx_ref[...], wxq_ref, qg_ref[...], wxo_ref)
    out_ref[...] = _ffn(x2, gf_ref[...], wup_ref, wdn_ref)


def _xattn_ffn(x, mkb, mvb, p, l):
    b, t, _ = x.shape
    ws = [p[k] for k in ("norm_x_g", "w_xq", "x_qnorm_g", "w_xo", "norm_ffn_g", "w_up", "w_down")]
    n = mkb.shape[1]
    row = pl.BlockSpec((None, _TM, D_MODEL), lambda i, j: (i, j, 0))
    mem = pl.BlockSpec((None, n, X_W), lambda i, j: (i, 0, 0))
    return pl.pallas_call(
        _xffn_kernel,
        grid=(b, t // _TM),
        in_specs=[row, mem, mem] + [_layer(w, l) for w in ws],
        out_specs=row,
        out_shape=jax.ShapeDtypeStruct((b, t, D_MODEL), F32),
        compiler_params=_params(("arbitrary", "arbitrary")),
        name="xattn_ffn",
    )(x, mkb, mvb, *ws)


def _sample_mixer_kernel(x_ref, sb_ref, sc_ref, g_ref, wr_ref, qg_ref, kg_ref,
                         wb_ref, wc_ref, wd_ref, bcw_ref, bcb_ref, ccw_ref, ccb_ref, clg_ref, clb_ref,
                         dlg_ref, dlb_ref, dws_ref, dbs_ref,
                         q_ref, k_ref, v_ref, cb_ref, cc_ref, dvn_ref, part_ref, g0_ref):
    x = x_ref[...]
    h = _rms(x, g_ref[...]).astype(BF16)
    q, k, v = _qkv_from_h(h, wr_ref[:, 0:_QKV_COLS], qg_ref[...], kg_ref[...])
    q_ref[...] = q
    k_ref[...] = k
    v_ref[...] = v

    def gate(i):
        return jnp.tanh(_dot(h, wr_ref[:, _R_GZ + i * D_MODEL:_R_GZ + (i + 1) * D_MODEL]))

    g0_ref[...] = gate(0)

    zb = _dot(h, wr_ref[:, _R_BX:_R_CA])
    u = zb[:, 2 * B_W:3 * B_W] * zb[:, 0:B_W]
    conv_b = (bcw_ref[0:1, :] * sb_ref[:, 0:B_W] + bcw_ref[1:2, :] * sb_ref[:, B_W:2 * B_W]
              + bcw_ref[2:3, :] * u + bcb_ref[...])
    part = _gated(gate(1), _dot((zb[:, B_W:2 * B_W] * conv_b).astype(BF16), wb_ref[...]))
    cb_ref[:, 0:B_W] = sb_ref[:, B_W:2 * B_W]
    cb_ref[:, B_W:2 * B_W] = u

    zc = _dot(h, wr_ref[:, _R_CA:_R_DU])
    uc = zc[:, 0:C_W] * _sigmoid(zc[:, C_W:2 * C_W])
    acc = ccw_ref[C_K - 1:C_K, :] * uc + ccb_ref[...]
    for kk in range(C_K - 1):
        acc = acc + ccw_ref[kk:kk + 1, :] * sc_ref[:, kk * C_W:(kk + 1) * C_W]
    y = _layer_norm(acc, clg_ref[...], clb_ref[...])
    yc = _dot((y * _sigmoid(y)).astype(BF16), wc_ref[...])
    part = part + _gated(gate(2), yc)
    cc_ref[:, 0:(C_K - 2) * C_W] = sc_ref[:, C_W:(C_K - 1) * C_W]
    cc_ref[:, (C_K - 2) * C_W:(C_K - 1) * C_W] = uc

    zd = jax.nn.gelu(_dot(h, wr_ref[:, _R_DU:_R_GZ]))
    dvn = _layer_norm(zd[:, D_W:2 * D_W], dlg_ref[...], dlb_ref[...])
    dvn_ref[...] = dvn
    w00 = sum(dws_ref[g, 0:1, 0:1] * _lane_group_mask(D_W, D_W // D_GROUPS, g) for g in range(D_GROUPS))
    b0 = sum(dbs_ref[g:g + 1, 0:1] * _lane_group_mask(D_W, D_W // D_GROUPS, g) for g in range(D_GROUPS))
    yd = _dot((zd[:, 0:D_W] * (w00 * dvn + b0)).astype(BF16), wd_ref[...])
    part_ref[...] = part + _gated(gate(3), yd)


def _sample_mixer(x, sb, sc, p, l):
    n = x.shape[0]
    shapes = [(n, A_W), (n, A_W), (n, A_W), (n, (B_K - 1) * B_W), (n, (C_K - 1) * C_W), (n, D_W),
              (n, D_MODEL), (n, D_MODEL)]
    ins = [x, sb, sc]
    ws = [p[k] for k in ("norm_mix_g", "w_in", "a_qnorm_g", "a_knorm_g", "w_b_out", "w_c_out", "w_d_out",
                         "b_conv_w", "b_conv_b", "c_conv_w", "c_conv_b", "c_ln_g", "c_ln_b", "d_ln_g", "d_ln_b",
                         "d_ws", "d_bs")]
    return pl.pallas_call(
        _sample_mixer_kernel,
        grid=(1,),
        in_specs=[_full(a.shape) for a in ins] + [_layer(w, l) for w in ws],
        out_specs=[_full(s) for s in shapes],
        out_shape=[jax.ShapeDtypeStruct(s, F32) for s in shapes],
        compiler_params=_params(("arbitrary",)),
        name="sample_mixer",
    )(*ins, *ws)


_DEC_ROWS = 2 * A_HEADS
_PAGE_ROWS = PAGE_SIZE * A_HEADS


def _head_rows(ref, row, per_head):
    return sum(jnp.where(_idiv(row, per_head) == h, jnp.broadcast_to(ref[h:h + 1, :], row.shape), 0.0)
               for h in range(ref.shape[0]))


def _decode_init(q_ref, qz_scr, m_scr, l_scr, acc_scr):
    row = lax.broadcasted_iota(jnp.int32, (_DEC_ROWS, A_VD), 0)
    lane = lax.broadcasted_iota(jnp.int32, (_DEC_ROWS, A_VD), 1)
    qz_scr[...] = jnp.where(_idiv(lane, A_HD) == (row & 1), _head_rows(q_ref, row, 2), 0.0)
    m_scr[...] = jnp.full(m_scr.shape, NEG_INF, F32)
    l_scr[...] = jnp.zeros(l_scr.shape, F32)
    acc_scr[...] = jnp.zeros(acc_scr.shape, F32)


def _decode_pages(j, k_refs, v_refs, qz_scr, m_scr, l_scr, acc_scr, past_len):
    n_pages = len(k_refs)
    width = n_pages * _PAGE_ROWS
    rowv = lax.broadcasted_iota(jnp.int32, (_DEC_ROWS, 1), 0)
    slope = sum(jnp.where(_idiv(rowv, 2) == i, 2.0 ** (-8.0 * (i + 1) / A_HEADS), 0.0)
                for i in range(A_HEADS))
    qz = qz_scr[...].astype(BF16)
    s = jnp.concatenate([_dot_nt(qz, k_refs[i][...].astype(BF16)) for i in range(n_pages)], axis=1)
    col = lax.broadcasted_iota(jnp.int32, (_DEC_ROWS, width), 1)
    kpos = j * (n_pages * PAGE_SIZE) + _idiv(lax.broadcasted_iota(jnp.int32, (1, width), 1), A_HEADS)
    own_head = (col & (A_HEADS - 1)) == _idiv(lax.broadcasted_iota(jnp.int32, (_DEC_ROWS, width), 0), 2)
    s = jnp.where(own_head, s - slope * (past_len - kpos).astype(F32), NEG_INF)
    m_old = m_scr[...]
    m_new = jnp.maximum(m_old, jnp.max(s, axis=-1, keepdims=True))
    alpha = jnp.exp(m_old - m_new)
    p = jnp.exp(s - m_new)
    l_scr[...] = alpha * l_scr[...] + jnp.sum(p, axis=-1, keepdims=True)
    p = p.astype(BF16)
    pv = sum(_dot(p[:, i * _PAGE_ROWS:(i + 1) * _PAGE_ROWS], v_refs[i][...].astype(BF16))
             for i in range(n_pages))
    acc_scr[...] = alpha * acc_scr[...] + pv
    m_scr[...] = m_new


def _decode_finish(kn_ref, vn_ref, lam_ref, sg_ref, o_ref, qz_scr, m_scr, l_scr, acc_scr, lam_init):
    row = lax.broadcasted_iota(jnp.int32, (_DEC_ROWS, A_VD), 0)
    s_new = jnp.sum(qz_scr[...] * _head_rows(kn_ref, row, 2), axis=-1, keepdims=True)
    m_old = m_scr[...]
    m_fin = jnp.maximum(m_old, s_new)
    alpha = jnp.exp(m_old - m_fin)
    p_new = jnp.exp(s_new - m_fin)
    l_fin = alpha * l_scr[...] + p_new
    o_all = (alpha * acc_scr[...] + p_new * _head_rows(vn_ref, row, 2)) / l_fin
    lam = _lam(lam_ref, lam_init)
    sg = sg_ref[...]
    for h in range(A_HEADS):
        o = o_all[2 * h:2 * h + 1] - lam * o_all[2 * h + 1:2 * h + 2]
        o_ref[h:h + 1, :] = _rms(o, sg) * (1.0 - lam_init)


def _decode_attn_kernel(pt_ref, q_ref, kn_ref, vn_ref, lam_ref, sg_ref, *rest, lam_init, past_len):
    del pt_ref
    k_refs = rest[:_DEC_PAGES]
    v_refs = rest[_DEC_PAGES:2 * _DEC_PAGES]
    o_ref = rest[2 * _DEC_PAGES]
    dec = rest[2 * _DEC_PAGES + 1:]
    j = pl.program_id(1)

    @pl.when(j == 0)
    def _():
        _decode_init(q_ref, *dec)

    _decode_pages(j, k_refs, v_refs, *dec, past_len)

    @pl.when(j == pl.num_programs(1) - 1)
    def _():
        _decode_finish(kn_ref, vn_ref, lam_ref, sg_ref, o_ref, *dec, lam_init)


def _decode_attention(page_table, q, k_new, v_new, cache_k, cache_v, p, layer, lam_init):
    n, n_pages = page_table.shape
    past_len = n_pages * PAGE_SIZE
    row = pl.BlockSpec((None, A_HEADS, A_VD), lambda b, j, pt: (b, 0, 0))

    def page(i):
        return pl.BlockSpec((None, None, _PAGE_ROWS, A_VD),
                            lambda b, j, pt: (layer, pt[b, j * _DEC_PAGES + i], 0, 0))

    grid_spec = pltpu.PrefetchScalarGridSpec(
        num_scalar_prefetch=1,
        grid=(n, n_pages // _DEC_PAGES),
        in_specs=[row, row, row, _layer(p["a_lam"], layer), _layer(p["a_subln_g"], layer)]
        + [page(i) for i in range(_DEC_PAGES)] + [page(i) for i in range(_DEC_PAGES)],
        out_specs=row,
        scratch_shapes=[pltpu.VMEM((_DEC_ROWS, A_VD), F32), pltpu.VMEM((_DEC_ROWS, 1), F32),
                        pltpu.VMEM((_DEC_ROWS, 1), F32), pltpu.VMEM((_DEC_ROWS, A_VD), F32)],
    )
    heads = lambda a: a.reshape(n, A_HEADS, A_VD)
    return pl.pallas_call(
        functools.partial(_decode_attn_kernel, lam_init=lam_init, past_len=past_len),
        grid_spec=grid_spec,
        out_shape=jax.ShapeDtypeStruct((n, A_HEADS, A_VD), F32),
        compiler_params=_params(("arbitrary", "arbitrary")),
        name="decode_attention",
    )(page_table, heads(q), heads(k_new), heads(v_new), p["a_lam"], p["a_subln_g"],
      *([cache_k] * _DEC_PAGES), *([cache_v] * _DEC_PAGES))


def _sample_tail_kernel(x_ref, o_ref, g0_ref, part_ref, mk_ref, mv_ref, wa_ref, wo_ref, gx_ref, wxq_ref,
                        qg_ref, wxo_ref, gf_ref, wup_ref, wdn_ref, out_ref, x1_scr, q_scr, om_scr):
    b = pl.program_id(0)
    nb = pl.num_programs(0)

    @pl.when(b == 0)
    def _():
        merged = _gated(g0_ref[...], _dot(o_ref[...].astype(BF16), wa_ref[...])) + part_ref[...]
        x1 = x_ref[...] + _dot(merged.astype(BF16), wo_ref[...])
        x1_scr[...] = x1
        q = _dot(_rms(x1, gx_ref[...]).astype(BF16), wxq_ref[...])
        qg = qg_ref[...]
        for h in range(X_HEADS):
            q_scr[h] = _rms(q[:, h * X_HD:(h + 1) * X_HD], qg) * (X_HD ** -0.5)

    n_rows = mk_ref.shape[1]
    row = lax.broadcasted_iota(jnp.int32, (_SUBLANES, X_HD), 0)
    col = lax.broadcasted_iota(jnp.int32, (_SUBLANES, n_rows), 1)
    own_head = (col & (X_HEADS - 1)) == lax.broadcasted_iota(jnp.int32, (_SUBLANES, n_rows), 0)
    for i in range(_TAIL_SEQS):
        seq = b * _TAIL_SEQS + i
        qz = sum(jnp.where(row == h, jnp.broadcast_to(q_scr[h, pl.ds(seq, 1), :], (_SUBLANES, X_HD)), 0.0)
                 for h in range(X_HEADS)).astype(BF16)
        s = jnp.where(own_head, _dot_nt(qz, mk_ref[i].astype(BF16)), NEG_INF)
        p = jnp.exp(s - jnp.max(s, axis=-1, keepdims=True))
        p = p / jnp.sum(p, axis=-1, keepdims=True)
        om = _dot(p.astype(BF16), mv_ref[i].astype(BF16))
        for h in range(X_HEADS):
            om_scr[h, pl.ds(seq, 1), :] = om[h:h + 1, :]

    @pl.when(b == nb - 1)
    def _():
        x2 = x1_scr[...] + sum(_dot(om_scr[h].astype(BF16), wxo_ref[h * X_HD:(h + 1) * X_HD, :])
                               for h in range(X_HEADS))
        out_ref[...] = _ffn(x2, gf_ref[...], wup_ref, wdn_ref)


def _sample_tail(x, o, g0, part, mem_k, mem_v, layer, p):
    n = x.shape[0]
    n_rows = mem_k.shape[2]
    assert n % _TAIL_SEQS == 0
    mem = pl.BlockSpec((None, _TAIL_SEQS, n_rows, X_HD), lambda b: (layer, b, 0, 0))
    ins = [x, o, g0, part]
    ws = [p[k] for k in ("w_a_out", "w_o", "norm_x_g", "w_xq", "x_qnorm_g", "w_xo", "norm_ffn_g", "w_up",
                         "w_down")]
    return pl.pallas_call(
        _sample_tail_kernel,
        grid=(n // _TAIL_SEQS,),
        in_specs=[_full(a.shape) for a in ins] + [mem, mem] + [_layer(w, layer) for w in ws],
        out_specs=_full((n, D_MODEL)),
        out_shape=jax.ShapeDtypeStruct((n, D_MODEL), F32),
        scratch_shapes=[pltpu.VMEM((n, D_MODEL), F32), pltpu.VMEM((X_HEADS, n, X_HD), F32),
                        pltpu.VMEM((X_HEADS, n, X_HD), F32)],
        compiler_params=_params(("arbitrary",)),
        name="sample_tail",
    )(*ins, mem_k, mem_v, *ws)


def _stacked_params(a):
    row = lambda v: v.reshape(v.shape[0], 1, -1)
    b16 = lambda w: w.astype(BF16)
    tiled = lambda v: row(jnp.tile(v, (1, A_W // A_HD)))
    out = {k: b16(a[k]) for k in ("w_o", "w_xq", "w_xk", "w_xv", "w_xo", "w_up", "w_down")}
    halve_gates = jnp.where(jnp.arange(IN_COLS) >= _R_GZ, 0.5, 1.0).astype(F32)
    out["w_in"] = b16(a["w_in"] * halve_gates)
    out.update({k: b16(0.5 * a[k]) for k in ("w_a_out", "w_b_out", "w_c_out", "w_d_out")})
    out.update({k: row(a[k]) for k in ("norm_mix_g", "a_subln_g", "b_conv_b", "c_conv_b", "c_ln_g", "c_ln_b",
                                       "d_ln_g", "d_ln_b", "norm_x_g", "mem_norm_g", "x_qnorm_g", "x_knorm_g",
                                       "norm_ffn_g")})
    out.update({k: a[k] for k in ("a_lam", "b_conv_w", "c_conv_w", "d_ws", "d_bs")})
    out.update(a_qnorm_g=tiled(a["a_qnorm_g"]), a_knorm_g=tiled(a["a_knorm_g"]),
               d_bs_t=jnp.swapaxes(a["d_bs"], 1, 2))
    return out


def kernel(x_prompt, x_sample, cache_k_a, cache_v_a, state_conv_b, state_conv_c, cache_mem_k, cache_mem_v,
           page_table, mem_prompt, norm_mix_g, w_in, a_qnorm_g, a_knorm_g, a_lam, a_subln_g, w_a_out,
           b_conv_w, b_conv_b, w_b_out, c_conv_w, c_conv_b, c_ln_g, c_ln_b, w_c_out, d_ln_g, d_ln_b,
           d_ws, d_bs, w_d_out, w_o, norm_x_g, mem_norm_g, w_xq, w_xk, w_xv, x_qnorm_g, x_knorm_g,
           w_xo, norm_ffn_g, w_up, w_down):
    weights = dict(norm_mix_g=norm_mix_g, w_in=w_in, a_qnorm_g=a_qnorm_g, a_knorm_g=a_knorm_g, a_lam=a_lam,
                   a_subln_g=a_subln_g, w_a_out=w_a_out, b_conv_w=b_conv_w, b_conv_b=b_conv_b,
                   w_b_out=w_b_out, c_conv_w=c_conv_w, c_conv_b=c_conv_b, c_ln_g=c_ln_g, c_ln_b=c_ln_b,
                   w_c_out=w_c_out, d_ln_g=d_ln_g, d_ln_b=d_ln_b, d_ws=d_ws, d_bs=d_bs, w_d_out=w_d_out,
                   w_o=w_o, norm_x_g=norm_x_g, mem_norm_g=mem_norm_g, w_xq=w_xq, w_xk=w_xk, w_xv=w_xv,
                   x_qnorm_g=x_qnorm_g, x_knorm_g=x_knorm_g, w_xo=w_xo, norm_ffn_g=norm_ffn_g,
                   w_up=w_up, w_down=w_down)
    depth = w_in.shape[0]
    bp, t, _ = x_prompt.shape
    ns = x_sample.shape[0]
    n_pool = cache_k_a.shape[1]
    n_mem = cache_mem_k.shape[2]
    cache_k = cache_k_a.reshape(depth, n_pool, _PAGE_ROWS, A_VD)
    cache_v = cache_v_a.reshape(depth, n_pool, _PAGE_ROWS, A_VD)
    mem_k_s = cache_mem_k.reshape(depth, ns, n_mem * X_HEADS, X_HD)
    mem_v_s = cache_mem_v.reshape(depth, ns, n_mem * X_HEADS, X_HD)
    slopes = jnp.asarray([2.0 ** (-8.0 * (i + 1) / A_HEADS) for i in range(A_HEADS)], F32)

    xp = x_prompt
    xs = x_sample.reshape(ns, D_MODEL)
    outs = [[] for _ in range(9)]
    p = _stacked_params(weights)
    for l in range(depth):
        lam_init = 0.8 - 0.6 * math.exp(-0.3 * l)

        mk, mv, mkb, mvb = _memory_kv(mem_prompt, p, l)
        k_p, v_p, qb, kb, vb = _qkv_proj(xp, p, l, depth, None if l == 0 else (k_p, v_p))
        o_p = _diff_attention(slopes, qb, kb, vb, p, l, lam_init)
        xp, cb_p, cc_p = _mixer(xp, o_p, p, l)
        xp = _xattn_ffn(xp, mkb, mvb, p, l)

        sb = state_conv_b[l].reshape(ns, (B_K - 1) * B_W)
        sc = state_conv_c[l].reshape(ns, (C_K - 1) * C_W)
        q_s, k_s, v_s, cb_s, cc_s, dvn_s, part, g0 = _sample_mixer(xs, sb, sc, p, l)
        o_s = _decode_attention(page_table, q_s, k_s, v_s, cache_k, cache_v, p, l, lam_init)
        xs = _sample_tail(xs, o_s.reshape(ns, A_W), g0, part, mem_k_s, mem_v_s, l, p)

        for lst, val in zip(outs, (
                cb_p, cc_p, mk, mv,
                k_s.reshape(ns, 1, A_HEADS, 2 * A_HD), v_s.reshape(ns, 1, A_HEADS, A_VD),
                cb_s.reshape(ns, B_K - 1, B_W), cc_s.reshape(ns, C_K - 1, C_W),
                dvn_s.reshape(ns, 1, D_W))):
            lst.append(val)
    return (xp, xs.reshape(ns, 1, D_MODEL), k_p, v_p) + tuple(jnp.stack(o) for o in outs)
```
